```python
import math
import jax
import jax.numpy as jnp
from jax import lax
import numpy as np

D_MODEL = 1024
BATCH = 2
SEQ = 16384
DEPTH = 2

GRID_W = 64
CTX_LEN = 256
N_EVEN = (DEPTH + 1) // 2
N_ODD = DEPTH // 2
EPS = 1e-6
N_MOD = 6

GROUP_DIM = 64
HY_WIDTH = 3 * D_MODEL // 4
FN_WIDTH = D_MODEL - HY_WIDTH
FN_GROUPS = FN_WIDTH // GROUP_DIM
EVEN_IN = 3 * HY_WIDTH + FN_WIDTH
HY_EMB_DIM = 33
HY_FILTER_HIDDEN = 64
HY_DECAY_TARGET = 1e-2
HY_FAST_DECAY_PCT = 0.3
HY_SLOW_DECAY_PCT = 1.5
D_FF = ((8 * D_MODEL // 3 + 255) // 256) * 256

HEAD_DIM = 64
N_Q_HEADS = (D_MODEL // 2) // HEAD_DIM
N_KV_HEADS = 2
Q_PER_KV = N_Q_HEADS // N_KV_HEADS
ATT_WIDTH = N_Q_HEADS * HEAD_DIM
KV_WIDTH = N_KV_HEADS * HEAD_DIM
SC_WIDTH = D_MODEL - ATT_WIDTH
ODD_IN = ATT_WIDTH + 2 * KV_WIDTH + 3 * SC_WIDTH
WINDOW = 128
BLOCK = 128
ROPE_BASE = 10000.0
N_EXPERTS = 8
TOP_K = 2
D_FF_EXPERT = 7 * D_MODEL // 2
NEG_INF = -1e30

kernel_name = "hybrid_hyena_fnet_shortconv_swa_moe_dit"


def rms_norm(x, g):
    xf = x.astype(jnp.float32)
    y = xf * lax.rsqrt(jnp.mean(xf * xf, axis=-1, keepdims=True) + EPS)
    return (y * g.astype(jnp.float32)).astype(x.dtype)


def ada_params(cond, w, b):
    m = jax.nn.silu(cond) @ w + b
    return jnp.split(m[..., None, :], N_MOD, axis=-1)


def modulate(x, g, shift, scale):
    return rms_norm(x, g) * (1 + scale) + shift


def short_conv(u, w):
    L = u.shape[1]
    up = jnp.pad(u, ((0, 0), (1, 1), (0, 0)))
    return up[:, :L] * w[0] + up[:, 1:L + 1] * w[1] + up[:, 2:] * w[2]


def swiglu(h, wg, wu, wd):
    return (jax.nn.silu(h @ wg) * (h @ wu)) @ wd


def hyena_kernel(L, w1, b1, freq, w2, b2, w3):
    f32 = jnp.float32
    t = jnp.linspace(0.0, 1.0, L, dtype=f32)[:, None]
    bands = (HY_EMB_DIM - 1) // 2
    ang = (2.0 * math.pi / L) * jnp.arange(L, dtype=f32)[:, None] * jnp.linspace(1e-4, bands - 1, bands, dtype=f32)[None, :]
    z = jnp.concatenate([t, jnp.cos(ang), -jnp.sin(ang)], axis=-1)
    fr = freq.astype(f32)
    h = jnp.sin(fr * (z @ w1.astype(f32) + b1.astype(f32)))
    h = jnp.sin(fr * (h @ w2.astype(f32) + b2.astype(f32)))
    h = h @ w3.astype(f32)
    max_decay = math.log(HY_DECAY_TARGET) / HY_FAST_DECAY_PCT
    min_decay = math.log(HY_DECAY_TARGET) / HY_SLOW_DECAY_PCT
    deltas = jnp.abs(jnp.linspace(min_decay, max_decay, HY_WIDTH, dtype=f32))
    decay = jnp.exp(-t * deltas[None, :])
    h_fwd = h[:, :HY_WIDTH] * decay
    h_bwd = h[1:, HY_WIDTH:] * decay[1:]
    l1 = jnp.sum(jnp.abs(h_fwd), axis=0) + jnp.sum(jnp.abs(h_bwd), axis=0)
    kern = jnp.concatenate([h_fwd, jnp.zeros((1, HY_WIDTH), f32), h_bwd[::-1]], axis=0)
    return kern / l1


def fft_conv_bidir(u, kern):
    L = u.shape[1]
    n = 2 * L
    u_f = jnp.fft.rfft(u.astype(jnp.float32), n=n, axis=1)
    k_f = jnp.fft.rfft(kern, n=n, axis=0)
    return jnp.fft.irfft(u_f * k_f[None], n=n, axis=1)[:, :L]


def hyena_mix(p, short_w, w1, b1, freq, w2, b2, w3, skip):
    L = p.shape[1]
    p = short_conv(p, short_w)
    x0, x1, v = jnp.split(p, 3, axis=-1)
    u = v * x1
    kern = hyena_kernel(L, w1, b1, freq, w2, b2, w3)
    y = fft_conv_bidir(u, kern).astype(u.dtype) + u * skip
    return x0 * y


def fourier_mix(f):
    B, L, _ = f.shape
    fg = f.astype(jnp.float32).reshape(B, L, FN_GROUPS, GROUP_DIM)
    y = jnp.fft.fft2(fg, axes=(1, 3), norm="ortho").real
    return y.reshape(B, L, FN_WIDTH).astype(f.dtype)


def axial_rope(x, rows, cols):
    half = HEAD_DIM // 2
    quarter = half // 2
    inv_freq = ROPE_BASE ** (-jnp.arange(quarter, dtype=jnp.float32) / quarter)
    xf = x.astype(jnp.float32)

    def rotate(xa, pos):
        ang = pos.astype(jnp.float32)[:, None] * inv_freq[None, :]
        cos = jnp.cos(ang)[None, :, None, :]
        sin = jnp.sin(ang)[None, :, None, :]
        a1, a2 = xa[..., :quarter], xa[..., quarter:]
        return jnp.concatenate([a1 * cos - a2 * sin, a2 * cos + a1 * sin], axis=-1)

    out = jnp.concatenate([rotate(xf[..., :half], rows), rotate(xf[..., half:], cols)], axis=-1)
    return out.astype(x.dtype)


def window_attention(q, k, v, kc, vc, sink):
    B, L = q.shape[0], q.shape[1]
    n_blocks = L // BLOCK
    qg = q.reshape(B, L, N_KV_HEADS, Q_PER_KV, HEAD_DIM)
    kp = jnp.pad(k, ((0, 0), (BLOCK, BLOCK), (0, 0), (0, 0)))
    vp = jnp.pad(v, ((0, 0), (BLOCK, BLOCK), (0, 0), (0, 0)))
    scale = HEAD_DIM ** -0.5
    a_idx = jnp.arange(BLOCK)[:, None]
    b_idx = jnp.arange(3 * BLOCK)[None, :]
    band = jnp.abs(b_idx - BLOCK - a_idx) <= WINDOW
    sink_l = sink.astype(jnp.float32).reshape(1, N_KV_HEADS, Q_PER_KV, 1, 1)
    n_ctx = kc.shape[1]

    def attend_block(i):
        start = i * BLOCK
        qb = lax.dynamic_slice_in_dim(qg, start, BLOCK, axis=1)
        kb = lax.dynamic_slice_in_dim(kp, start, 3 * BLOCK, axis=1)
        vb = lax.dynamic_slice_in_dim(vp, start, 3 * BLOCK, axis=1)
        kpos = start - BLOCK + jnp.arange(3 * BLOCK)
        mask = band & ((kpos >= 0) & (kpos < L))[None, :]
        s_loc = jnp.einsum("bqhgd,bshd->bhgqs", qb, kb).astype(jnp.float32) * scale
        s_loc = jnp.where(mask, s_loc, NEG_INF)
        s_ctx = jnp.einsum("bqhgd,bchd->bhgqc", qb, kc).astype(jnp.float32) * scale
        s_sink = jnp.broadcast_to(sink_l, (B, N_KV_HEADS, Q_PER_KV, BLOCK, 1))
        p = jax.nn.softmax(jnp.concatenate([s_loc, s_ctx, s_sink], axis=-1), axis=-1)
        p_loc = p[..., :3 * BLOCK].astype(v.dtype)
        p_ctx = p[..., 3 * BLOCK:3 * BLOCK + n_ctx].astype(v.dtype)
        return jnp.einsum("bhgqs,bshd->bqhgd", p_loc, vb) + jnp.einsum("bhgqc,bchd->bqhgd", p_ctx, vc)

    out = lax.map(attend_block, jnp.arange(n_blocks))
    return jnp.moveaxis(out, 0, 1).reshape(B, L, ATT_WIDTH)


def context_attention(qc, kc, vc, sink):
    B, C = qc.shape[0], qc.shape[1]
    qg = qc.reshape(B, C, N_KV_HEADS, Q_PER_KV, HEAD_DIM)
    s = jnp.einsum("bqhgd,bshd->bhgqs", qg, kc).astype(jnp.float32) * HEAD_DIM ** -0.5
    s_sink = jnp.broadcast_to(sink.astype(jnp.float32).reshape(1, N_KV_HEADS, Q_PER_KV, 1, 1), (B, N_KV_HEADS, Q_PER_KV, C, 1))
    p = jax.nn.softmax(jnp.concatenate([s, s_sink], axis=-1), axis=-1)[..., :C].astype(vc.dtype)
    return jnp.einsum("bhgqs,bshd->bqhgd", p, vc).reshape(B, C, ATT_WIDTH)


def short_gated_conv_mix(s, conv_w):
    gate_b, gate_c, hx = jnp.split(s, 3, axis=-1)
    return gate_b * short_conv(gate_c * hx, conv_w)


def moe_swiglu(h, router_w, router_b, wg, wu, wd):
    logits = (h @ router_w + router_b).astype(jnp.float32)
    top_val, top_idx = lax.top_k(logits, TOP_K)
    gates = jax.nn.softmax(top_val, axis=-1)
    combine = jnp.einsum("blk,blke->ble", gates, jax.nn.one_hot(top_idx, N_EXPERTS, dtype=jnp.float32)).astype(h.dtype)
    out = jnp.zeros_like(h)
    for e in range(N_EXPERTS):
        out = out + combine[..., e:e + 1] * swiglu(h, wg[e], wu[e], wd[e])
    return out


def even_layer(x, ctx, mod_x, mod_c, norm1, norm2, w_in, hy_short, hy_w1, hy_b1, hy_freq, hy_w2, hy_b2, hy_w3, hy_skip, w_out, ffn_wg, ffn_wu, ffn_wd):
    def run(s, mod):
        sh1, sc1, g1, sh2, sc2, g2 = mod
        p = modulate(s, norm1, sh1, sc1) @ w_in
        y_h = hyena_mix(p[..., :3 * HY_WIDTH], hy_short, hy_w1, hy_b1, hy_freq, hy_w2, hy_b2, hy_w3, hy_skip)
        y_f = fourier_mix(p[..., 3 * HY_WIDTH:])
        s = s + g1 * (jnp.concatenate([y_h, y_f], axis=-1) @ w_out)
        return s + g2 * swiglu(modulate(s, norm2, sh2, sc2), ffn_wg, ffn_wu, ffn_wd)

    x_new = run(x, mod_x)
    ctx_new = run(ctx, mod_c) if mod_c is not None else None
    return x_new, ctx_new


def odd_layer(x, ctx, mod_x, mod_c, rows, cols, need_ctx, norm1, norm2, w_in, q_norm, k_norm, sink, sc_conv, w_out, router_w, router_b, moe_wg, moe_wu, moe_wd):
    B, L, _ = x.shape
    n_ctx = ctx.shape[1]
    csh1, csc1, cg1, csh2, csc2, cg2 = mod_c
    hc = modulate(ctx, norm1, csh1, csc1)
    if need_ctx:
        pc = hc @ w_in
        qc, kc, vc, sc_c = jnp.split(pc, [ATT_WIDTH, ATT_WIDTH + KV_WIDTH, ATT_WIDTH + 2 * KV_WIDTH], axis=-1)
    else:
        kvc = hc @ w_in[:, ATT_WIDTH:ATT_WIDTH + 2 * KV_WIDTH]
        kc, vc = jnp.split(kvc, 2, axis=-1)
    kc = rms_norm(kc.reshape(B, n_ctx, N_KV_HEADS, HEAD_DIM), k_norm)
    vc = vc.reshape(B, n_ctx, N_KV_HEADS, HEAD_DIM)

    sh1, sc1, g1, sh2, sc2, g2 = mod_x
    p = modulate(x, norm1, sh1, sc1) @ w_in
    q, k, v, s = jnp.split(p, [ATT_WIDTH, ATT_WIDTH + KV_WIDTH, ATT_WIDTH + 2 * KV_WIDTH], axis=-1)
    q = axial_rope(rms_norm(q.reshape(B, L, N_Q_HEADS, HEAD_DIM), q_norm), rows, cols)
    k = axial_rope(rms_norm(k.reshape(B, L, N_KV_HEADS, HEAD_DIM), k_norm), rows, cols)
    v = v.reshape(B, L, N_KV_HEADS, HEAD_DIM)
    y_att = window_attention(q, k, v, kc, vc, sink)
    y_sc = short_gated_conv_mix(s, sc_conv)
    x = x + g1 * (jnp.concatenate([y_att, y_sc], axis=-1) @ w_out)
    x = x + g2 * moe_swiglu(modulate(x, norm2, sh2, sc2), router_w, router_b, moe_wg, moe_wu, moe_wd)

    if need_ctx:
        qc = rms_norm(qc.reshape(B, n_ctx, N_Q_HEADS, HEAD_DIM), q_norm)
        yc = jnp.concatenate([context_attention(qc, kc, vc, sink), short_gated_conv_mix(sc_c, sc_conv)], axis=-1) @ w_out
        ctx = ctx + cg1 * yc
        ctx = ctx + cg2 * moe_swiglu(modulate(ctx, norm2, csh2, csc2), router_w, router_b, moe_wg, moe_wu, moe_wd)
    else:
        ctx = None
    return x, ctx


def setup_inputs(seed: int = 0) -> dict:
    key = jax.random.key(seed)
    ks = iter(jax.random.split(key, 48))
    f32 = jnp.float32
    D = D_MODEL
    NE, NO = N_EVEN, N_ODD

    def nrm(shape, scale):
        return jax.random.normal(next(ks), shape, f32) * scale

    def gain(shape):
        return 1.0 + nrm(shape, 0.02)

    return {
        "x": nrm((BATCH, SEQ, D), 1.0),
        "c": nrm((BATCH, D), 1.0),
        "ctx": nrm((BATCH, CTX_LEN, D), 1.0),
        "c_ctx": nrm((D,), 1.0),
        "e_ada_w": nrm((NE, D, N_MOD * D), D ** -0.5),
        "e_ada_b": nrm((NE, N_MOD * D), 0.02),
        "e_norm1": gain((NE, D)),
        "e_norm2": gain((NE, D)),
        "e_w_in": nrm((NE, D, EVEN_IN), D ** -0.5),
        "e_hy_short": nrm((NE, 3, 3 * HY_WIDTH), 3 ** -0.5),
        "e_hy_w1": nrm((NE, HY_EMB_DIM, HY_FILTER_HIDDEN), HY_EMB_DIM ** -0.5),
        "e_hy_b1": nrm((NE, HY_FILTER_HIDDEN), 0.02),
        "e_hy_freq": gain((NE, HY_FILTER_HIDDEN)),
        "e_hy_w2": nrm((NE, HY_FILTER_HIDDEN, HY_FILTER_HIDDEN), HY_FILTER_HIDDEN ** -0.5),
        "e_hy_b2": nrm((NE, HY_FILTER_HIDDEN), 0.02),
        "e_hy_w3": nrm((NE, HY_FILTER_HIDDEN, 2 * HY_WIDTH), HY_FILTER_HIDDEN ** -0.5),
        "e_hy_skip": nrm((NE, HY_WIDTH), 1.0),
        "e_w_out": nrm((NE, D, D), D ** -0.5),
        "e_ffn_wg": nrm((NE, D, D_FF), D ** -0.5),
        "e_ffn_wu": nrm((NE, D, D_FF), D ** -0.5),
        "e_ffn_wd": nrm((NE, D_FF, D), D_FF ** -0.5),
        "o_ada_w": nrm((NO, D, N_MOD * D), D ** -0.5),
        "o_ada_b": nrm((NO, N_MOD * D), 0.02),
        "o_norm1": gain((NO, D)),
        "o_norm2": gain((NO, D)),
        "o_w_in": nrm((NO, D, ODD_IN), D ** -0.5),
        "o_q_norm": gain((NO, HEAD_DIM)),
        "o_k_norm": gain((NO, HEAD_DIM)),
        "o_sink": nrm((NO, N_Q_HEADS), 0.5),
        "o_sc_conv": nrm((NO, 3, SC_WIDTH), 3 ** -0.5),
        "o_w_out": nrm((NO, D, D), D ** -0.5),
        "o_router_w": nrm((NO, D, N_EXPERTS), D ** -0.5),
        "o_router_b": nrm((NO, N_EXPERTS), 0.01),
        "o_moe_wg": nrm((NO, N_EXPERTS, D, D_FF_EXPERT), D ** -0.5),
        "o_moe_wu": nrm((NO, N_EXPERTS, D, D_FF_EXPERT), D ** -0.5),
        "o_moe_wd": nrm((NO, N_EXPERTS, D_FF_EXPERT, D), D_FF_EXPERT ** -0.5),
    }


def reference(x, c, ctx, c_ctx,
              e_ada_w, e_ada_b, e_norm1, e_norm2, e_w_in, e_hy_short, e_hy_w1, e_hy_b1, e_hy_freq,
              e_hy_w2, e_hy_b2, e_hy_w3, e_hy_skip, e_w_out, e_ffn_wg, e_ffn_wu, e_ffn_wd,
              o_ada_w, o_ada_b, o_norm1, o_norm2, o_w_in, o_q_norm, o_k_norm, o_sink, o_sc_conv,
              o_w_out, o_router_w, o_router_b, o_moe_wg, o_moe_wu, o_moe_wd):
    L = x.shape[1]
    n_rows = L // GRID_W
    rows = jnp.repeat(jnp.arange(n_rows, dtype=jnp.int32), GRID_W)
    cols = jnp.tile(jnp.arange(GRID_W, dtype=jnp.int32), n_rows)
    for layer in range(DEPTH):
        j = layer // 2
        need_ctx = layer < DEPTH - 1
        if layer % 2 == 0:
            mod_x = ada_params(c, e_ada_w[j], e_ada_b[j])
            mod_c = ada_params(c_ctx, e_ada_w[j], e_ada_b[j]) if need_ctx else None
            x, ctx = even_layer(x, ctx, mod_x, mod_c, e_norm1[j], e_norm2[j], e_w_in[j], e_hy_short[j],
                                e_hy_w1[j], e_hy_b1[j], e_hy_freq[j], e_hy_w2[j], e_hy_b2[j], e_hy_w3[j],
                                e_hy_skip[j], e_w_out[j], e_ffn_wg[j], e_ffn_wu[j], e_ffn_wd[j])
        else:
            mod_x = ada_params(c, o_ada_w[j], o_ada_b[j])
            mod_c = ada_params(c_ctx, o_ada_w[j], o_ada_b[j])
            x, ctx = odd_layer(x, ctx, mod_x, mod_c, rows, cols, need_ctx, o_norm1[j], o_norm2[j], o_w_in[j],
                               o_q_norm[j], o_k_norm[j], o_sink[j], o_sc_conv[j], o_w_out[j],
                               o_router_w[j], o_router_b[j], o_moe_wg[j], o_moe_wu[j], o_moe_wd[j])
    return x
```

```python
import functools
import math

import jax
import jax.numpy as jnp
from jax import lax
from jax.experimental import pallas as pl
from jax.experimental.pallas import tpu as pltpu

f32 = jnp.float32
bf16 = jnp.bfloat16
SDS = jax.ShapeDtypeStruct

EPS = 1e-6
N_MOD = 6
GROUP_DIM = 64
HEAD_DIM = 64
N_Q_HEADS = 8
N_KV_HEADS = 2
Q_PER_KV = N_Q_HEADS // N_KV_HEADS
WINDOW = 128
GRID_W = 64
ROPE_BASE = 10000.0
N_EXPERTS = 8
NEG_INF = -1e30
HY_EMB_DIM = 33
HY_DECAY_TARGET = 1e-2
HY_FAST_DECAY_PCT = 0.3
HY_SLOW_DECAY_PCT = 1.5

LANES = 128
DFT_N2 = 128
K1_PER_STEP = 8
VMEM_LIMIT = 56 * 1024 * 1024


def _cparams(*sem):
    return pltpu.CompilerParams(dimension_semantics=sem, vmem_limit_bytes=VMEM_LIMIT)


def _dot(a, b, **kw):
    return jnp.dot(a, b, preferred_element_type=f32, **kw)


def _full_spec(shape):
    nd = len(shape)
    return pl.BlockSpec(shape, lambda *_: (0,) * nd)


def _ada_kernel(c_ref, w_ref, b_ref, o_ref):
    c = c_ref[...]
    s = (c * jax.nn.sigmoid(c)).astype(bf16)
    o_ref[...] = _dot(s, w_ref[...].astype(bf16)) + b_ref[...]


def _ada(cond, w, b):
    d, n = w.shape
    tn = n // 4
    return pl.pallas_call(
        _ada_kernel,
        out_shape=SDS((cond.shape[0], n), f32),
        grid=(n // tn,),
        in_specs=[
            pl.BlockSpec(cond.shape, lambda j: (0, 0)),
            pl.BlockSpec((d, tn), lambda j: (0, j)),
            pl.BlockSpec((1, tn), lambda j: (0, j)),
        ],
        out_specs=pl.BlockSpec((cond.shape[0], tn), lambda j: (0, j)),
        compiler_params=_cparams("arbitrary"),
        name="ada",
    )(cond, w, b.reshape(1, n))


def _modulated(x, a, sh):
    ms = jnp.mean(x * x, axis=-1, keepdims=True)
    return (x * lax.rsqrt(ms + EPS)) * a + sh


def _inproj_kernel(x_ref, a_ref, sh_ref, w_ref, *o_refs, splits):
    h = _modulated(x_ref[0], a_ref[0], sh_ref[0]).astype(bf16)
    for o_ref, (s, e) in zip(o_refs, splits):
        o_ref[0] = _dot(h, w_ref[:, s:e]).astype(o_ref.dtype)


def _inproj(x, a, sh, w, splits, tm):
    b, l, d = x.shape
    n = w.shape[1]
    return pl.pallas_call(
        functools.partial(_inproj_kernel, splits=splits),
        out_shape=[SDS((b, l, e - s), bf16) for s, e in splits],
        grid=(b, l // tm),
        in_specs=[
            pl.BlockSpec((1, tm, d), lambda i, j: (i, j, 0)),
            pl.BlockSpec((1, 1, d), lambda i, j: (i, 0, 0)),
            pl.BlockSpec((1, 1, d), lambda i, j: (i, 0, 0)),
            pl.BlockSpec((d, n), lambda i, j: (0, 0)),
        ],
        out_specs=[pl.BlockSpec((1, tm, e - s), lambda i, j: (i, j, 0)) for s, e in splits],
        compiler_params=_cparams("parallel", "parallel"),
        name="inproj",
    )(x, a, sh, w)


def _norm_rope(t, seg, gain, cos, sin_a, sin_b):
    ms = _dot((t * t).astype(bf16), seg)
    tn = t * lax.rsqrt(ms + EPS) * gain
    w = t.shape[1]
    reps = w // cos.shape[1]
    cos = jnp.concatenate([cos] * reps, axis=1)
    sin_a = jnp.concatenate([sin_a] * reps, axis=1)
    sin_b = jnp.concatenate([sin_b] * reps, axis=1)
    quarter = HEAD_DIM // 4
    return tn * cos + pltpu.roll(tn, w - quarter, 1) * sin_a + pltpu.roll(tn, quarter, 1) * sin_b


def _inproj_odd_kernel(x_ref, a_ref, sh_ref, w_ref, cos_ref, sa_ref, sb_ref, gq_ref, gk_ref,
                       segq_ref, segk_ref, q_ref, k_ref, v_ref, s_ref, *, wq, wk):
    h = _modulated(x_ref[0], a_ref[0], sh_ref[0]).astype(bf16)
    cos, sa, sb = cos_ref[...], sa_ref[...], sb_ref[...]
    q = _norm_rope(_dot(h, w_ref[:, :wq]), segq_ref[...], gq_ref[...], cos, sa, sb)
    q_ref[0] = (q * HEAD_DIM ** -0.5).astype(bf16)
    k = _norm_rope(_dot(h, w_ref[:, wq:wq + wk]), segk_ref[...], gk_ref[...], cos, sa, sb)
    k_ref[0] = k.astype(bf16)
    v_ref[0] = _dot(h, w_ref[:, wq + wk:wq + 2 * wk]).astype(bf16)
    s_ref[0] = _dot(h, w_ref[:, wq + 2 * wk:]).astype(bf16)


def _inproj_odd(x, a, sh, w, tables, gq, gk, tm):
    b, l, d = x.shape
    n = w.shape[1]
    wq, wk = N_Q_HEADS * HEAD_DIM, N_KV_HEADS * HEAD_DIM
    ws = n - wq - 2 * wk
    cos, sa, sb = tables
    seg = lambda width: jnp.kron(jnp.eye(width // HEAD_DIM, dtype=f32),
                                 jnp.full((HEAD_DIM, HEAD_DIM), 1.0 / HEAD_DIM, f32)).astype(bf16)
    row = lambda i, j: (i, j, 0)
    tab = pl.BlockSpec((tm, LANES), lambda i, j: (j, 0))
    return pl.pallas_call(
        functools.partial(_inproj_odd_kernel, wq=wq, wk=wk),
        out_shape=[SDS((b, l, wq), bf16), SDS((b, l, wk), bf16), SDS((b, l, wk), bf16), SDS((b, l, ws), bf16)],
        grid=(b, l // tm),
        in_specs=[
            pl.BlockSpec((1, tm, d), row),
            pl.BlockSpec((1, 1, d), lambda i, j: (i, 0, 0)),
            pl.BlockSpec((1, 1, d), lambda i, j: (i, 0, 0)),
            pl.BlockSpec((d, n), lambda i, j: (0, 0)),
            tab, tab, tab,
            _full_spec((1, wq)), _full_spec((1, wk)), _full_spec((wq, wq)), _full_spec((wk, wk)),
        ],
        out_specs=[pl.BlockSpec((1, tm, wq), row), pl.BlockSpec((1, tm, wk), row),
                   pl.BlockSpec((1, tm, wk), row), pl.BlockSpec((1, tm, ws), row)],
        compiler_params=_cparams("parallel", "parallel"),
        name="inproj_odd",
    )(x, a, sh, w, cos, sa, sb, jnp.tile(gq, wq // HEAD_DIM)[None], jnp.tile(gk, wk // HEAD_DIM)[None],
      seg(wq), seg(wk))


HALO = 16


def _conv3(p, prev_row, next_row, w):
    tl = p.shape[0]
    r = lax.broadcasted_iota(jnp.int32, p.shape, 0)
    pm1 = jnp.where(r == 0, prev_row, pltpu.roll(p, 1, 0))
    pp1 = jnp.where(r == tl - 1, next_row, pltpu.roll(p, tl - 1, 0))
    return pm1 * w[0:1] + p * w[1:2] + pp1 * w[2:3]


def _halo_rows(pp_ref, pn_ref):
    i, n = pl.program_id(1), pl.num_programs(1)
    prev = pp_ref[0].astype(f32)[HALO - 1:HALO]
    nxt = pn_ref[0].astype(f32)[0:1]
    return jnp.where(i > 0, prev, 0.0), jnp.where(i < n - 1, nxt, 0.0)


def _hyprep_kernel(p_ref, pp_ref, pn_ref, w_ref, u_ref, x0_ref, *, width):
    prev, nxt = _halo_rows(pp_ref, pn_ref)
    c = _conv3(p_ref[0].astype(f32), prev, nxt, w_ref[...])
    x0_ref[0] = c[:, :width].astype(bf16)
    u_ref[0] = (c[:, 2 * width:] * c[:, width:2 * width]).astype(bf16)


def _sgconv_kernel(s_ref, sp_ref, sn_ref, w_ref, o_ref, *, width):
    def inner(t):
        return t[:, width:2 * width] * t[:, 2 * width:]
    i, n = pl.program_id(1), pl.num_programs(1)
    s = s_ref[0].astype(f32)
    prev = jnp.where(i > 0, inner(sp_ref[0].astype(f32))[HALO - 1:HALO], 0.0)
    nxt = jnp.where(i < n - 1, inner(sn_ref[0].astype(f32))[0:1], 0.0)
    o_ref[0] = (s[:, :width] * _conv3(inner(s), prev, nxt, w_ref[...])).astype(bf16)


def _conv_call(kernel, p, w, out_widths, tl, name):
    b, l, c = p.shape
    r = tl // HALO
    nh = l // HALO
    row = lambda i, j: (i, j, 0)
    return pl.pallas_call(
        kernel,
        out_shape=[SDS((b, l, ow), bf16) for ow in out_widths],
        grid=(b, l // tl),
        in_specs=[
            pl.BlockSpec((1, tl, c), row),
            pl.BlockSpec((1, HALO, c), lambda i, j: (i, jnp.maximum(j * r - 1, 0), 0)),
            pl.BlockSpec((1, HALO, c), lambda i, j: (i, jnp.minimum((j + 1) * r, nh - 1), 0)),
            _full_spec(w.shape),
        ],
        out_specs=[pl.BlockSpec((1, tl, ow), row) for ow in out_widths],
        compiler_params=_cparams("parallel", "parallel"),
        name=name,
    )(p, p, p, w)


def _lmm_kernel(w_ref, x_ref, o_ref):
    o_ref[0] = _dot(w_ref[...], x_ref[0]).astype(o_ref.dtype)


def _lmm(w, x, tn, out_dtype=bf16):
    m, k = w.shape
    b, _, n = x.shape
    return pl.pallas_call(
        _lmm_kernel,
        out_shape=SDS((b, m, n), out_dtype),
        grid=(b, n // tn),
        in_specs=[_full_spec((m, k)), pl.BlockSpec((1, k, tn), lambda i, j: (i, 0, j))],
        out_specs=pl.BlockSpec((1, m, tn), lambda i, j: (i, 0, j)),
        compiler_params=_cparams("parallel", "parallel"),
        name="lmm",
    )(w, x)


def _filt_kernel(z_ref, t_ref, w1_ref, b1_ref, fr_ref, w2_ref, b2_ref, w3_ref, dl_ref, hfb_ref, l1_ref, *, width):
    i = pl.program_id(0)
    hp = lax.Precision.HIGHEST
    fr = fr_ref[...]
    h = jnp.sin(fr * (_dot(z_ref[...], w1_ref[...], precision=hp) + b1_ref[...]))
    h = jnp.sin(fr * (_dot(h, w2_ref[...], precision=hp) + b2_ref[...]))
    h = _dot(h.astype(bf16), w3_ref[...])
    decay = jnp.exp(-t_ref[...] * dl_ref[...])
    hf = h[:, :width] * decay
    hb = h[:, width:] * decay
    tl = hf.shape[0]
    row = lax.broadcasted_iota(jnp.int32, hb.shape, 0) + i * tl
    hb = jnp.where(row == 0, 0.0, hb)
    hfb_ref[:, :width] = hf.astype(bf16)
    hfb_ref[:, width:] = hb.astype(bf16)
    part = jnp.sum(jnp.abs(hf), axis=0, keepdims=True) + jnp.sum(jnp.abs(hb), axis=0, keepdims=True)

    @pl.when(i == 0)
    def _():
        l1_ref[...] = jnp.zeros_like(l1_ref)

    l1_ref[...] += part


def _hyena_filter(l, w1, b1, freq, w2, b2, w3, tl):
    hid = w1.shape[1]
    width = w3.shape[1] // 2
    t = jnp.linspace(0.0, 1.0, l, dtype=f32)[:, None]
    bands = (HY_EMB_DIM - 1) // 2
    ang = (2.0 * math.pi / l) * jnp.arange(l, dtype=f32)[:, None] * jnp.linspace(1e-4, bands - 1, bands, dtype=f32)[None, :]
    z = jnp.concatenate([t, jnp.cos(ang), -jnp.sin(ang)], axis=-1)
    z = jnp.pad(z, ((0, 0), (0, LANES - HY_EMB_DIM)))
    pad_h = LANES - hid
    w1p = jnp.pad(w1.astype(f32), ((0, LANES - HY_EMB_DIM), (0, pad_h)))
    b1p = jnp.pad(b1.astype(f32), (0, pad_h))[None]
    frp = jnp.pad(freq.astype(f32), (0, pad_h))[None]
    w2p = jnp.pad(w2.astype(f32), ((0, pad_h), (0, pad_h)))
    b2p = jnp.pad(b2.astype(f32), (0, pad_h))[None]
    w3p = jnp.pad(w3, ((0, pad_h), (0, 0))).astype(bf16)
    max_decay = math.log(HY_DECAY_TARGET) / HY_FAST_DECAY_PCT
    min_decay = math.log(HY_DECAY_TARGET) / HY_SLOW_DECAY_PCT
    deltas = jnp.abs(jnp.linspace(min_decay, max_decay, width, dtype=f32))[None]
    return pl.pallas_call(
        functools.partial(_filt_kernel, width=width),
        out_shape=[SDS((l, 2 * width), bf16), SDS((1, width), f32)],
        grid=(l // tl,),
        in_specs=[
            pl.BlockSpec((tl, LANES), lambda i: (i, 0)),
            pl.BlockSpec((tl, 1), lambda i: (i, 0)),
            _full_spec(w1p.shape), _full_spec(b1p.shape), _full_spec(frp.shape),
            _full_spec(w2p.shape), _full_spec(b2p.shape), _full_spec(w3p.shape), _full_spec(deltas.shape),
        ],
        out_specs=[pl.BlockSpec((tl, 2 * width), lambda i: (i, 0)), pl.BlockSpec((1, width), lambda i: (0, 0))],
        compiler_params=_cparams("arbitrary"),
        name="hyena_filter",
    )(z, t, w1p, b1p, frp, w2p, b2p, w3p, deltas)


def _cs(num, den):
    ang = (2.0 * math.pi / den) * (num % den).astype(f32)
    return jnp.cos(ang), jnp.sin(ang)


def _outer(n_rows, n_cols):
    return jnp.arange(n_rows, dtype=jnp.int32)[:, None] * jnp.arange(n_cols, dtype=jnp.int32)[None, :]


def _interleave_rows(a, b):
    return jnp.stack([a, b], axis=1).reshape(2 * a.shape[0], a.shape[1])


def _interleave_cols(a, b):
    return jnp.stack([a, b], axis=2).reshape(a.shape[0], 2 * a.shape[1])


def _inner_tables(n1, scale):
    c2, s2 = _cs(_outer(DFT_N2, DFT_N2), DFT_N2)
    ct, st = _cs(_outer(n1, DFT_N2), n1 * DFT_N2)
    return c2 * scale, -s2 * scale, ct[:, None, :], -st[:, None, :]


def _gmat(f2r, f2i, tr, ti):
    return f2r * tr - f2i * ti, f2r * ti + f2i * tr


def _gblock(gr, gi):
    return jnp.concatenate([jnp.concatenate([gr, -gi], axis=1), jnp.concatenate([gi, gr], axis=1)], axis=0)


def _s2f_kernel(a_ref, f2r_ref, f2i_ref, tr_ref, ti_ref, k_ref, *, width):
    f2r, f2i = f2r_ref[...], f2i_ref[...]
    h2 = DFT_N2
    for j in range(K1_PER_STEP):
        gr, gi = _gmat(f2r, f2i, tr_ref[j], ti_ref[j])
        hh = _dot(_gblock(gr, gi).astype(bf16), a_ref[0, j])
        k_ref[j, :h2, :] = hh[:h2, :width] + hh[:h2, width:]
        k_ref[j, h2:, :] = hh[h2:, :width] - hh[h2:, width:]


def _s23_kernel(a_ref, kh_ref, f2r_ref, f2i_ref, tr_ref, ti_ref, o_ref):
    f2r, f2i = f2r_ref[...], f2i_ref[...]
    h2 = DFT_N2
    for j in range(K1_PER_STEP):
        gr, gi = _gmat(f2r, f2i, tr_ref[j], ti_ref[j])
        uh = _dot(_gblock(gr, gi).astype(bf16), a_ref[0, j])
        ur, ui = uh[:h2], uh[h2:]
        kr, ki = kh_ref[j, :h2, :], kh_ref[j, h2:, :]
        yh = jnp.concatenate([ur * kr - ui * ki, ur * ki + ui * kr], axis=0).astype(bf16)
        grt, git = gr.T, gi.T
        o_ref[0, j] = _dot(_gblock(grt, -git).astype(bf16), yh).astype(o_ref.dtype)


def _s4_kernel(f4_ref, bq_ref, u_ref, x0_ref, sc_ref, sk_ref, o_ref):
    y = _dot(f4_ref[...], bq_ref[0])
    u = u_ref[0].astype(f32)
    o_ref[0] = (x0_ref[0].astype(f32) * (y * sc_ref[...] + u * sk_ref[...])).astype(o_ref.dtype)


def _s4(f4, bq, u, x0, scale, skip, tn):
    m, k = f4.shape
    b, _, n = bq.shape
    reps = tn // scale.shape[1]
    col = lambda i, j: (i, 0, j)
    return pl.pallas_call(
        _s4_kernel,
        out_shape=SDS((b, m, n), bf16),
        grid=(b, n // tn),
        in_specs=[_full_spec((m, k)), pl.BlockSpec((1, k, tn), col), pl.BlockSpec((1, m, tn), col),
                  pl.BlockSpec((1, m, tn), col), _full_spec((1, tn)), _full_spec((1, tn))],
        out_specs=pl.BlockSpec((1, m, tn), col),
        compiler_params=_cparams("parallel", "parallel"),
        name="hyena_s4",
    )(f4, bq, u, x0, jnp.tile(scale, (1, reps)), jnp.tile(skip, (1, reps)))


def _hyena_long(u, x0, hfb, l1, skip):
    b, l, w = u.shape
    n2 = DFT_N2
    half = l // n2
    n1 = 2 * half
    kb = K1_PER_STEP
    c1, s1 = _cs(_outer(n1, half), n1)
    f1 = _interleave_rows(c1, -s1).astype(bf16)
    f2r, f2i, tr, ti = _inner_tables(n1, 1.0)
    tab_specs = [_full_spec((n2, n2)), _full_spec((n2, n2)),
                 pl.BlockSpec((kb, 1, n2), lambda i, *_: (i, 0, 0)), pl.BlockSpec((kb, 1, n2), lambda i, *_: (i, 0, 0))]

    a_f = _lmm(f1, hfb.reshape(1, half, n2 * 2 * w), tn=4 * 2 * w).reshape(1, n1, 2 * n2, 2 * w)
    kh = pl.pallas_call(
        functools.partial(_s2f_kernel, width=w),
        out_shape=SDS((n1, 2 * n2, w), f32),
        grid=(n1 // kb,),
        in_specs=[pl.BlockSpec((1, kb, 2 * n2, 2 * w), lambda i: (0, i, 0, 0))] + tab_specs,
        out_specs=pl.BlockSpec((kb, 2 * n2, w), lambda i: (i, 0, 0)),
        compiler_params=_cparams("parallel"),
        name="hyena_s2f",
    )(a_f, f2r, f2i, tr, ti)

    a_u = _lmm(f1, u.reshape(b, half, n2 * w), tn=8 * w).reshape(b, n1, 2 * n2, w)
    bq = pl.pallas_call(
        _s23_kernel,
        out_shape=SDS((b, n1, 2 * n2, w), bf16),
        grid=(n1 // kb, b),
        in_specs=[pl.BlockSpec((1, kb, 2 * n2, w), lambda i, j: (j, i, 0, 0)),
                  pl.BlockSpec((kb, 2 * n2, w), lambda i, j: (i, 0, 0))] + tab_specs,
        out_specs=pl.BlockSpec((1, kb, 2 * n2, w), lambda i, j: (j, i, 0, 0)),
        compiler_params=_cparams("parallel", "parallel"),
        name="hyena_s23",
    )(a_u, kh, f2r, f2i, tr, ti)

    c4, s4 = _cs(_outer(half, n1), n1)
    f4 = (_interleave_cols(c4, -s4) * (1.0 / (n1 * n2))).astype(bf16)
    y = _s4(f4, bq.reshape(b, 2 * n1, n2 * w), u.reshape(b, half, n2 * w), x0.reshape(b, half, n2 * w),
            1.0 / l1, skip[None], tn=8 * w)
    return y.reshape(b, l, w)


def _cmul_kernel(uh_ref, hh_ref, o_ref, *, width, nf):
    ur, ui = uh_ref[0, :nf, :].astype(f32), uh_ref[0, nf:, :].astype(f32)
    hh = hh_ref[0].astype(f32)
    kr = hh[:nf, :width] + hh[:nf, width:]
    ki = hh[nf:, :width] - hh[nf:, width:]
    o_ref[0, :nf, :] = (ur * kr - ui * ki).astype(o_ref.dtype)
    o_ref[0, nf:, :] = (ur * ki + ui * kr).astype(o_ref.dtype)


def _hyena_short(u, x0, hfb, l1, skip):
    b, l, w = u.shape
    nf = 2 * l
    c, s = _cs(_outer(nf, l), nf)
    fwd = jnp.concatenate([c, -s], axis=0).astype(bf16)
    uh = _lmm(fwd, u, tn=w, out_dtype=f32)
    hh = _lmm(fwd, hfb[None], tn=2 * w, out_dtype=f32)
    yh = pl.pallas_call(
        functools.partial(_cmul_kernel, width=w, nf=nf),
        out_shape=SDS((b, 2 * nf, w), bf16),
        grid=(b,),
        in_specs=[pl.BlockSpec((1, 2 * nf, w), lambda i: (i, 0, 0)), _full_spec((1, 2 * nf, 2 * w))],
        out_specs=pl.BlockSpec((1, 2 * nf, w), lambda i: (i, 0, 0)),
        compiler_params=_cparams("parallel"),
        name="hyena_cmul",
    )(uh, hh)
    ci, si = _cs(_outer(l, nf), nf)
    inv = (jnp.concatenate([ci, -si], axis=1) * (1.0 / nf)).astype(bf16)
    return _s4(inv, yh, u, x0, 1.0 / l1, skip[None], tn=w)


def _f2_kernel(a_ref, f2r_ref, f2i_ref, tr_ref, ti_ref, o_ref, *, width):
    f2r, f2i = f2r_ref[...], f2i_ref[...]
    for j in range(K1_PER_STEP):
        gr, gi = _gmat(f2r, f2i, tr_ref[j], ti_ref[j])
        gcat = jnp.concatenate([gr, gi], axis=1).astype(bf16)
        s0 = a_ref[0, j, 0].astype(f32)
        s1 = a_ref[0, j, 1].astype(f32)
        x2 = jnp.concatenate([s0[:, :width] - s1[:, width:], s0[:, width:] + s1[:, :width]], axis=0).astype(bf16)
        o_ref[0, :, j, :] = _dot(gcat, x2).astype(o_ref.dtype)


def _fnet_long(pq):
    b, l, w2 = pq.shape
    w = w2 // 2
    n2 = DFT_N2
    n1 = l // n2
    kb = K1_PER_STEP
    c1, s1 = _cs(_outer(n1, n1), n1)
    f1 = _interleave_rows(c1, s1).astype(bf16)
    a = _lmm(f1, pq.reshape(b, n1, n2 * w2), tn=16 * w2).reshape(b, n1, 2, n2, w2)
    f2r, f2i, tr, ti = _inner_tables(n1, 1.0 / math.sqrt(l))
    out = pl.pallas_call(
        functools.partial(_f2_kernel, width=w),
        out_shape=SDS((b, n2, n1, w), bf16),
        grid=(b, n1 // kb),
        in_specs=[pl.BlockSpec((1, kb, 2, n2, w2), lambda i, j: (i, j, 0, 0, 0)),
                  _full_spec((n2, n2)), _full_spec((n2, n2)),
                  pl.BlockSpec((kb, 1, n2), lambda i, j: (j, 0, 0)), pl.BlockSpec((kb, 1, n2), lambda i, j: (j, 0, 0))],
        out_specs=pl.BlockSpec((1, n2, kb, w), lambda i, j: (i, 0, j, 0)),
        compiler_params=_cparams("parallel", "parallel"),
        name="fnet_f2",
    )(a, f2r, f2i, tr, ti)
    return out.reshape(b, l, w)


def _fnet_short(pq):
    b, l, w2 = pq.shape
    w = w2 // 2
    c, s = _cs(_outer(l, l), l)
    m = (jnp.concatenate([c, -s], axis=1) * (1.0 / math.sqrt(l))).astype(bf16)
    x = jnp.concatenate([pq[..., :w], pq[..., w:]], axis=1)
    return _lmm(m, x, tn=w)


def _mm32_kernel(a_ref, b_ref, o_ref):
    o_ref[...] = _dot(a_ref[...], b_ref[...], precision=lax.Precision.HIGHEST)


def _mm32(a, b):
    return pl.pallas_call(
        _mm32_kernel,
        out_shape=SDS((a.shape[0], b.shape[1]), f32),
        in_specs=[_full_spec(a.shape), _full_spec(b.shape)],
        out_specs=_full_spec((a.shape[0], b.shape[1])),
        grid=(1,),
        compiler_params=_cparams("arbitrary"),
        name="mm32",
    )(a, b)


def _outproj_kernel(x_ref, ya_ref, yb_ref, wa_ref, wb_ref, g_ref, o_ref):
    acc = _dot(ya_ref[0], wa_ref[...]) + _dot(yb_ref[0], wb_ref[...])
    o_ref[0] = x_ref[0] + g_ref[0] * acc


def _outproj(x, ya, yb, w, gate, tm):
    b, l, d = x.shape
    ka, kb_ = ya.shape[2], yb.shape[2]
    row = lambda i, j: (i, j, 0)
    return pl.pallas_call(
        _outproj_kernel,
        out_shape=SDS((b, l, d), f32),
        grid=(b, l // tm),
        in_specs=[pl.BlockSpec((1, tm, d), row), pl.BlockSpec((1, tm, ka), row), pl.BlockSpec((1, tm, kb_), row),
                  _full_spec((ka, d)), _full_spec((kb_, d)), pl.BlockSpec((1, 1, d), lambda i, j: (i, 0, 0))],
        out_specs=pl.BlockSpec((1, tm, d), row),
        compiler_params=_cparams("parallel", "parallel"),
        name="outproj",
    )(x, ya, yb, w[:ka], w[ka:], gate)


def _ffn_kernel(x_ref, a_ref, sh_ref, g_ref, wg_ref, wu_ref, wd_ref, o_ref, h_ref, acc_ref):
    f = pl.program_id(2)

    @pl.when(f == 0)
    def _():
        h_ref[...] = _modulated(x_ref[0], a_ref[0], sh_ref[0]).astype(bf16)
        acc_ref[...] = jnp.zeros_like(acc_ref)

    h = h_ref[...]
    g = _dot(h, wg_ref[...])
    u = _dot(h, wu_ref[...])
    acc_ref[...] += _dot((g * jax.nn.sigmoid(g) * u).astype(bf16), wd_ref[...])

    @pl.when(f == pl.num_programs(2) - 1)
    def _():
        o_ref[0] = x_ref[0] + g_ref[0] * acc_ref[...]


def _ffn(x, a, sh, gate, wg, wu, wd, tm, tf):
    b, l, d = x.shape
    ff = wg.shape[1]
    row = lambda i, j, f: (i, j, 0)
    vec = lambda i, j, f: (i, 0, 0)
    return pl.pallas_call(
        _ffn_kernel,
        out_shape=SDS((b, l, d), f32),
        grid=(b, l // tm, ff // tf),
        in_specs=[pl.BlockSpec((1, tm, d), row), pl.BlockSpec((1, 1, d), vec), pl.BlockSpec((1, 1, d), vec),
                  pl.BlockSpec((1, 1, d), vec),
                  pl.BlockSpec((d, tf), lambda i, j, f: (0, f)), pl.BlockSpec((d, tf), lambda i, j, f: (0, f)),
                  pl.BlockSpec((tf, d), lambda i, j, f: (f, 0))],
        out_specs=pl.BlockSpec((1, tm, d), row),
        scratch_shapes=[pltpu.VMEM((tm, d), bf16), pltpu.VMEM((tm, d), f32)],
        compiler_params=_cparams("parallel", "parallel", "arbitrary"),
        name="ffn",
    )(x, a, sh, gate, wg, wu, wd)


def _attn_kernel(sink_ref, q_ref, kp_ref, kc_ref, kn_ref, vp_ref, vc_ref, vn_ref, ck_ref, cv_ref, o_ref):
    i, n = pl.program_id(1), pl.num_programs(1)
    tq = q_ref.shape[1]
    n_ctx = ck_ref.shape[1]
    n_loc = 3 * tq
    r = lax.broadcasted_iota(jnp.int32, (tq, n_loc + n_ctx), 0)
    c = lax.broadcasted_iota(jnp.int32, (tq, n_loc + n_ctx), 1)
    lo = jnp.where(i > 0, 0, tq)
    hi = jnp.where(i < n - 1, n_loc, 2 * tq)
    mask = ((c >= r) & (c <= r + 2 * WINDOW) & (c >= lo) & (c < hi)) | (c >= n_loc)
    keys = jnp.concatenate([kp_ref[0], kc_ref[0], kn_ref[0], ck_ref[0]], axis=0)
    vals = jnp.concatenate([vp_ref[0], vc_ref[0], vn_ref[0], cv_ref[0]], axis=0)
    q = q_ref[0]
    outs = []
    for hq in range(N_Q_HEADS):
        hk = hq // Q_PER_KV
        qh = q[:, hq * HEAD_DIM:(hq + 1) * HEAD_DIM]
        kh = keys[:, hk * HEAD_DIM:(hk + 1) * HEAD_DIM]
        vh = vals[:, hk * HEAD_DIM:(hk + 1) * HEAD_DIM]
        s = lax.dot_general(qh, kh, (((1,), (1,)), ((), ())), preferred_element_type=f32)
        s = jnp.where(mask, s, NEG_INF)
        sink = sink_ref[hq]
        m = jnp.maximum(jnp.max(s, axis=-1, keepdims=True), sink)
        p = jnp.exp(s - m)
        denom = jnp.sum(p, axis=-1, keepdims=True) + jnp.exp(sink - m)
        outs.append(_dot(p.astype(bf16), vh) / denom)
    o_ref[0] = jnp.concatenate(outs, axis=1).astype(o_ref.dtype)


def _attention(q, k, v, kc, vc, sink):
    b, l, wq = q.shape
    wk = k.shape[2]
    n_ctx = kc.shape[1]
    tq = WINDOW
    nb = l // tq
    cur = lambda i, j: (i, j, 0)
    prv = lambda i, j: (i, jnp.maximum(j - 1, 0), 0)
    nxt = lambda i, j: (i, jnp.minimum(j + 1, nb - 1), 0)
    kv = lambda f: pl.BlockSpec((1, tq, wk), f)
    ctx = pl.BlockSpec((1, n_ctx, wk), lambda i, j: (i, 0, 0))
    return pl.pallas_call(
        _attn_kernel,
        out_shape=SDS((b, l, wq), bf16),
        grid=(b, nb),
        in_specs=[pl.BlockSpec(memory_space=pltpu.SMEM), pl.BlockSpec((1, tq, wq), cur),
                  kv(prv), kv(cur), kv(nxt), kv(prv), kv(cur), kv(nxt), ctx, ctx],
        out_specs=pl.BlockSpec((1, tq, wq), cur),
        compiler_params=_cparams("parallel", "parallel"),
        name="attention",
    )(sink.astype(f32), q, k, k, k, v, v, v, kc, vc)


def _router_kernel(x_ref, a_ref, sh_ref, rw_ref, rb_ref, h_ref, cw_ref):
    h = _modulated(x_ref[0], a_ref[0], sh_ref[0])
    h_ref[0] = h.astype(bf16)
    logits = _dot(h, rw_ref[...], precision=lax.Precision.HIGHEST) + rb_ref[...]
    lane = lax.broadcasted_iota(jnp.int32, logits.shape, 1).astype(f32)
    m1 = jnp.max(logits, axis=-1, keepdims=True)
    i1 = jnp.min(jnp.where(logits == m1, lane, float(LANES)), axis=-1, keepdims=True)
    rest = jnp.where(lane == i1, NEG_INF, logits)
    m2 = jnp.max(rest, axis=-1, keepdims=True)
    i2 = jnp.min(jnp.where(rest == m2, lane, float(LANES)), axis=-1, keepdims=True)
    e2 = jnp.exp(m2 - m1)
    g1 = 1.0 / (1.0 + e2)
    cw_ref[0] = jnp.where(lane == i1, g1, 0.0) + jnp.where(lane == i2, e2 * g1, 0.0)


def _router(x, a, sh, rw, rb, tm):
    b, l, d = x.shape
    e = rw.shape[1]
    rwp = jnp.pad(rw.astype(f32), ((0, 0), (0, LANES - e)))
    rbp = jnp.pad(rb.astype(f32), (0, LANES - e), constant_values=NEG_INF)[None]
    row = lambda i, j: (i, j, 0)
    vec = lambda i, j: (i, 0, 0)
    return pl.pallas_call(
        _router_kernel,
        out_shape=[SDS((b, l, d), bf16), SDS((b, l, LANES), f32)],
        grid=(b, l // tm),
        in_specs=[pl.BlockSpec((1, tm, d), row), pl.BlockSpec((1, 1, d), vec), pl.BlockSpec((1, 1, d), vec),
                  _full_spec(rwp.shape), _full_spec(rbp.shape)],
        out_specs=[pl.BlockSpec((1, tm, d), row), pl.BlockSpec((1, tm, LANES), row)],
        compiler_params=_cparams("parallel", "parallel"),
        name="router",
    )(x, a, sh, rwp, rbp)


def _moe_kernel(h_ref, cw_ref, x_ref, g_ref, wg_ref, wu_ref, wd_ref, o_ref, acc_ref):
    e, f = pl.program_id(2), pl.program_id(3)

    @pl.when((e == 0) & (f == 0))
    def _():
        acc_ref[...] = jnp.zeros_like(acc_ref)

    h = h_ref[0]
    g = _dot(h, wg_ref[0].astype(bf16))
    u = _dot(h, wu_ref[0].astype(bf16))
    cw = cw_ref[0]
    lane = lax.broadcasted_iota(jnp.int32, cw.shape, 1)
    cwe = jnp.sum(jnp.where(lane == e, cw, 0.0), axis=-1, keepdims=True)
    act = (g * jax.nn.sigmoid(g) * u * cwe).astype(bf16)
    acc_ref[...] += _dot(act, wd_ref[0].astype(bf16))

    @pl.when((e == pl.num_programs(2) - 1) & (f == pl.num_programs(3) - 1))
    def _():
        o_ref[0] = x_ref[0] + g_ref[0] * acc_ref[...]


def _moe(h, cw, x, gate, wg, wu, wd, tm, tf):
    b, l, d = x.shape
    ne, _, ff = wg.shape
    row = lambda i, j, e, f: (i, j, 0)
    return pl.pallas_call(
        _moe_kernel,
        out_shape=SDS((b, l, d), f32),
        grid=(b, l // tm, ne, ff // tf),
        in_specs=[pl.BlockSpec((1, tm, d), row), pl.BlockSpec((1, tm, LANES), row), pl.BlockSpec((1, tm, d), row),
                  pl.BlockSpec((1, 1, d), lambda i, j, e, f: (i, 0, 0)),
                  pl.BlockSpec((1, d, tf), lambda i, j, e, f: (e, 0, f)),
                  pl.BlockSpec((1, d, tf), lambda i, j, e, f: (e, 0, f)),
                  pl.BlockSpec((1, tf, d), lambda i, j, e, f: (e, f, 0))],
        out_specs=pl.BlockSpec((1, tm, d), row),
        scratch_shapes=[pltpu.VMEM((tm, d), f32)],
        compiler_params=_cparams("parallel", "parallel", "arbitrary", "arbitrary"),
        name="moe",
    )(h, cw, x, gate, wg, wu, wd)


def _mod_params(c, c_ctx, w, b):
    nb = c.shape[0]
    cond = jnp.concatenate([c, c_ctx[None], jnp.zeros((8 - nb - 1, c.shape[1]), f32)], axis=0)
    m = _ada(cond, w, b)
    return [p[:, None, :] for p in jnp.split(m, N_MOD, axis=-1)], nb


def _even_tokens(s, a1, sh1, g1, a2, sh2, g2, w_in, hy_short, filt, hy_skip, w_out, wg, wu, wd, tm, tp, long):
    hw = hy_skip.shape[0]
    ph, pq = _inproj(s, a1, sh1, w_in, [(0, 3 * hw), (3 * hw, w_in.shape[1])], tp)
    u, x0 = _conv_call(functools.partial(_hyprep_kernel, width=hw), ph, hy_short, [hw, hw], tp, "hyena_prep")
    hfb, l1 = filt
    if long:
        y_h = _hyena_long(u, x0, hfb, l1, hy_skip)
        y_f = _fnet_long(pq)
    else:
        y_h = _hyena_short(u, x0, hfb, l1, hy_skip)
        y_f = _fnet_short(pq)
    s = _outproj(s, y_h, y_f, w_out, g1, tm)
    return _ffn(s, a2, sh2, g2, wg, wu, wd, tp, wg.shape[1] // 2)


def _rope_tables(l):
    quarter = HEAD_DIM // 4
    t = jnp.arange(l, dtype=jnp.int32)
    rows = (t // GRID_W).astype(f32)[:, None]
    cols = (t % GRID_W).astype(f32)[:, None]
    inv_freq = ROPE_BASE ** (-jnp.arange(quarter, dtype=f32) / quarter)
    ar, ac = rows * inv_freq[None, :], cols * inv_freq[None, :]
    zero = jnp.zeros_like(ar)
    head = lambda x, y: jnp.concatenate([x, y], axis=1)
    cos = head(head(jnp.cos(ar), jnp.cos(ar)), head(jnp.cos(ac), jnp.cos(ac)))
    sin_a = head(head(-jnp.sin(ar), zero), head(-jnp.sin(ac), zero))
    sin_b = head(head(zero, jnp.sin(ar)), head(zero, jnp.sin(ac)))
    two = lambda x: jnp.concatenate([x, x], axis=1)
    return two(cos), two(sin_a), two(sin_b)


def kernel(x, c, ctx, c_ctx, e_ada_w, e_ada_b, e_norm1, e_norm2, e_w_in, e_hy_short, e_hy_w1, e_hy_b1, e_hy_freq, e_hy_w2, e_hy_b2, e_hy_w3, e_hy_skip, e_w_out, e_ffn_wg, e_ffn_wu, e_ffn_wd, o_ada_w, o_ada_b, o_norm1, o_norm2, o_w_in, o_q_norm, o_k_norm, o_sink, o_sc_conv, o_w_out, o_router_w, o_router_b, o_moe_wg, o_moe_wu, o_moe_wd):
    b, l, d = x.shape
    n_ctx = ctx.shape[1]
    tm = 1024
    tp = 512

    (sh1, sc1, g1, sh2, sc2, g2), nb = _mod_params(c, c_ctx, e_ada_w[0], e_ada_b[0])
    a1 = e_norm1[0] * (1.0 + sc1)
    a2 = e_norm2[0] * (1.0 + sc2)
    hw = e_hy_skip.shape[1]
    fw = d - hw
    cg, sg = _cs(_outer(GROUP_DIM, GROUP_DIM), GROUP_DIM)
    eye = jnp.eye(fw // GROUP_DIM, dtype=f32)
    chan = jnp.concatenate([jnp.kron(eye, cg), jnp.kron(eye, sg)], axis=1) * (1.0 / math.sqrt(GROUP_DIM))
    w_in = jnp.concatenate([e_w_in[0][:, :3 * hw], _mm32(e_w_in[0][:, 3 * hw:], chan)], axis=1).astype(bf16)
    w_out = e_w_out[0].astype(bf16)
    wg, wu, wd = e_ffn_wg[0].astype(bf16), e_ffn_wu[0].astype(bf16), e_ffn_wd[0].astype(bf16)
    filt_args = (e_hy_w1[0], e_hy_b1[0], e_hy_freq[0], e_hy_w2[0], e_hy_b2[0], e_hy_w3[0])
    lat = lambda p: p[:nb]
    cx = lambda p: jnp.broadcast_to(p[nb:nb + 1], (b,) + p.shape[1:])
    x = _even_tokens(x, lat(a1), lat(sh1), lat(g1), lat(a2), lat(sh2), lat(g2), w_in, e_hy_short[0],
                     _hyena_filter(l, *filt_args, tl=1024), e_hy_skip[0], w_out, wg, wu, wd, tm, tp, True)
    ctx = _even_tokens(ctx, cx(a1), cx(sh1), cx(g1), cx(a2), cx(sh2), cx(g2), w_in, e_hy_short[0],
                       _hyena_filter(n_ctx, *filt_args, tl=n_ctx), e_hy_skip[0], w_out, wg, wu, wd, n_ctx, n_ctx, False)

    (sh1, sc1, g1, sh2, sc2, g2), nb = _mod_params(c, c_ctx, o_ada_w[0], o_ada_b[0])
    a1 = o_norm1[0] * (1.0 + sc1)
    a2 = o_norm2[0] * (1.0 + sc2)
    w_in = o_w_in[0].astype(bf16)
    ident = (jnp.ones((n_ctx, LANES), f32), jnp.zeros((n_ctx, LANES), f32), jnp.zeros((n_ctx, LANES), f32))
    _, kc, vc, _ = _inproj_odd(ctx, cx(a1), cx(sh1), w_in, ident, o_q_norm[0], o_k_norm[0], n_ctx)
    q, k, v, s = _inproj_odd(x, lat(a1), lat(sh1), w_in, _rope_tables(l), o_q_norm[0], o_k_norm[0], tp)
    y_att = _attention(q, k, v, kc, vc, o_sink[0])
    sw = s.shape[2] // 3
    (y_sc,) = _conv_call(functools.partial(_sgconv_kernel, width=sw), s, o_sc_conv[0], [sw], 512, "sgconv")
    x = _outproj(x, y_att, y_sc, o_w_out[0].astype(bf16), lat(g1), tm)
    h, cw = _router(x, lat(a2), lat(sh2), o_router_w[0], o_router_b[0], tm)
    return _moe(h, cw, x, lat(g2), o_moe_wg[0], o_moe_wu[0], o_moe_wd[0], tm, 512)
```

```python
import functools
import math

import jax
import jax.numpy as jnp
from jax import lax
from jax.experimental import pallas as pl
from jax.experimental.pallas import tpu as pltpu

f32 = jnp.float32
bf16 = jnp.bfloat16
SDS = jax.ShapeDtypeStruct

EPS = 1e-6
N_MOD = 6
GROUP_DIM = 64
HEAD_DIM = 64
N_Q_HEADS = 8
N_KV_HEADS = 2
Q_PER_KV = N_Q_HEADS // N_KV_HEADS
WINDOW = 128
GRID_W = 64
ROPE_BASE = 10000.0
N_EXPERTS = 8
NEG_INF = -1e30
HY_EMB_DIM = 33
HY_DECAY_TARGET = 1e-2
HY_FAST_DECAY_PCT = 0.3
HY_SLOW_DECAY_PCT = 1.5

LANES = 128
DFT_N2 = 128
K1_PER_STEP = 8
VMEM_LIMIT = 56 * 1024 * 1024


def _cparams(*sem):
    return pltpu.CompilerParams(dimension_semantics=sem, vmem_limit_bytes=VMEM_LIMIT)


def _dot(a, b, **kw):
    return jnp.dot(a, b, preferred_element_type=f32, **kw)


def _full_spec(shape):
    nd = len(shape)
    return pl.BlockSpec(shape, lambda *_: (0,) * nd)


def _ada_kernel(c_ref, w_ref, b_ref, o_ref):
    c = c_ref[...]
    s = (c * jax.nn.sigmoid(c)).astype(bf16)
    o_ref[...] = _dot(s, w_ref[...].astype(bf16)) + b_ref[...]


def _ada(cond, w, b):
    d, n = w.shape
    tn = n // 4
    return pl.pallas_call(
        _ada_kernel,
        out_shape=SDS((cond.shape[0], n), f32),
        grid=(n // tn,),
        in_specs=[
            pl.BlockSpec(cond.shape, lambda j: (0, 0)),
            pl.BlockSpec((d, tn), lambda j: (0, j)),
            pl.BlockSpec((1, tn), lambda j: (0, j)),
        ],
        out_specs=pl.BlockSpec((cond.shape[0], tn), lambda j: (0, j)),
        compiler_params=_cparams("arbitrary"),
        name="ada",
    )(cond, w, b.reshape(1, n))


def _modulated(x, a, sh):
    ms = jnp.mean(x * x, axis=-1, keepdims=True)
    return (x * lax.rsqrt(ms + EPS)) * a + sh


def _inproj_kernel(x_ref, a_ref, sh_ref, w_ref, *o_refs, splits):
    h = _modulated(x_ref[0], a_ref[0], sh_ref[0]).astype(bf16)
    for o_ref, (s, e) in zip(o_refs, splits):
        o_ref[0] = _dot(h, w_ref[:, s:e]).astype(o_ref.dtype)


def _inproj(x, a, sh, w, splits, tm):
    b, l, d = x.shape
    n = w.shape[1]
    return pl.pallas_call(
        functools.partial(_inproj_kernel, splits=splits),
        out_shape=[SDS((b, l, e - s), bf16) for s, e in splits],
        grid=(b, l // tm),
        in_specs=[
            pl.BlockSpec((1, tm, d), lambda i, j: (i, j, 0)),
            pl.BlockSpec((1, 1, d), lambda i, j: (i, 0, 0)),
            pl.BlockSpec((1, 1, d), lambda i, j: (i, 0, 0)),
            pl.BlockSpec((d, n), lambda i, j: (0, 0)),
        ],
        out_specs=[pl.BlockSpec((1, tm, e - s), lambda i, j: (i, j, 0)) for s, e in splits],
        compiler_params=_cparams("parallel", "parallel"),
        name="inproj",
    )(x, a, sh, w)


def _norm_rope(t, seg, gain, cos, sin_a, sin_b):
    ms = _dot((t * t).astype(bf16), seg)
    tn = t * lax.rsqrt(ms + EPS) * gain
    w = t.shape[1]
    reps = w // cos.shape[1]
    cos = jnp.concatenate([cos] * reps, axis=1)
    sin_a = jnp.concatenate([sin_a] * reps, axis=1)
    sin_b = jnp.concatenate([sin_b] * reps, axis=1)
    quarter = HEAD_DIM // 4
    return tn * cos + pltpu.roll(tn, w - quarter, 1) * sin_a + pltpu.roll(tn, quarter, 1) * sin_b


def _inproj_odd_kernel(x_ref, a_ref, sh_ref, w_ref, cos_ref, sa_ref, sb_ref, gq_ref, gk_ref,
                       segq_ref, segk_ref, q_ref, k_ref, v_ref, s_ref, *, wq, wk):
    h = _modulated(x_ref[0], a_ref[0], sh_ref[0]).astype(bf16)
    cos, sa, sb = cos_ref[...], sa_ref[...], sb_ref[...]
    q = _norm_rope(_dot(h, w_ref[:, :wq]), segq_ref[...], gq_ref[...], cos, sa, sb)
    q_ref[0] = (q * HEAD_DIM ** -0.5).astype(bf16)
    k = _norm_rope(_dot(h, w_ref[:, wq:wq + wk]), segk_ref[...], gk_ref[...], cos, sa, sb)
    k_ref[0] = k.astype(bf16)
    v_ref[0] = _dot(h, w_ref[:, wq + wk:wq + 2 * wk]).astype(bf16)
    s_ref[0] = _dot(h, w_ref[:, wq + 2 * wk:]).astype(bf16)


def _inproj_odd(x, a, sh, w, tables, gq, gk, tm):
    b, l, d = x.shape
    n = w.shape[1]
    wq, wk = N_Q_HEADS * HEAD_DIM, N_KV_HEADS * HEAD_DIM
    ws = n - wq - 2 * wk
    cos, sa, sb = tables
    seg = lambda width: jnp.kron(jnp.eye(width // HEAD_DIM, dtype=f32),
                                 jnp.full((HEAD_DIM, HEAD_DIM), 1.0 / HEAD_DIM, f32)).astype(bf16)
    row = lambda i, j: (i, j, 0)
    tab = pl.BlockSpec((tm, LANES), lambda i, j: (j, 0))
    return pl.pallas_call(
        functools.partial(_inproj_odd_kernel, wq=wq, wk=wk),
        out_shape=[SDS((b, l, wq), bf16), SDS((b, l, wk), bf16), SDS((b, l, wk), bf16), SDS((b, l, ws), bf16)],
        grid=(b, l // tm),
        in_specs=[
            pl.BlockSpec((1, tm, d), row),
            pl.BlockSpec((1, 1, d), lambda i, j: (i, 0, 0)),
            pl.BlockSpec((1, 1, d), lambda i, j: (i, 0, 0)),
            pl.BlockSpec((d, n), lambda i, j: (0, 0)),
            tab, tab, tab,
            _full_spec((1, wq)), _full_spec((1, wk)), _full_spec((wq, wq)), _full_spec((wk, wk)),
        ],
        out_specs=[pl.BlockSpec((1, tm, wq), row), pl.BlockSpec((1, tm, wk), row),
                   pl.BlockSpec((1, tm, wk), row), pl.BlockSpec((1, tm, ws), row)],
        compiler_params=_cparams("parallel", "parallel"),
        name="inproj_odd",
    )(x, a, sh, w, cos, sa, sb, jnp.tile(gq, wq // HEAD_DIM)[None], jnp.tile(gk, wk // HEAD_DIM)[None],
      seg(wq), seg(wk))


HALO = 16


def _conv3(p, prev_row, next_row, w):
    tl = p.shape[0]
    r = lax.broadcasted_iota(jnp.int32, p.shape, 0)
    pm1 = jnp.where(r == 0, prev_row, pltpu.roll(p, 1, 0))
    pp1 = jnp.where(r == tl - 1, next_row, pltpu.roll(p, tl - 1, 0))
    return pm1 * w[0:1] + p * w[1:2] + pp1 * w[2:3]


def _halo_rows(pp_ref, pn_ref):
    i, n = pl.program_id(1), pl.num_programs(1)
    prev = pp_ref[0].astype(f32)[HALO - 1:HALO]
    nxt = pn_ref[0].astype(f32)[0:1]
    return jnp.where(i > 0, prev, 0.0), jnp.where(i < n - 1, nxt, 0.0)


def _hyprep_kernel(p_ref, pp_ref, pn_ref, w_ref, u_ref, x0_ref, *, width):
    prev, nxt = _halo_rows(pp_ref, pn_ref)
    c = _conv3(p_ref[0].astype(f32), prev, nxt, w_ref[...])
    x0_ref[0] = c[:, :width].astype(bf16)
    u_ref[0] = (c[:, 2 * width:] * c[:, width:2 * width]).astype(bf16)


def _sgconv_kernel(s_ref, sp_ref, sn_ref, w_ref, o_ref, *, width):
    def inner(t):
        return t[:, width:2 * width] * t[:, 2 * width:]
    i, n = pl.program_id(1), pl.num_programs(1)
    s = s_ref[0].astype(f32)
    prev = jnp.where(i > 0, inner(sp_ref[0].astype(f32))[HALO - 1:HALO], 0.0)
    nxt = jnp.where(i < n - 1, inner(sn_ref[0].astype(f32))[0:1], 0.0)
    o_ref[0] = (s[:, :width] * _conv3(inner(s), prev, nxt, w_ref[...])).astype(bf16)


def _conv_call(kernel, p, w, out_widths, tl, name):
    b, l, c = p.shape
    r = tl // HALO
    nh = l // HALO
    row = lambda i, j: (i, j, 0)
    return pl.pallas_call(
        kernel,
        out_shape=[SDS((b, l, ow), bf16) for ow in out_widths],
        grid=(b, l // tl),
        in_specs=[
            pl.BlockSpec((1, tl, c), row),
            pl.BlockSpec((1, HALO, c), lambda i, j: (i, jnp.maximum(j * r - 1, 0), 0)),
            pl.BlockSpec((1, HALO, c), lambda i, j: (i, jnp.minimum((j + 1) * r, nh - 1), 0)),
            _full_spec(w.shape),
        ],
        out_specs=[pl.BlockSpec((1, tl, ow), row) for ow in out_widths],
        compiler_params=_cparams("parallel", "parallel"),
        name=name,
    )(p, p, p, w)


def _lmm_kernel(w_ref, x_ref, o_ref):
    o_ref[0] = _dot(w_ref[...], x_ref[0]).astype(o_ref.dtype)


def _lmm(w, x, tn, out_dtype=bf16):
    m, k = w.shape
    b, _, n = x.shape
    return pl.pallas_call(
        _lmm_kernel,
        out_shape=SDS((b, m, n), out_dtype),
        grid=(b, n // tn),
        in_specs=[_full_spec((m, k)), pl.BlockSpec((1, k, tn), lambda i, j: (i, 0, j))],
        out_specs=pl.BlockSpec((1, m, tn), lambda i, j: (i, 0, j)),
        compiler_params=_cparams("parallel", "parallel"),
        name="lmm",
    )(w, x)


def _filt_kernel(z_ref, t_ref, w1_ref, b1_ref, fr_ref, w2_ref, b2_ref, w3_ref, dl_ref, hfb_ref, l1_ref, *, width):
    i = pl.program_id(0)
    hp = lax.Precision.HIGHEST
    fr = fr_ref[...]
    h = jnp.sin(fr * (_dot(z_ref[...], w1_ref[...], precision=hp) + b1_ref[...]))
    h = jnp.sin(fr * (_dot(h, w2_ref[...], precision=hp) + b2_ref[...]))
    h = _dot(h.astype(bf16), w3_ref[...])
    decay = jnp.exp(-t_ref[...] * dl_ref[...])
    hf = h[:, :width] * decay
    hb = h[:, width:] * decay
    tl = hf.shape[0]
    row = lax.broadcasted_iota(jnp.int32, hb.shape, 0) + i * tl
    hb = jnp.where(row == 0, 0.0, hb)
    hfb_ref[:, :width] = hf.astype(bf16)
    hfb_ref[:, width:] = hb.astype(bf16)
    part = jnp.sum(jnp.abs(hf), axis=0, keepdims=True) + jnp.sum(jnp.abs(hb), axis=0, keepdims=True)

    @pl.when(i == 0)
    def _():
        l1_ref[...] = jnp.zeros_like(l1_ref)

    l1_ref[...] += part


def _hyena_filter(l, w1, b1, freq, w2, b2, w3, tl):
    hid = w1.shape[1]
    width = w3.shape[1] // 2
    t = jnp.linspace(0.0, 1.0, l, dtype=f32)[:, None]
    bands = (HY_EMB_DIM - 1) // 2
    ang = (2.0 * math.pi / l) * jnp.arange(l, dtype=f32)[:, None] * jnp.linspace(1e-4, bands - 1, bands, dtype=f32)[None, :]
    z = jnp.concatenate([t, jnp.cos(ang), -jnp.sin(ang)], axis=-1)
    z = jnp.pad(z, ((0, 0), (0, LANES - HY_EMB_DIM)))
    pad_h = LANES - hid
    w1p = jnp.pad(w1.astype(f32), ((0, LANES - HY_EMB_DIM), (0, pad_h)))
    b1p = jnp.pad(b1.astype(f32), (0, pad_h))[None]
    frp = jnp.pad(freq.astype(f32), (0, pad_h))[None]
    w2p = jnp.pad(w2.astype(f32), ((0, pad_h), (0, pad_h)))
    b2p = jnp.pad(b2.astype(f32), (0, pad_h))[None]
    w3p = jnp.pad(w3, ((0, pad_h), (0, 0))).astype(bf16)
    max_decay = math.log(HY_DECAY_TARGET) / HY_FAST_DECAY_PCT
    min_decay = math.log(HY_DECAY_TARGET) / HY_SLOW_DECAY_PCT
    deltas = jnp.abs(jnp.linspace(min_decay, max_decay, width, dtype=f32))[None]
    return pl.pallas_call(
        functools.partial(_filt_kernel, width=width),
        out_shape=[SDS((l, 2 * width), bf16), SDS((1, width), f32)],
        grid=(l // tl,),
        in_specs=[
            pl.BlockSpec((tl, LANES), lambda i: (i, 0)),
            pl.BlockSpec((tl, 1), lambda i: (i, 0)),
            _full_spec(w1p.shape), _full_spec(b1p.shape), _full_spec(frp.shape),
            _full_spec(w2p.shape), _full_spec(b2p.shape), _full_spec(w3p.shape), _full_spec(deltas.shape),
        ],
        out_specs=[pl.BlockSpec((tl, 2 * width), lambda i: (i, 0)), pl.BlockSpec((1, width), lambda i: (0, 0))],
        compiler_params=_cparams("arbitrary"),
        name="hyena_filter",
    )(z, t, w1p, b1p, frp, w2p, b2p, w3p, deltas)


def _cs(num, den):
    ang = (2.0 * math.pi / den) * (num % den).astype(f32)
    return jnp.cos(ang), jnp.sin(ang)


def _outer(n_rows, n_cols):
    return jnp.arange(n_rows, dtype=jnp.int32)[:, None] * jnp.arange(n_cols, dtype=jnp.int32)[None, :]


def _interleave_rows(a, b):
    return jnp.stack([a, b], axis=1).reshape(2 * a.shape[0], a.shape[1])


def _interleave_cols(a, b):
    return jnp.stack([a, b], axis=2).reshape(a.shape[0], 2 * a.shape[1])


def _inner_tables(n1, scale):
    c2, s2 = _cs(_outer(DFT_N2, DFT_N2), DFT_N2)
    ct, st = _cs(_outer(n1, DFT_N2), n1 * DFT_N2)
    return c2 * scale, -s2 * scale, ct[:, None, :], -st[:, None, :]


def _gmat(f2r, f2i, tr, ti):
    return f2r * tr - f2i * ti, f2r * ti + f2i * tr


def _gblock(gr, gi):
    return jnp.concatenate([jnp.concatenate([gr, -gi], axis=1), jnp.concatenate([gi, gr], axis=1)], axis=0)


def _s2f_kernel(a_ref, f2r_ref, f2i_ref, tr_ref, ti_ref, k_ref, *, width):
    f2r, f2i = f2r_ref[...], f2i_ref[...]
    h2 = DFT_N2
    for j in range(K1_PER_STEP):
        gr, gi = _gmat(f2r, f2i, tr_ref[j], ti_ref[j])
        hh = _dot(_gblock(gr, gi).astype(bf16), a_ref[0, j])
        k_ref[j, :h2, :] = hh[:h2, :width] + hh[:h2, width:]
        k_ref[j, h2:, :] = hh[h2:, :width] - hh[h2:, width:]


def _s23_kernel(a_ref, kh_ref, f2r_ref, f2i_ref, tr_ref, ti_ref, o_ref):
    f2r, f2i = f2r_ref[...], f2i_ref[...]
    h2 = DFT_N2
    for j in range(K1_PER_STEP):
        gr, gi = _gmat(f2r, f2i, tr_ref[j], ti_ref[j])
        uh = _dot(_gblock(gr, gi).astype(bf16), a_ref[0, j])
        ur, ui = uh[:h2], uh[h2:]
        kr, ki = kh_ref[j, :h2, :], kh_ref[j, h2:, :]
        yh = jnp.concatenate([ur * kr - ui * ki, ur * ki + ui * kr], axis=0).astype(bf16)
        grt, git = gr.T, gi.T
        o_ref[0, j] = _dot(_gblock(grt, -git).astype(bf16), yh).astype(o_ref.dtype)


def _s4_kernel(f4_ref, bq_ref, u_ref, x0_ref, sc_ref, sk_ref, o_ref):
    y = _dot(f4_ref[...], bq_ref[0])
    u = u_ref[0].astype(f32)
    o_ref[0] = (x0_ref[0].astype(f32) * (y * sc_ref[...] + u * sk_ref[...])).astype(o_ref.dtype)


def _s4(f4, bq, u, x0, scale, skip, tn):
    m, k = f4.shape
    b, _, n = bq.shape
    reps = tn // scale.shape[1]
    col = lambda i, j: (i, 0, j)
    return pl.pallas_call(
        _s4_kernel,
        out_shape=SDS((b, m, n), bf16),
        grid=(b, n // tn),
        in_specs=[_full_spec((m, k)), pl.BlockSpec((1, k, tn), col), pl.BlockSpec((1, m, tn), col),
                  pl.BlockSpec((1, m, tn), col), _full_spec((1, tn)), _full_spec((1, tn))],
        out_specs=pl.BlockSpec((1, m, tn), col),
        compiler_params=_cparams("parallel", "parallel"),
        name="hyena_s4",
    )(f4, bq, u, x0, jnp.tile(scale, (1, reps)), jnp.tile(skip, (1, reps)))


def _hyena_long(u, x0, hfb, l1, skip):
    b, l, w = u.shape
    n2 = DFT_N2
    half = l // n2
    n1 = 2 * half
    kb = K1_PER_STEP
    c1, s1 = _cs(_outer(n1, half), n1)
    f1 = _interleave_rows(c1, -s1).astype(bf16)
    f2r, f2i, tr, ti = _inner_tables(n1, 1.0)
    tab_specs = [_full_spec((n2, n2)), _full_spec((n2, n2)),
                 pl.BlockSpec((kb, 1, n2), lambda i, *_: (i, 0, 0)), pl.BlockSpec((kb, 1, n2), lambda i, *_: (i, 0, 0))]

    a_f = _lmm(f1, hfb.reshape(1, half, n2 * 2 * w), tn=4 * 2 * w).reshape(1, n1, 2 * n2, 2 * w)
    kh = pl.pallas_call(
        functools.partial(_s2f_kernel, width=w),
        out_shape=SDS((n1, 2 * n2, w), f32),
        grid=(n1 // kb,),
        in_specs=[pl.BlockSpec((1, kb, 2 * n2, 2 * w), lambda i: (0, i, 0, 0))] + tab_specs,
        out_specs=pl.BlockSpec((kb, 2 * n2, w), lambda i: (i, 0, 0)),
        compiler_params=_cparams("parallel"),
        name="hyena_s2f",
    )(a_f, f2r, f2i, tr, ti)

    a_u = _lmm(f1, u.reshape(b, half, n2 * w), tn=8 * w).reshape(b, n1, 2 * n2, w)
    bq = pl.pallas_call(
        _s23_kernel,
        out_shape=SDS((b, n1, 2 * n2, w), bf16),
        grid=(n1 // kb, b),
        in_specs=[pl.BlockSpec((1, kb, 2 * n2, w), lambda i, j: (j, i, 0, 0)),
                  pl.BlockSpec((kb, 2 * n2, w), lambda i, j: (i, 0, 0))] + tab_specs,
        out_specs=pl.BlockSpec((1, kb, 2 * n2, w), lambda i, j: (j, i, 0, 0)),
        compiler_params=_cparams("parallel", "parallel"),
        name="hyena_s23",
    )(a_u, kh, f2r, f2i, tr, ti)

    c4, s4 = _cs(_outer(half, n1), n1)
    f4 = (_interleave_cols(c4, -s4) * (1.0 / (n1 * n2))).astype(bf16)
    y = _s4(f4, bq.reshape(b, 2 * n1, n2 * w), u.reshape(b, half, n2 * w), x0.reshape(b, half, n2 * w),
            1.0 / l1, skip[None], tn=8 * w)
    return y.reshape(b, l, w)


def _cmul_kernel(uh_ref, hh_ref, o_ref, *, width, nf):
    ur, ui = uh_ref[0, :nf, :].astype(f32), uh_ref[0, nf:, :].astype(f32)
    hh = hh_ref[0].astype(f32)
    kr = hh[:nf, :width] + hh[:nf, width:]
    ki = hh[nf:, :width] - hh[nf:, width:]
    o_ref[0, :nf, :] = (ur * kr - ui * ki).astype(o_ref.dtype)
    o_ref[0, nf:, :] = (ur * ki + ui * kr).astype(o_ref.dtype)


def _hyena_short(u, x0, hfb, l1, skip):
    b, l, w = u.shape
    nf = 2 * l
    c, s = _cs(_outer(nf, l), nf)
    fwd = jnp.concatenate([c, -s], axis=0).astype(bf16)
    uh = _lmm(fwd, u, tn=w, out_dtype=f32)
    hh = _lmm(fwd, hfb[None], tn=2 * w, out_dtype=f32)
    yh = pl.pallas_call(
        functools.partial(_cmul_kernel, width=w, nf=nf),
        out_shape=SDS((b, 2 * nf, w), bf16),
        grid=(b,),
        in_specs=[pl.BlockSpec((1, 2 * nf, w), lambda i: (i, 0, 0)), _full_spec((1, 2 * nf, 2 * w))],
        out_specs=pl.BlockSpec((1, 2 * nf, w), lambda i: (i, 0, 0)),
        compiler_params=_cparams("parallel"),
        name="hyena_cmul",
    )(uh, hh)
    ci, si = _cs(_outer(l, nf), nf)
    inv = (jnp.concatenate([ci, -si], axis=1) * (1.0 / nf)).astype(bf16)
    return _s4(inv, yh, u, x0, 1.0 / l1, skip[None], tn=w)


def _f2_kernel(a_ref, f2r_ref, f2i_ref, tr_ref, ti_ref, o_ref, *, width):
    f2r, f2i = f2r_ref[...], f2i_ref[...]
    for j in range(K1_PER_STEP):
        gr, gi = _gmat(f2r, f2i, tr_ref[j], ti_ref[j])
        gcat = jnp.concatenate([gr, gi], axis=1).astype(bf16)
        s0 = a_ref[0, j, 0].astype(f32)
        s1 = a_ref[0, j, 1].astype(f32)
        x2 = jnp.concatenate([s0[:, :width] - s1[:, width:], s0[:, width:] + s1[:, :width]], axis=0).astype(bf16)
        o_ref[0, :, j, :] = _dot(gcat, x2).astype(o_ref.dtype)


def _fnet_long(pq):
    b, l, w2 = pq.shape
    w = w2 // 2
    n2 = DFT_N2
    n1 = l // n2
    kb = K1_PER_STEP
    c1, s1 = _cs(_outer(n1, n1), n1)
    f1 = _interleave_rows(c1, s1).astype(bf16)
    a = _lmm(f1, pq.reshape(b, n1, n2 * w2), tn=16 * w2).reshape(b, n1, 2, n2, w2)
    f2r, f2i, tr, ti = _inner_tables(n1, 1.0 / math.sqrt(l))
    out = pl.pallas_call(
        functools.partial(_f2_kernel, width=w),
        out_shape=SDS((b, n2, n1, w), bf16),
        grid=(b, n1 // kb),
        in_specs=[pl.BlockSpec((1, kb, 2, n2, w2), lambda i, j: (i, j, 0, 0, 0)),
                  _full_spec((n2, n2)), _full_spec((n2, n2)),
                  pl.BlockSpec((kb, 1, n2), lambda i, j: (j, 0, 0)), pl.BlockSpec((kb, 1, n2), lambda i, j: (j, 0, 0))],
        out_specs=pl.BlockSpec((1, n2, kb, w), lambda i, j: (i, 0, j, 0)),
        compiler_params=_cparams("parallel", "parallel"),
        name="fnet_f2",
    )(a, f2r, f2i, tr, ti)
    return out.reshape(b, l, w)


def _fnet_short(pq):
    b, l, w2 = pq.shape
    w = w2 // 2
    c, s = _cs(_outer(l, l), l)
    m = (jnp.concatenate([c, -s], axis=1) * (1.0 / math.sqrt(l))).astype(bf16)
    x = jnp.concatenate([pq[..., :w], pq[..., w:]], axis=1)
    return _lmm(m, x, tn=w)


def _mm32_kernel(a_ref, b_ref, o_ref):
    o_ref[...] = _dot(a_ref[...], b_ref[...], precision=lax.Precision.HIGHEST)


def _mm32(a, b):
    return pl.pallas_call(
        _mm32_kernel,
        out_shape=SDS((a.shape[0], b.shape[1]), f32),
        in_specs=[_full_spec(a.shape), _full_spec(b.shape)],
        out_specs=_full_spec((a.shape[0], b.shape[1])),
        grid=(1,),
        compiler_params=_cparams("arbitrary"),
        name="mm32",
    )(a, b)


def _outproj_kernel(x_ref, ya_ref, yb_ref, wa_ref, wb_ref, g_ref, o_ref):
    acc = _dot(ya_ref[0], wa_ref[...]) + _dot(yb_ref[0], wb_ref[...])
    o_ref[0] = x_ref[0] + g_ref[0] * acc


def _outproj(x, ya, yb, w, gate, tm):
    b, l, d = x.shape
    ka, kb_ = ya.shape[2], yb.shape[2]
    row = lambda i, j: (i, j, 0)
    return pl.pallas_call(
        _outproj_kernel,
        out_shape=SDS((b, l, d), f32),
        grid=(b, l // tm),
        in_specs=[pl.BlockSpec((1, tm, d), row), pl.BlockSpec((1, tm, ka), row), pl.BlockSpec((1, tm, kb_), row),
                  _full_spec((ka, d)), _full_spec((kb_, d)), pl.BlockSpec((1, 1, d), lambda i, j: (i, 0, 0))],
        out_specs=pl.BlockSpec((1, tm, d), row),
        compiler_params=_cparams("parallel", "parallel"),
        name="outproj",
    )(x, ya, yb, w[:ka], w[ka:], gate)


def _ffn_kernel(x_ref, a_ref, sh_ref, g_ref, wg_ref, wu_ref, wd_ref, o_ref, h_ref, acc_ref):
    f = pl.program_id(2)

    @pl.when(f == 0)
    def _():
        h_ref[...] = _modulated(x_ref[0], a_ref[0], sh_ref[0]).astype(bf16)
        acc_ref[...] = jnp.zeros_like(acc_ref)

    h = h_ref[...]
    g = _dot(h, wg_ref[...])
    u = _dot(h, wu_ref[...])
    acc_ref[...] += _dot((g * jax.nn.sigmoid(g) * u).astype(bf16), wd_ref[...])

    @pl.when(f == pl.num_programs(2) - 1)
    def _():
        o_ref[0] = x_ref[0] + g_ref[0] * acc_ref[...]


def _ffn(x, a, sh, gate, wg, wu, wd, tm, tf):
    b, l, d = x.shape
    ff = wg.shape[1]
    row = lambda i, j, f: (i, j, 0)
    vec = lambda i, j, f: (i, 0, 0)
    return pl.pallas_call(
        _ffn_kernel,
        out_shape=SDS((b, l, d), f32),
        grid=(b, l // tm, ff // tf),
        in_specs=[pl.BlockSpec((1, tm, d), row), pl.BlockSpec((1, 1, d), vec), pl.BlockSpec((1, 1, d), vec),
                  pl.BlockSpec((1, 1, d), vec),
                  pl.BlockSpec((d, tf), lambda i, j, f: (0, f)), pl.BlockSpec((d, tf), lambda i, j, f: (0, f)),
                  pl.BlockSpec((tf, d), lambda i, j, f: (f, 0))],
        out_specs=pl.BlockSpec((1, tm, d), row),
        scratch_shapes=[pltpu.VMEM((tm, d), bf16), pltpu.VMEM((tm, d), f32)],
        compiler_params=_cparams("parallel", "parallel", "arbitrary"),
        name="ffn",
    )(x, a, sh, gate, wg, wu, wd)


def _attn_kernel(sink_ref, q_ref, kp_ref, kc_ref, kn_ref, vp_ref, vc_ref, vn_ref, ck_ref, cv_ref, o_ref):
    i, n = pl.program_id(1), pl.num_programs(1)
    tq = q_ref.shape[1]
    n_ctx = ck_ref.shape[1]
    n_loc = 3 * tq
    r = lax.broadcasted_iota(jnp.int32, (tq, n_loc + n_ctx), 0)
    c = lax.broadcasted_iota(jnp.int32, (tq, n_loc + n_ctx), 1)
    lo = jnp.where(i > 0, 0, tq)
    hi = jnp.where(i < n - 1, n_loc, 2 * tq)
    mask = ((c >= r) & (c <= r + 2 * WINDOW) & (c >= lo) & (c < hi)) | (c >= n_loc)
    keys = jnp.concatenate([kp_ref[0], kc_ref[0], kn_ref[0], ck_ref[0]], axis=0)
    vals = jnp.concatenate([vp_ref[0], vc_ref[0], vn_ref[0], cv_ref[0]], axis=0)
    q = q_ref[0]
    outs = []
    for hq in range(N_Q_HEADS):
        hk = hq // Q_PER_KV
        qh = q[:, hq * HEAD_DIM:(hq + 1) * HEAD_DIM]
        kh = keys[:, hk * HEAD_DIM:(hk + 1) * HEAD_DIM]
        vh = vals[:, hk * HEAD_DIM:(hk + 1) * HEAD_DIM]
        s = lax.dot_general(qh, kh, (((1,), (1,)), ((), ())), preferred_element_type=f32)
        s = jnp.where(mask, s, NEG_INF)
        sink = sink_ref[hq]
        m = jnp.maximum(jnp.max(s, axis=-1, keepdims=True), sink)
        p = jnp.exp(s - m)
        denom = jnp.sum(p, axis=-1, keepdims=True) + jnp.exp(sink - m)
        outs.append(_dot(p.astype(bf16), vh) / denom)
    o_ref[0] = jnp.concatenate(outs, axis=1).astype(o_ref.dtype)


def _attention(q, k, v, kc, vc, sink):
    b, l, wq = q.shape
    wk = k.shape[2]
    n_ctx = kc.shape[1]
    tq = WINDOW
    nb = l // tq
    cur = lambda i, j: (i, j, 0)
    prv = lambda i, j: (i, jnp.maximum(j - 1, 0), 0)
    nxt = lambda i, j: (i, jnp.minimum(j + 1, nb - 1), 0)
    kv = lambda f: pl.BlockSpec((1, tq, wk), f)
    ctx = pl.BlockSpec((1, n_ctx, wk), lambda i, j: (i, 0, 0))
    return pl.pallas_call(
        _attn_kernel,
        out_shape=SDS((b, l, wq), bf16),
        grid=(b, nb),
        in_specs=[pl.BlockSpec(memory_space=pltpu.SMEM), pl.BlockSpec((1, tq, wq), cur),
                  kv(prv), kv(cur), kv(nxt), kv(prv), kv(cur), kv(nxt), ctx, ctx],
        out_specs=pl.BlockSpec((1, tq, wq), cur),
        compiler_params=_cparams("parallel", "parallel"),
        name="attention",
    )(sink.astype(f32), q, k, k, k, v, v, v, kc, vc)


ROUTE_BLOCK = 256
EXPERT_ROWS = 1024
R_E1, R_E2, R_C1, R_C2, R_G1, R_G2 = range(6)


def _router_kernel(x_ref, a_ref, sh_ref, rw_ref, rb_ref, h_ref, tok_ref, rec_ref, cb_ref, tot_ref, carry_ref):
    @pl.when((pl.program_id(0) == 0) & (pl.program_id(1) == 0))
    def _():
        carry_ref[...] = jnp.zeros_like(carry_ref)

    h = _modulated(x_ref[0], a_ref[0], sh_ref[0])
    h_ref[0] = h.astype(bf16)
    logits = _dot(h, rw_ref[...], precision=lax.Precision.HIGHEST) + rb_ref[...]
    lane = lax.broadcasted_iota(jnp.int32, logits.shape, 1).astype(f32)
    m1 = jnp.max(logits, axis=-1, keepdims=True)
    i1 = jnp.min(jnp.where(logits == m1, lane, float(LANES)), axis=-1, keepdims=True)
    rest = jnp.where(lane == i1, NEG_INF, logits)
    m2 = jnp.max(rest, axis=-1, keepdims=True)
    i2 = jnp.min(jnp.where(rest == m2, lane, float(LANES)), axis=-1, keepdims=True)
    e2 = jnp.exp(m2 - m1)
    g1 = 1.0 / (1.0 + e2)
    sel = jnp.where((lane == i1) | (lane == i2), 1.0, 0.0)
    tm = sel.shape[0]
    tri = jnp.where(lax.broadcasted_iota(jnp.int32, (tm, tm), 0) > lax.broadcasted_iota(jnp.int32, (tm, tm), 1), 1.0, 0.0)
    cnt = _dot(tri.astype(bf16), sel.astype(bf16)) + carry_ref[...]
    c1 = jnp.sum(jnp.where(lane == i1, cnt, 0.0), axis=-1, keepdims=True)
    c2 = jnp.sum(jnp.where(lane == i2, cnt, 0.0), axis=-1, keepdims=True)
    rec = jnp.zeros_like(logits)
    for k, v in ((R_E1, i1), (R_E2, i2), (R_C1, c1), (R_C2, c2), (R_G1, g1), (R_G2, e2 * g1)):
        rec = rec + jnp.where(lane == float(k), v, 0.0)
    tok_ref[...] = rec
    rec_ref[...] = rec.T[:rec_ref.shape[0]]
    cb_ref[0] = jnp.zeros(cb_ref.shape[1:], f32)
    for s in range(tm // ROUTE_BLOCK):
        cb_ref[0, s:s + 1, :] = cnt[s * ROUTE_BLOCK:s * ROUTE_BLOCK + 1]
    total = carry_ref[...] + jnp.sum(sel, axis=0, keepdims=True)
    carry_ref[...] = total
    tot_ref[...] = total


def _router(x, a, sh, rw, rb, tm):
    b, l, d = x.shape
    e = rw.shape[1]
    nl = l // tm
    rwp = jnp.pad(rw.astype(f32), ((0, 0), (0, LANES - e)))
    rbp = jnp.pad(rb.astype(f32), (0, LANES - e), constant_values=NEG_INF)[None]
    row = lambda i, j: (i, j, 0)
    vec = lambda i, j: (i, 0, 0)
    return pl.pallas_call(
        _router_kernel,
        out_shape=[SDS((b, l, d), bf16), SDS((b * l, LANES), f32), SDS((8, b * l), f32),
                   SDS((b * nl, 8, LANES), f32), SDS((1, LANES), f32)],
        grid=(b, nl),
        in_specs=[pl.BlockSpec((1, tm, d), row), pl.BlockSpec((1, 1, d), vec), pl.BlockSpec((1, 1, d), vec),
                  _full_spec(rwp.shape), _full_spec(rbp.shape)],
        out_specs=[pl.BlockSpec((1, tm, d), row), pl.BlockSpec((tm, LANES), lambda i, j: (i * nl + j, 0)),
                   pl.BlockSpec((8, tm), lambda i, j: (0, i * nl + j)),
                   pl.BlockSpec((1, 8, LANES), lambda i, j: (i * nl + j, 0, 0)), _full_spec((1, LANES))],
        scratch_shapes=[pltpu.VMEM((1, LANES), f32)],
        compiler_params=_cparams("arbitrary", "arbitrary"),
        name="router",
    )(x, a, sh, rwp, rbp)


def _route_plan(cb, tot, n_tok, tm):
    ne, tg, tr = N_EXPERTS, ROUTE_BLOCK, EXPERT_ROWS
    i32 = jnp.int32
    cbl = cb[:, :tm // tg, :ne].reshape(-1, ne).astype(i32)
    cnt = tot[0, :ne].astype(i32)
    cbx = jnp.concatenate([cbl, cnt[None]], axis=0)
    padded = ((cnt + tr - 1) // tr) * tr
    ends = jnp.cumsum(padded)
    off = ends - padded
    n_tiles = 2 * n_tok // tr + ne
    n_act = (ends[-1] // tr).astype(i32)[None]
    tile_e = jnp.minimum(jnp.searchsorted(ends, jnp.arange(n_tiles, dtype=i32) * tr, side="right"), ne - 1).astype(i32)
    g0 = jnp.arange(n_tiles * (tr // tg), dtype=i32) * tg
    ge = jnp.minimum(jnp.searchsorted(ends, g0, side="right"), ne - 1)
    rho0 = g0 - off[ge]
    cbe = cbx[:, ge]
    blo = jnp.sum(cbe[1:] <= rho0[None], axis=0)
    bhi = jnp.sum(cbe[:-1] < (rho0 + tg)[None], axis=0) - 1
    valid = (g0 < ends[-1]) & (rho0 < cnt[ge])
    nb = jnp.where(valid, jnp.maximum(bhi - blo + 1, 0), 0).astype(i32)
    blo = jnp.where(valid, blo, 0).astype(i32)
    w0 = ((off[None, :] + cbl) // tg).astype(i32).reshape(-1)
    return dict(off=off.astype(i32), n_act=n_act, tile_e=tile_e, blo=blo, nb=nb, w0=w0, n_rows=n_tiles * tr)


def _offset_of(idx, off_ref):
    o = jnp.zeros_like(idx)
    for e in range(N_EXPERTS):
        o = o + jnp.where(idx == float(e), off_ref[e].astype(f32), 0.0)
    return o


def _gather_kernel(blo_ref, nb_ref, off_ref, rec_ref, h_hbm, xs_ref, gs_ref, hbuf, sem, acc_ref, gacc_ref):
    j = pl.program_id(0)
    tg, tb = xs_ref.shape[0], hbuf.shape[1]
    nb, b0 = nb_ref[j], blo_ref[j]
    rows = (lax.broadcasted_iota(jnp.int32, (tg, 1), 0) + j * tg).astype(f32)
    acc_ref[...] = jnp.zeros_like(acc_ref)
    gacc_ref[...] = jnp.zeros_like(gacc_ref)

    def copy(k, slot):
        return pltpu.make_async_copy(h_hbm.at[pl.ds((b0 + k) * tb, tb)], hbuf.at[slot], sem.at[slot])

    @pl.when(nb > 0)
    def _():
        copy(0, 0).start()

    def body(k, carry):
        slot = k % 2
        copy(k, slot).wait()

        @pl.when(k + 1 < nb)
        def _():
            copy(k + 1, 1 - slot).start()

        rec = rec_ref[:, pl.ds(pl.multiple_of((b0 + k) * tb, tb), tb)]
        field = lambda r: rec[r:r + 1]
        m1 = (field(R_C1) + _offset_of(field(R_E1), off_ref)) == rows
        m2 = (field(R_C2) + _offset_of(field(R_E2), off_ref)) == rows
        acc_ref[...] += _dot(jnp.where(m1 | m2, 1.0, 0.0).astype(bf16), hbuf[slot])
        gacc_ref[...] += jnp.sum(jnp.where(m1, field(R_G1), 0.0) + jnp.where(m2, field(R_G2), 0.0), axis=-1, keepdims=True)
        return carry

    lax.fori_loop(0, nb, body, 0)
    xs_ref[...] = acc_ref[...].astype(xs_ref.dtype)
    gs_ref[...] = gacc_ref[...]


def _gather(h, rec, plan):
    t, d = h.shape
    tg = ROUTE_BLOCK
    n_rows = plan["n_rows"]
    return pl.pallas_call(
        _gather_kernel,
        out_shape=[SDS((n_rows, d), bf16), SDS((n_rows, 1), f32)],
        grid_spec=pltpu.PrefetchScalarGridSpec(
            num_scalar_prefetch=3,
            grid=(n_rows // tg,),
            in_specs=[pl.BlockSpec(rec.shape, lambda j, *_: (0, 0)), pl.BlockSpec(memory_space=pl.ANY)],
            out_specs=[pl.BlockSpec((tg, d), lambda j, *_: (j, 0)), pl.BlockSpec((tg, 1), lambda j, *_: (j, 0))],
            scratch_shapes=[pltpu.VMEM((2, tg, d), bf16), pltpu.SemaphoreType.DMA((2,)),
                            pltpu.VMEM((tg, d), f32), pltpu.VMEM((tg, 1), f32)],
        ),
        compiler_params=_cparams("arbitrary"),
        name="moe_gather",
    )(plan["blo"], plan["nb"], plan["off"], rec, h)


def _expert_kernel(te_ref, na_ref, xs_ref, gs_ref, wg_ref, wu_ref, wd_ref, o_ref, acc_ref):
    j, f = pl.program_id(0), pl.program_id(1)
    last = f == pl.num_programs(1) - 1
    active = j < na_ref[0]

    @pl.when(active)
    def _():
        @pl.when(f == 0)
        def _():
            acc_ref[...] = jnp.zeros_like(acc_ref)

        xs = xs_ref[...]
        g = _dot(xs, wg_ref[0].astype(bf16))
        u = _dot(xs, wu_ref[0].astype(bf16))
        acc_ref[...] += _dot((g * jax.nn.sigmoid(g) * u).astype(bf16), wd_ref[0].astype(bf16))

        @pl.when(last)
        def _():
            o_ref[...] = (acc_ref[...] * gs_ref[...]).astype(o_ref.dtype)

    @pl.when(jnp.logical_not(active) & last)
    def _():
        o_ref[...] = jnp.zeros_like(o_ref)


def _experts(xs, gs, plan, wg, wu, wd, tf):
    n_rows, d = xs.shape
    ff = wg.shape[2]
    tr = EXPERT_ROWS
    nf = ff // tf

    def rows(j, f, te, na):
        return (jnp.minimum(j, na[0] - 1), 0)

    def w_in(j, f, te, na):
        return (te[jnp.minimum(j, na[0] - 1)], 0, jnp.where(j < na[0], f, nf - 1))

    def w_out(j, f, te, na):
        return (te[jnp.minimum(j, na[0] - 1)], jnp.where(j < na[0], f, nf - 1), 0)

    return pl.pallas_call(
        _expert_kernel,
        out_shape=SDS((n_rows, d), bf16),
        grid_spec=pltpu.PrefetchScalarGridSpec(
            num_scalar_prefetch=2,
            grid=(n_rows // tr, nf),
            in_specs=[pl.BlockSpec((tr, d), rows), pl.BlockSpec((tr, 1), rows),
                      pl.BlockSpec((1, d, tf), w_in), pl.BlockSpec((1, d, tf), w_in), pl.BlockSpec((1, tf, d), w_out)],
            out_specs=pl.BlockSpec((tr, d), lambda j, f, te, na: (j, 0)),
            scratch_shapes=[pltpu.VMEM((tr, d), f32)],
        ),
        compiler_params=_cparams("arbitrary", "arbitrary"),
        name="moe_experts",
    )(plan["tile_e"], plan["n_act"], xs, gs, wg, wu, wd)


def _combine_kernel(w0_ref, off_ref, x_ref, tok_ref, g_ref, *refs):
    ys_refs, o_ref = refs[:-1], refs[-1]
    b = pl.program_id(0)
    tb = x_ref.shape[0]
    tok = tok_ref[...]
    e1, e2 = tok[:, R_E1:R_E1 + 1], tok[:, R_E2:R_E2 + 1]
    p1 = tok[:, R_C1:R_C1 + 1] + _offset_of(e1, off_ref)
    p2 = tok[:, R_C2:R_C2 + 1] + _offset_of(e2, off_ref)
    lane = lax.broadcasted_iota(jnp.int32, (tb, 2 * tb), 1).astype(f32)
    acc = jnp.zeros(x_ref.shape, f32)
    for e in range(N_EXPERTS):
        base = (w0_ref[b * N_EXPERTS + e] * tb).astype(f32)
        hit1 = jnp.where(e1 == float(e), p1 - base, -1.0) == lane
        hit2 = jnp.where(e2 == float(e), p2 - base, -1.0) == lane
        ys = jnp.concatenate([ys_refs[2 * e][...], ys_refs[2 * e + 1][...]], axis=0)
        acc = acc + _dot(jnp.where(hit1 | hit2, 1.0, 0.0).astype(bf16), ys)
    o_ref[...] = x_ref[...] + g_ref[0] * acc


def _combine(x, tok, gate, ys, plan, l):
    t, d = x.shape
    tb = ROUTE_BLOCK
    nrb = ys.shape[0] // tb
    ne = N_EXPERTS

    def win(e, second):
        def index(i, w0, off):
            return (jnp.minimum(w0[i * ne + e] + second, nrb - 1), 0)
        return pl.BlockSpec((tb, d), index)

    ys_specs = [win(e, s) for e in range(ne) for s in (0, 1)]
    return pl.pallas_call(
        _combine_kernel,
        out_shape=SDS((t, d), f32),
        grid_spec=pltpu.PrefetchScalarGridSpec(
            num_scalar_prefetch=2,
            grid=(t // tb,),
            in_specs=[pl.BlockSpec((tb, d), lambda i, *_: (i, 0)), pl.BlockSpec((tb, LANES), lambda i, *_: (i, 0)),
                      pl.BlockSpec((1, 1, d), lambda i, *_: (i // (l // tb), 0, 0))] + ys_specs,
            out_specs=pl.BlockSpec((tb, d), lambda i, *_: (i, 0)),
        ),
        compiler_params=_cparams("arbitrary"),
        name="moe_combine",
    )(plan["w0"], plan["off"], x, tok, gate, *([ys] * (2 * ne)))


def _mod_params(c, c_ctx, w, b):
    nb = c.shape[0]
    cond = jnp.concatenate([c, c_ctx[None], jnp.zeros((8 - nb - 1, c.shape[1]), f32)], axis=0)
    m = _ada(cond, w, b)
    return [p[:, None, :] for p in jnp.split(m, N_MOD, axis=-1)], nb


def _even_tokens(s, a1, sh1, g1, a2, sh2, g2, w_in, hy_short, filt, hy_skip, w_out, wg, wu, wd, tm, tp, long):
    hw = hy_skip.shape[0]
    ph, pq = _inproj(s, a1, sh1, w_in, [(0, 3 * hw), (3 * hw, w_in.shape[1])], tp)
    u, x0 = _conv_call(functools.partial(_hyprep_kernel, width=hw), ph, hy_short, [hw, hw], tp, "hyena_prep")
    hfb, l1 = filt
    if long:
        y_h = _hyena_long(u, x0, hfb, l1, hy_skip)
        y_f = _fnet_long(pq)
    else:
        y_h = _hyena_short(u, x0, hfb, l1, hy_skip)
        y_f = _fnet_short(pq)
    s = _outproj(s, y_h, y_f, w_out, g1, tm)
    return _ffn(s, a2, sh2, g2, wg, wu, wd, tp, wg.shape[1] // 2)


def _rope_tables(l):
    quarter = HEAD_DIM // 4
    t = jnp.arange(l, dtype=jnp.int32)
    rows = (t // GRID_W).astype(f32)[:, None]
    cols = (t % GRID_W).astype(f32)[:, None]
    inv_freq = ROPE_BASE ** (-jnp.arange(quarter, dtype=f32) / quarter)
    ar, ac = rows * inv_freq[None, :], cols * inv_freq[None, :]
    zero = jnp.zeros_like(ar)
    head = lambda x, y: jnp.concatenate([x, y], axis=1)
    cos = head(head(jnp.cos(ar), jnp.cos(ar)), head(jnp.cos(ac), jnp.cos(ac)))
    sin_a = head(head(-jnp.sin(ar), zero), head(-jnp.sin(ac), zero))
    sin_b = head(head(zero, jnp.sin(ar)), head(zero, jnp.sin(ac)))
    two = lambda x: jnp.concatenate([x, x], axis=1)
    return two(cos), two(sin_a), two(sin_b)


def kernel(x, c, ctx, c_ctx, e_ada_w, e_ada_b, e_norm1, e_norm2, e_w_in, e_hy_short, e_hy_w1, e_hy_b1, e_hy_freq, e_hy_w2, e_hy_b2, e_hy_w3, e_hy_skip, e_w_out, e_ffn_wg, e_ffn_wu, e_ffn_wd, o_ada_w, o_ada_b, o_norm1, o_norm2, o_w_in, o_q_norm, o_k_norm, o_sink, o_sc_conv, o_w_out, o_router_w, o_router_b, o_moe_wg, o_moe_wu, o_moe_wd):
    b, l, d = x.shape
    n_ctx = ctx.shape[1]
    tm = 1024
    tp = 512

    (sh1, sc1, g1, sh2, sc2, g2), nb = _mod_params(c, c_ctx, e_ada_w[0], e_ada_b[0])
    a1 = e_norm1[0] * (1.0 + sc1)
    a2 = e_norm2[0] * (1.0 + sc2)
    hw = e_hy_skip.shape[1]
    fw = d - hw
    cg, sg = _cs(_outer(GROUP_DIM, GROUP_DIM), GROUP_DIM)
    eye = jnp.eye(fw // GROUP_DIM, dtype=f32)
    chan = jnp.concatenate([jnp.kron(eye, cg), jnp.kron(eye, sg)], axis=1) * (1.0 / math.sqrt(GROUP_DIM))
    w_in = jnp.concatenate([e_w_in[0][:, :3 * hw], _mm32(e_w_in[0][:, 3 * hw:], chan)], axis=1).astype(bf16)
    w_out = e_w_out[0].astype(bf16)
    wg, wu, wd = e_ffn_wg[0].astype(bf16), e_ffn_wu[0].astype(bf16), e_ffn_wd[0].astype(bf16)
    filt_args = (e_hy_w1[0], e_hy_b1[0], e_hy_freq[0], e_hy_w2[0], e_hy_b2[0], e_hy_w3[0])
    lat = lambda p: p[:nb]
    cx = lambda p: jnp.broadcast_to(p[nb:nb + 1], (b,) + p.shape[1:])
    x = _even_tokens(x, lat(a1), lat(sh1), lat(g1), lat(a2), lat(sh2), lat(g2), w_in, e_hy_short[0],
                     _hyena_filter(l, *filt_args, tl=1024), e_hy_skip[0], w_out, wg, wu, wd, tm, tp, True)
    ctx = _even_tokens(ctx, cx(a1), cx(sh1), cx(g1), cx(a2), cx(sh2), cx(g2), w_in, e_hy_short[0],
                       _hyena_filter(n_ctx, *filt_args, tl=n_ctx), e_hy_skip[0], w_out, wg, wu, wd, n_ctx, n_ctx, False)

    (sh1, sc1, g1, sh2, sc2, g2), nb = _mod_params(c, c_ctx, o_ada_w[0], o_ada_b[0])
    a1 = o_norm1[0] * (1.0 + sc1)
    a2 = o_norm2[0] * (1.0 + sc2)
    w_in = o_w_in[0].astype(bf16)
    ident = (jnp.ones((n_ctx, LANES), f32), jnp.zeros((n_ctx, LANES), f32), jnp.zeros((n_ctx, LANES), f32))
    _, kc, vc, _ = _inproj_odd(ctx, cx(a1), cx(sh1), w_in, ident, o_q_norm[0], o_k_norm[0], n_ctx)
    q, k, v, s = _inproj_odd(x, lat(a1), lat(sh1), w_in, _rope_tables(l), o_q_norm[0], o_k_norm[0], tp)
    y_att = _attention(q, k, v, kc, vc, o_sink[0])
    sw = s.shape[2] // 3
    (y_sc,) = _conv_call(functools.partial(_sgconv_kernel, width=sw), s, o_sc_conv[0], [sw], 512, "sgconv")
    x = _outproj(x, y_att, y_sc, o_w_out[0].astype(bf16), lat(g1), tm)
    h, tok, rec, cb, tot = _router(x, lat(a2), lat(sh2), o_router_w[0], o_router_b[0], tm)
    plan = _route_plan(cb, tot, b * l, tm)
    xs, gs = _gather(h.reshape(b * l, d), rec, plan)
    ys = _experts(xs, gs, plan, o_moe_wg[0], o_moe_wu[0], o_moe_wd[0], 512)
    return _combine(x.reshape(b * l, d), tok, lat(g2), ys, plan, l).reshape(b, l, d)
```

```python
import functools
import math

import jax
import jax.numpy as jnp
from jax import lax
from jax.experimental import pallas as pl
from jax.experimental.pallas import tpu as pltpu

f32 = jnp.float32
bf16 = jnp.bfloat16
SDS = jax.ShapeDtypeStruct

EPS = 1e-6
N_MOD = 6
GROUP_DIM = 64
HEAD_DIM = 64
N_Q_HEADS = 8
N_KV_HEADS = 2
Q_PER_KV = N_Q_HEADS // N_KV_HEADS
WINDOW = 128
GRID_W = 64
ROPE_BASE = 10000.0
N_EXPERTS = 8
NEG_INF = -1e30
HY_EMB_DIM = 33
HY_DECAY_TARGET = 1e-2
HY_FAST_DECAY_PCT = 0.3
HY_SLOW_DECAY_PCT = 1.5

LANES = 128
DFT_N2 = 128
K1_PER_STEP = 8
VMEM_LIMIT = 56 * 1024 * 1024


def _cparams(*sem):
    return pltpu.CompilerParams(dimension_semantics=sem, vmem_limit_bytes=VMEM_LIMIT)


def _dot(a, b, **kw):
    return jnp.dot(a, b, preferred_element_type=f32, **kw)


def _full_spec(shape):
    nd = len(shape)
    return pl.BlockSpec(shape, lambda *_: (0,) * nd)


def _ada_kernel(c_ref, w_ref, b_ref, o_ref):
    c = c_ref[...]
    s = (c * jax.nn.sigmoid(c)).astype(bf16)
    o_ref[...] = _dot(s, w_ref[...].astype(bf16)) + b_ref[...]


def _ada(cond, w, b):
    d, n = w.shape
    tn = n // 4
    return pl.pallas_call(
        _ada_kernel,
        out_shape=SDS((cond.shape[0], n), f32),
        grid=(n // tn,),
        in_specs=[
            pl.BlockSpec(cond.shape, lambda j: (0, 0)),
            pl.BlockSpec((d, tn), lambda j: (0, j)),
            pl.BlockSpec((1, tn), lambda j: (0, j)),
        ],
        out_specs=pl.BlockSpec((cond.shape[0], tn), lambda j: (0, j)),
        compiler_params=_cparams("arbitrary"),
        name="ada",
    )(cond, w, b.reshape(1, n))


def _modulated(x, a, sh):
    ms = jnp.mean(x * x, axis=-1, keepdims=True)
    return (x * lax.rsqrt(ms + EPS)) * a + sh


def _inproj_kernel(x_ref, a_ref, sh_ref, w_ref, *o_refs, splits):
    h = _modulated(x_ref[0], a_ref[0], sh_ref[0]).astype(bf16)
    for o_ref, (s, e) in zip(o_refs, splits):
        o_ref[0] = _dot(h, w_ref[:, s:e]).astype(o_ref.dtype)


def _inproj(x, a, sh, w, splits, tm):
    b, l, d = x.shape
    n = w.shape[1]
    return pl.pallas_call(
        functools.partial(_inproj_kernel, splits=splits),
        out_shape=[SDS((b, l, e - s), bf16) for s, e in splits],
        grid=(b, l // tm),
        in_specs=[
            pl.BlockSpec((1, tm, d), lambda i, j: (i, j, 0)),
            pl.BlockSpec((1, 1, d), lambda i, j: (i, 0, 0)),
            pl.BlockSpec((1, 1, d), lambda i, j: (i, 0, 0)),
            pl.BlockSpec((d, n), lambda i, j: (0, 0)),
        ],
        out_specs=[pl.BlockSpec((1, tm, e - s), lambda i, j: (i, j, 0)) for s, e in splits],
        compiler_params=_cparams("parallel", "parallel"),
        name="inproj",
    )(x, a, sh, w)


def _norm_rope(t, seg, gain, cos, sin_a, sin_b):
    ms = _dot((t * t).astype(bf16), seg)
    tn = t * lax.rsqrt(ms + EPS) * gain
    w = t.shape[1]
    reps = w // cos.shape[1]
    cos = jnp.concatenate([cos] * reps, axis=1)
    sin_a = jnp.concatenate([sin_a] * reps, axis=1)
    sin_b = jnp.concatenate([sin_b] * reps, axis=1)
    quarter = HEAD_DIM // 4
    return tn * cos + pltpu.roll(tn, w - quarter, 1) * sin_a + pltpu.roll(tn, quarter, 1) * sin_b


def _inproj_odd_kernel(x_ref, a_ref, sh_ref, w_ref, cos_ref, sa_ref, sb_ref, gq_ref, gk_ref,
                       segq_ref, segk_ref, q_ref, k_ref, v_ref, s_ref, *, wq, wk):
    h = _modulated(x_ref[0], a_ref[0], sh_ref[0]).astype(bf16)
    cos, sa, sb = cos_ref[...], sa_ref[...], sb_ref[...]
    q = _norm_rope(_dot(h, w_ref[:, :wq]), segq_ref[...], gq_ref[...], cos, sa, sb)
    q_ref[0] = (q * HEAD_DIM ** -0.5).astype(bf16)
    k = _norm_rope(_dot(h, w_ref[:, wq:wq + wk]), segk_ref[...], gk_ref[...], cos, sa, sb)
    k_ref[0] = k.astype(bf16)
    v_ref[0] = _dot(h, w_ref[:, wq + wk:wq + 2 * wk]).astype(bf16)
    s_ref[0] = _dot(h, w_ref[:, wq + 2 * wk:]).astype(bf16)


def _inproj_odd(x, a, sh, w, tables, gq, gk, tm):
    b, l, d = x.shape
    n = w.shape[1]
    wq, wk = N_Q_HEADS * HEAD_DIM, N_KV_HEADS * HEAD_DIM
    ws = n - wq - 2 * wk
    cos, sa, sb = tables
    seg = lambda width: jnp.kron(jnp.eye(width // HEAD_DIM, dtype=f32),
                                 jnp.full((HEAD_DIM, HEAD_DIM), 1.0 / HEAD_DIM, f32)).astype(bf16)
    row = lambda i, j: (i, j, 0)
    tab = pl.BlockSpec((tm, LANES), lambda i, j: (j, 0))
    return pl.pallas_call(
        functools.partial(_inproj_odd_kernel, wq=wq, wk=wk),
        out_shape=[SDS((b, l, wq), bf16), SDS((b, l, wk), bf16), SDS((b, l, wk), bf16), SDS((b, l, ws), bf16)],
        grid=(b, l // tm),
        in_specs=[
            pl.BlockSpec((1, tm, d), row),
            pl.BlockSpec((1, 1, d), lambda i, j: (i, 0, 0)),
            pl.BlockSpec((1, 1, d), lambda i, j: (i, 0, 0)),
            pl.BlockSpec((d, n), lambda i, j: (0, 0)),
            tab, tab, tab,
            _full_spec((1, wq)), _full_spec((1, wk)), _full_spec((wq, wq)), _full_spec((wk, wk)),
        ],
        out_specs=[pl.BlockSpec((1, tm, wq), row), pl.BlockSpec((1, tm, wk), row),
                   pl.BlockSpec((1, tm, wk), row), pl.BlockSpec((1, tm, ws), row)],
        compiler_params=_cparams("parallel", "parallel"),
        name="inproj_odd",
    )(x, a, sh, w, cos, sa, sb, jnp.tile(gq, wq // HEAD_DIM)[None], jnp.tile(gk, wk // HEAD_DIM)[None],
      seg(wq), seg(wk))


HALO = 16


def _conv3(p, prev_row, next_row, w):
    tl = p.shape[0]
    r = lax.broadcasted_iota(jnp.int32, p.shape, 0)
    pm1 = jnp.where(r == 0, prev_row, pltpu.roll(p, 1, 0))
    pp1 = jnp.where(r == tl - 1, next_row, pltpu.roll(p, tl - 1, 0))
    return pm1 * w[0:1] + p * w[1:2] + pp1 * w[2:3]


def _halo_rows(pp_ref, pn_ref):
    i, n = pl.program_id(1), pl.num_programs(1)
    prev = pp_ref[0].astype(f32)[HALO - 1:HALO]
    nxt = pn_ref[0].astype(f32)[0:1]
    return jnp.where(i > 0, prev, 0.0), jnp.where(i < n - 1, nxt, 0.0)


def _hyprep_kernel(p_ref, pp_ref, pn_ref, w_ref, u_ref, x0_ref, *, width):
    prev, nxt = _halo_rows(pp_ref, pn_ref)
    c = _conv3(p_ref[0].astype(f32), prev, nxt, w_ref[...])
    x0_ref[0] = c[:, :width].astype(bf16)
    u_ref[0] = (c[:, 2 * width:] * c[:, width:2 * width]).astype(bf16)


def _sgconv_kernel(s_ref, sp_ref, sn_ref, w_ref, o_ref, *, width):
    def inner(t):
        return t[:, width:2 * width] * t[:, 2 * width:]
    i, n = pl.program_id(1), pl.num_programs(1)
    s = s_ref[0].astype(f32)
    prev = jnp.where(i > 0, inner(sp_ref[0].astype(f32))[HALO - 1:HALO], 0.0)
    nxt = jnp.where(i < n - 1, inner(sn_ref[0].astype(f32))[0:1], 0.0)
    o_ref[0] = (s[:, :width] * _conv3(inner(s), prev, nxt, w_ref[...])).astype(bf16)


def _conv_call(kernel, p, w, out_widths, tl, name):
    b, l, c = p.shape
    r = tl // HALO
    nh = l // HALO
    row = lambda i, j: (i, j, 0)
    return pl.pallas_call(
        kernel,
        out_shape=[SDS((b, l, ow), bf16) for ow in out_widths],
        grid=(b, l // tl),
        in_specs=[
            pl.BlockSpec((1, tl, c), row),
            pl.BlockSpec((1, HALO, c), lambda i, j: (i, jnp.maximum(j * r - 1, 0), 0)),
            pl.BlockSpec((1, HALO, c), lambda i, j: (i, jnp.minimum((j + 1) * r, nh - 1), 0)),
            _full_spec(w.shape),
        ],
        out_specs=[pl.BlockSpec((1, tl, ow), row) for ow in out_widths],
        compiler_params=_cparams("parallel", "parallel"),
        name=name,
    )(p, p, p, w)


def _lmm_kernel(w_ref, x_ref, o_ref):
    o_ref[0] = _dot(w_ref[...], x_ref[0]).astype(o_ref.dtype)


def _lmm(w, x, tn, out_dtype=bf16):
    m, k = w.shape
    b, _, n = x.shape
    return pl.pallas_call(
        _lmm_kernel,
        out_shape=SDS((b, m, n), out_dtype),
        grid=(b, n // tn),
        in_specs=[_full_spec((m, k)), pl.BlockSpec((1, k, tn), lambda i, j: (i, 0, j))],
        out_specs=pl.BlockSpec((1, m, tn), lambda i, j: (i, 0, j)),
        compiler_params=_cparams("parallel", "parallel"),
        name="lmm",
    )(w, x)


def _filt_kernel(z_ref, t_ref, w1_ref, b1_ref, fr_ref, w2_ref, b2_ref, w3_ref, dl_ref, hfb_ref, l1_ref, *, width):
    i = pl.program_id(0)
    hp = lax.Precision.HIGHEST
    fr = fr_ref[...]
    h = jnp.sin(fr * (_dot(z_ref[...], w1_ref[...], precision=hp) + b1_ref[...]))
    h = jnp.sin(fr * (_dot(h, w2_ref[...], precision=hp) + b2_ref[...]))
    h = _dot(h.astype(bf16), w3_ref[...])
    decay = jnp.exp(-t_ref[...] * dl_ref[...])
    hf = h[:, :width] * decay
    hb = h[:, width:] * decay
    tl = hf.shape[0]
    row = lax.broadcasted_iota(jnp.int32, hb.shape, 0) + i * tl
    hb = jnp.where(row == 0, 0.0, hb)
    hfb_ref[0] = hf.astype(bf16)
    hfb_ref[1] = hb.astype(bf16)
    part = jnp.sum(jnp.abs(hf), axis=0, keepdims=True) + jnp.sum(jnp.abs(hb), axis=0, keepdims=True)

    @pl.when(i == 0)
    def _():
        l1_ref[...] = jnp.zeros_like(l1_ref)

    l1_ref[...] += part


def _hyena_filter(l, w1, b1, freq, w2, b2, w3, tl):
    hid = w1.shape[1]
    width = w3.shape[1] // 2
    t = jnp.linspace(0.0, 1.0, l, dtype=f32)[:, None]
    bands = (HY_EMB_DIM - 1) // 2
    ang = (2.0 * math.pi / l) * jnp.arange(l, dtype=f32)[:, None] * jnp.linspace(1e-4, bands - 1, bands, dtype=f32)[None, :]
    z = jnp.concatenate([t, jnp.cos(ang), -jnp.sin(ang)], axis=-1)
    z = jnp.pad(z, ((0, 0), (0, LANES - HY_EMB_DIM)))
    pad_h = LANES - hid
    w1p = jnp.pad(w1.astype(f32), ((0, LANES - HY_EMB_DIM), (0, pad_h)))
    b1p = jnp.pad(b1.astype(f32), (0, pad_h))[None]
    frp = jnp.pad(freq.astype(f32), (0, pad_h))[None]
    w2p = jnp.pad(w2.astype(f32), ((0, pad_h), (0, pad_h)))
    b2p = jnp.pad(b2.astype(f32), (0, pad_h))[None]
    w3p = jnp.pad(w3, ((0, pad_h), (0, 0))).astype(bf16)
    max_decay = math.log(HY_DECAY_TARGET) / HY_FAST_DECAY_PCT
    min_decay = math.log(HY_DECAY_TARGET) / HY_SLOW_DECAY_PCT
    deltas = jnp.abs(jnp.linspace(min_decay, max_decay, width, dtype=f32))[None]
    return pl.pallas_call(
        functools.partial(_filt_kernel, width=width),
        out_shape=[SDS((2, l, width), bf16), SDS((1, width), f32)],
        grid=(l // tl,),
        in_specs=[
            pl.BlockSpec((tl, LANES), lambda i: (i, 0)),
            pl.BlockSpec((tl, 1), lambda i: (i, 0)),
            _full_spec(w1p.shape), _full_spec(b1p.shape), _full_spec(frp.shape),
            _full_spec(w2p.shape), _full_spec(b2p.shape), _full_spec(w3p.shape), _full_spec(deltas.shape),
        ],
        out_specs=[pl.BlockSpec((2, tl, width), lambda i: (0, i, 0)), pl.BlockSpec((1, width), lambda i: (0, 0))],
        compiler_params=_cparams("arbitrary"),
        name="hyena_filter",
    )(z, t, w1p, b1p, frp, w2p, b2p, w3p, deltas)


def _cs(num, den):
    ang = (2.0 * math.pi / den) * (num % den).astype(f32)
    return jnp.cos(ang), jnp.sin(ang)


def _outer(n_rows, n_cols):
    return jnp.arange(n_rows, dtype=jnp.int32)[:, None] * jnp.arange(n_cols, dtype=jnp.int32)[None, :]


def _interleave_rows(a, b):
    return jnp.stack([a, b], axis=1).reshape(2 * a.shape[0], a.shape[1])


def _interleave_cols(a, b):
    return jnp.stack([a, b], axis=2).reshape(a.shape[0], 2 * a.shape[1])


def _inner_tables(n1, scale):
    c2, s2 = _cs(_outer(DFT_N2, DFT_N2), DFT_N2)
    ct, st = _cs(_outer(n1, DFT_N2), n1 * DFT_N2)
    return c2 * scale, -s2 * scale, ct[:, None, :], -st[:, None, :]


def _gmat(f2r, f2i, tr, ti):
    return f2r * tr - f2i * ti, f2r * ti + f2i * tr


def _gblock(gr, gi):
    return jnp.concatenate([jnp.concatenate([gr, -gi], axis=1), jnp.concatenate([gi, gr], axis=1)], axis=0)


N2_PER_STEP = 8


def _s1_kernel(fc_ref, fs_ref, x_ref, o_ref):
    fc, fs = fc_ref[...], fs_ref[...]
    for j in range(N2_PER_STEP):
        slab = x_ref[0, :, j, :]
        o_ref[0, :, 0, j, :] = _dot(fc, slab).astype(o_ref.dtype)
        o_ref[0, :, 1, j, :] = _dot(fs, slab).astype(o_ref.dtype)


def _s1(fc, fs, x):
    m, k = fc.shape
    b, _, n2, c = x.shape
    nb = N2_PER_STEP
    return pl.pallas_call(
        _s1_kernel,
        out_shape=SDS((b, m, 2, n2, c), bf16),
        grid=(b, n2 // nb),
        in_specs=[_full_spec((m, k)), _full_spec((m, k)), pl.BlockSpec((1, k, nb, c), lambda i, j: (i, 0, j, 0))],
        out_specs=pl.BlockSpec((1, m, 2, nb, c), lambda i, j: (i, 0, 0, j, 0)),
        compiler_params=_cparams("parallel", "parallel"),
        name="dft_outer",
    )(fc, fs, x)


def _s2f_kernel(a_ref, f2r_ref, f2i_ref, tr_ref, ti_ref, k_ref):
    f2r, f2i = f2r_ref[...], f2i_ref[...]
    h2 = DFT_N2
    for j in range(K1_PER_STEP):
        gr, gi = _gmat(f2r, f2i, tr_ref[j], ti_ref[j])
        g = _gblock(gr, gi).astype(bf16)
        hf, hb = _dot(g, a_ref[0, j]), _dot(g, a_ref[1, j])
        k_ref[j, :h2, :] = hf[:h2] + hb[:h2]
        k_ref[j, h2:, :] = hf[h2:] - hb[h2:]


def _s23_kernel(a_ref, kh_ref, f2r_ref, f2i_ref, tr_ref, ti_ref, o_ref):
    f2r, f2i = f2r_ref[...], f2i_ref[...]
    h2 = DFT_N2
    for j in range(K1_PER_STEP):
        gr, gi = _gmat(f2r, f2i, tr_ref[j], ti_ref[j])
        uh = _dot(_gblock(gr, gi).astype(bf16), a_ref[0, j])
        ur, ui = uh[:h2], uh[h2:]
        kr, ki = kh_ref[j, :h2, :], kh_ref[j, h2:, :]
        yh = jnp.concatenate([ur * kr - ui * ki, ur * ki + ui * kr], axis=0).astype(bf16)
        grt, git = gr.T, gi.T
        res = _dot(_gblock(grt, -git).astype(bf16), yh).astype(o_ref.dtype)
        o_ref[0, :, 2 * j, :] = res[:h2]
        o_ref[0, :, 2 * j + 1, :] = res[h2:]


def _s4t_kernel(f4_ref, bq_ref, u_ref, x0_ref, sc_ref, sk_ref, o_ref):
    f4, sc, sk = f4_ref[...], sc_ref[...], sk_ref[...]
    for j in range(N2_PER_STEP):
        y = _dot(f4, bq_ref[0, j])
        u = u_ref[0, :, j, :].astype(f32)
        o_ref[0, :, j, :] = (x0_ref[0, :, j, :].astype(f32) * (y * sc + u * sk)).astype(o_ref.dtype)


def _s4_kernel(f4_ref, bq_ref, u_ref, x0_ref, sc_ref, sk_ref, o_ref):
    y = _dot(f4_ref[...], bq_ref[0])
    u = u_ref[0].astype(f32)
    o_ref[0] = (x0_ref[0].astype(f32) * (y * sc_ref[...] + u * sk_ref[...])).astype(o_ref.dtype)


def _s4(f4, bq, u, x0, scale, skip, tn):
    m, k = f4.shape
    b, _, n = bq.shape
    reps = tn // scale.shape[1]
    col = lambda i, j: (i, 0, j)
    return pl.pallas_call(
        _s4_kernel,
        out_shape=SDS((b, m, n), bf16),
        grid=(b, n // tn),
        in_specs=[_full_spec((m, k)), pl.BlockSpec((1, k, tn), col), pl.BlockSpec((1, m, tn), col),
                  pl.BlockSpec((1, m, tn), col), _full_spec((1, tn)), _full_spec((1, tn))],
        out_specs=pl.BlockSpec((1, m, tn), col),
        compiler_params=_cparams("parallel", "parallel"),
        name="hyena_s4",
    )(f4, bq, u, x0, jnp.tile(scale, (1, reps)), jnp.tile(skip, (1, reps)))


def _hyena_long(u, x0, hfb, l1, skip):
    b, l, w = u.shape
    n2 = DFT_N2
    half = l // n2
    n1 = 2 * half
    kb = K1_PER_STEP
    nb = N2_PER_STEP
    c1, s1 = _cs(_outer(n1, half), n1)
    fc, fs = c1.astype(bf16), (-s1).astype(bf16)
    f2r, f2i, tr, ti = _inner_tables(n1, 1.0)
    tab_specs = [_full_spec((n2, n2)), _full_spec((n2, n2)),
                 pl.BlockSpec((kb, 1, n2), lambda i, *_: (i, 0, 0)), pl.BlockSpec((kb, 1, n2), lambda i, *_: (i, 0, 0))]

    a_f = _s1(fc, fs, hfb.reshape(2, half, n2, w)).reshape(2, n1, 2 * n2, w)
    kh = pl.pallas_call(
        _s2f_kernel,
        out_shape=SDS((n1, 2 * n2, w), f32),
        grid=(n1 // kb,),
        in_specs=[pl.BlockSpec((2, kb, 2 * n2, w), lambda i: (0, i, 0, 0))] + tab_specs,
        out_specs=pl.BlockSpec((kb, 2 * n2, w), lambda i: (i, 0, 0)),
        compiler_params=_cparams("parallel"),
        name="hyena_s2f",
    )(a_f, f2r, f2i, tr, ti)

    u4, x04 = u.reshape(b, half, n2, w), x0.reshape(b, half, n2, w)
    a_u = _s1(fc, fs, u4).reshape(b, n1, 2 * n2, w)
    bq = pl.pallas_call(
        _s23_kernel,
        out_shape=SDS((b, n2, 2 * n1, w), bf16),
        grid=(n1 // kb, b),
        in_specs=[pl.BlockSpec((1, kb, 2 * n2, w), lambda i, j: (j, i, 0, 0)),
                  pl.BlockSpec((kb, 2 * n2, w), lambda i, j: (i, 0, 0))] + tab_specs,
        out_specs=pl.BlockSpec((1, n2, 2 * kb, w), lambda i, j: (j, 0, i, 0)),
        compiler_params=_cparams("parallel", "parallel"),
        name="hyena_s23",
    )(a_u, kh, f2r, f2i, tr, ti)

    c4, s4 = _cs(_outer(half, n1), n1)
    f4 = (_interleave_cols(c4, -s4) * (1.0 / (n1 * n2))).astype(bf16)
    slab = pl.BlockSpec((1, half, nb, w), lambda i, j: (i, 0, j, 0))
    y = pl.pallas_call(
        _s4t_kernel,
        out_shape=SDS((b, half, n2, w), bf16),
        grid=(b, n2 // nb),
        in_specs=[_full_spec(f4.shape), pl.BlockSpec((1, nb, 2 * n1, w), lambda i, j: (i, j, 0, 0)), slab, slab,
                  _full_spec((1, w)), _full_spec((1, w))],
        out_specs=slab,
        compiler_params=_cparams("parallel", "parallel"),
        name="hyena_s4",
    )(f4, bq, u4, x04, 1.0 / l1, skip[None])
    return y.reshape(b, l, w)


def _cmul_kernel(uh_ref, hh_ref, o_ref, *, nf):
    ur, ui = uh_ref[0, :nf, :], uh_ref[0, nf:, :]
    kr = hh_ref[0, :nf, :] + hh_ref[1, :nf, :]
    ki = hh_ref[0, nf:, :] - hh_ref[1, nf:, :]
    o_ref[0, :nf, :] = (ur * kr - ui * ki).astype(o_ref.dtype)
    o_ref[0, nf:, :] = (ur * ki + ui * kr).astype(o_ref.dtype)


def _hyena_short(u, x0, hfb, l1, skip):
    b, l, w = u.shape
    nf = 2 * l
    c, s = _cs(_outer(nf, l), nf)
    fwd = jnp.concatenate([c, -s], axis=0).astype(bf16)
    uh = _lmm(fwd, u, tn=w, out_dtype=f32)
    hh = _lmm(fwd, hfb, tn=w, out_dtype=f32)
    yh = pl.pallas_call(
        functools.partial(_cmul_kernel, nf=nf),
        out_shape=SDS((b, 2 * nf, w), bf16),
        grid=(b,),
        in_specs=[pl.BlockSpec((1, 2 * nf, w), lambda i: (i, 0, 0)), _full_spec((2, 2 * nf, w))],
        out_specs=pl.BlockSpec((1, 2 * nf, w), lambda i: (i, 0, 0)),
        compiler_params=_cparams("parallel"),
        name="hyena_cmul",
    )(uh, hh)
    ci, si = _cs(_outer(l, nf), nf)
    inv = (jnp.concatenate([ci, -si], axis=1) * (1.0 / nf)).astype(bf16)
    return _s4(inv, yh, u, x0, 1.0 / l1, skip[None], tn=w)


def _f2_kernel(a_ref, f2r_ref, f2i_ref, tr_ref, ti_ref, o_ref, *, width):
    f2r, f2i = f2r_ref[...], f2i_ref[...]
    for j in range(K1_PER_STEP):
        gr, gi = _gmat(f2r, f2i, tr_ref[j], ti_ref[j])
        gcat = jnp.concatenate([gr, gi], axis=1).astype(bf16)
        s0 = a_ref[0, j, 0].astype(f32)
        s1 = a_ref[0, j, 1].astype(f32)
        x2 = jnp.concatenate([s0[:, :width] - s1[:, width:], s0[:, width:] + s1[:, :width]], axis=0).astype(bf16)
        o_ref[0, :, j, :] = _dot(gcat, x2).astype(o_ref.dtype)


def _fnet_long(pq):
    b, l, w2 = pq.shape
    w = w2 // 2
    n2 = DFT_N2
    n1 = l // n2
    kb = K1_PER_STEP
    c1, s1 = _cs(_outer(n1, n1), n1)
    a = _s1(c1.astype(bf16), s1.astype(bf16), pq.reshape(b, n1, n2, w2))
    f2r, f2i, tr, ti = _inner_tables(n1, 1.0 / math.sqrt(l))
    out = pl.pallas_call(
        functools.partial(_f2_kernel, width=w),
        out_shape=SDS((b, n2, n1, w), bf16),
        grid=(b, n1 // kb),
        in_specs=[pl.BlockSpec((1, kb, 2, n2, w2), lambda i, j: (i, j, 0, 0, 0)),
                  _full_spec((n2, n2)), _full_spec((n2, n2)),
                  pl.BlockSpec((kb, 1, n2), lambda i, j: (j, 0, 0)), pl.BlockSpec((kb, 1, n2), lambda i, j: (j, 0, 0))],
        out_specs=pl.BlockSpec((1, n2, kb, w), lambda i, j: (i, 0, j, 0)),
        compiler_params=_cparams("parallel", "parallel"),
        name="fnet_f2",
    )(a, f2r, f2i, tr, ti)
    return out.reshape(b, l, w)


def _fnet_short(pq):
    b, l, w2 = pq.shape
    w = w2 // 2
    c, s = _cs(_outer(l, l), l)
    m = (jnp.concatenate([c, -s], axis=1) * (1.0 / math.sqrt(l))).astype(bf16)
    x = jnp.concatenate([pq[..., :w], pq[..., w:]], axis=1)
    return _lmm(m, x, tn=w)


def _mm32_kernel(a_ref, b_ref, o_ref):
    o_ref[...] = _dot(a_ref[...], b_ref[...], precision=lax.Precision.HIGHEST)


def _mm32(a, b):
    return pl.pallas_call(
        _mm32_kernel,
        out_shape=SDS((a.shape[0], b.shape[1]), f32),
        in_specs=[_full_spec(a.shape), _full_spec(b.shape)],
        out_specs=_full_spec((a.shape[0], b.shape[1])),
        grid=(1,),
        compiler_params=_cparams("arbitrary"),
        name="mm32",
    )(a, b)


def _outproj_kernel(x_ref, ya_ref, yb_ref, wa_ref, wb_ref, g_ref, o_ref):
    acc = _dot(ya_ref[0], wa_ref[...]) + _dot(yb_ref[0], wb_ref[...])
    o_ref[0] = x_ref[0] + g_ref[0] * acc


def _outproj(x, ya, yb, w, gate, tm):
    b, l, d = x.shape
    ka, kb_ = ya.shape[2], yb.shape[2]
    row = lambda i, j: (i, j, 0)
    return pl.pallas_call(
        _outproj_kernel,
        out_shape=SDS((b, l, d), f32),
        grid=(b, l // tm),
        in_specs=[pl.BlockSpec((1, tm, d), row), pl.BlockSpec((1, tm, ka), row), pl.BlockSpec((1, tm, kb_), row),
                  _full_spec((ka, d)), _full_spec((kb_, d)), pl.BlockSpec((1, 1, d), lambda i, j: (i, 0, 0))],
        out_specs=pl.BlockSpec((1, tm, d), row),
        compiler_params=_cparams("parallel", "parallel"),
        name="outproj",
    )(x, ya, yb, w[:ka], w[ka:], gate)


def _ffn_kernel(x_ref, a_ref, sh_ref, g_ref, wg_ref, wu_ref, wd_ref, o_ref, h_ref, acc_ref):
    f = pl.program_id(2)

    @pl.when(f == 0)
    def _():
        h_ref[...] = _modulated(x_ref[0], a_ref[0], sh_ref[0]).astype(bf16)
        acc_ref[...] = jnp.zeros_like(acc_ref)

    h = h_ref[...]
    g = _dot(h, wg_ref[...])
    u = _dot(h, wu_ref[...])
    acc_ref[...] += _dot((g * jax.nn.sigmoid(g) * u).astype(bf16), wd_ref[...])

    @pl.when(f == pl.num_programs(2) - 1)
    def _():
        o_ref[0] = x_ref[0] + g_ref[0] * acc_ref[...]


def _ffn(x, a, sh, gate, wg, wu, wd, tm, tf):
    b, l, d = x.shape
    ff = wg.shape[1]
    row = lambda i, j, f: (i, j, 0)
    vec = lambda i, j, f: (i, 0, 0)
    return pl.pallas_call(
        _ffn_kernel,
        out_shape=SDS((b, l, d), f32),
        grid=(b, l // tm, ff // tf),
        in_specs=[pl.BlockSpec((1, tm, d), row), pl.BlockSpec((1, 1, d), vec), pl.BlockSpec((1, 1, d), vec),
                  pl.BlockSpec((1, 1, d), vec),
                  pl.BlockSpec((d, tf), lambda i, j, f: (0, f)), pl.BlockSpec((d, tf), lambda i, j, f: (0, f)),
                  pl.BlockSpec((tf, d), lambda i, j, f: (f, 0))],
        out_specs=pl.BlockSpec((1, tm, d), row),
        scratch_shapes=[pltpu.VMEM((tm, d), bf16), pltpu.VMEM((tm, d), f32)],
        compiler_params=_cparams("parallel", "parallel", "arbitrary"),
        name="ffn",
    )(x, a, sh, gate, wg, wu, wd)


def _attn_kernel(sink_ref, q_ref, kp_ref, kc_ref, kn_ref, vp_ref, vc_ref, vn_ref, ck_ref, cv_ref, o_ref):
    i, n = pl.program_id(1), pl.num_programs(1)
    tq = q_ref.shape[1]
    n_ctx = ck_ref.shape[1]
    n_loc = 3 * tq
    r = lax.broadcasted_iota(jnp.int32, (tq, n_loc + n_ctx), 0)
    c = lax.broadcasted_iota(jnp.int32, (tq, n_loc + n_ctx), 1)
    lo = jnp.where(i > 0, 0, tq)
    hi = jnp.where(i < n - 1, n_loc, 2 * tq)
    mask = ((c >= r) & (c <= r + 2 * WINDOW) & (c >= lo) & (c < hi)) | (c >= n_loc)
    keys = jnp.concatenate([kp_ref[0], kc_ref[0], kn_ref[0], ck_ref[0]], axis=0)
    vals = jnp.concatenate([vp_ref[0], vc_ref[0], vn_ref[0], cv_ref[0]], axis=0)
    q = q_ref[0]
    outs = []
    for hq in range(N_Q_HEADS):
        hk = hq // Q_PER_KV
        qh = q[:, hq * HEAD_DIM:(hq + 1) * HEAD_DIM]
        kh = keys[:, hk * HEAD_DIM:(hk + 1) * HEAD_DIM]
        vh = vals[:, hk * HEAD_DIM:(hk + 1) * HEAD_DIM]
        s = lax.dot_general(qh, kh, (((1,), (1,)), ((), ())), preferred_element_type=f32)
        s = jnp.where(mask, s, NEG_INF)
        sink = sink_ref[hq]
        m = jnp.maximum(jnp.max(s, axis=-1, keepdims=True), sink)
        p = jnp.exp(s - m)
        denom = jnp.sum(p, axis=-1, keepdims=True) + jnp.exp(sink - m)
        outs.append(_dot(p.astype(bf16), vh) / denom)
    o_ref[0] = jnp.concatenate(outs, axis=1).astype(o_ref.dtype)


def _attention(q, k, v, kc, vc, sink):
    b, l, wq = q.shape
    wk = k.shape[2]
    n_ctx = kc.shape[1]
    tq = WINDOW
    nb = l // tq
    cur = lambda i, j: (i, j, 0)
    prv = lambda i, j: (i, jnp.maximum(j - 1, 0), 0)
    nxt = lambda i, j: (i, jnp.minimum(j + 1, nb - 1), 0)
    kv = lambda f: pl.BlockSpec((1, tq, wk), f)
    ctx = pl.BlockSpec((1, n_ctx, wk), lambda i, j: (i, 0, 0))
    return pl.pallas_call(
        _attn_kernel,
        out_shape=SDS((b, l, wq), bf16),
        grid=(b, nb),
        in_specs=[pl.BlockSpec(memory_space=pltpu.SMEM), pl.BlockSpec((1, tq, wq), cur),
                  kv(prv), kv(cur), kv(nxt), kv(prv), kv(cur), kv(nxt), ctx, ctx],
        out_specs=pl.BlockSpec((1, tq, wq), cur),
        compiler_params=_cparams("parallel", "parallel"),
        name="attention",
    )(sink.astype(f32), q, k, k, k, v, v, v, kc, vc)


GATHER_SLOTS = 8
ROUTE_BLOCK = 256
EXPERT_ROWS = 1024
R_E1, R_E2, R_C1, R_C2, R_G1, R_G2 = range(6)


def _router_kernel(x_ref, a_ref, sh_ref, rw_ref, rb_ref, h_ref, tok_ref, rec_ref, cb_ref, tot_ref, carry_ref):
    @pl.when((pl.program_id(0) == 0) & (pl.program_id(1) == 0))
    def _():
        carry_ref[...] = jnp.zeros_like(carry_ref)

    h = _modulated(x_ref[0], a_ref[0], sh_ref[0])
    h_ref[0] = h.astype(bf16)
    logits = _dot(h, rw_ref[...], precision=lax.Precision.HIGHEST) + rb_ref[...]
    lane = lax.broadcasted_iota(jnp.int32, logits.shape, 1).astype(f32)
    m1 = jnp.max(logits, axis=-1, keepdims=True)
    i1 = jnp.min(jnp.where(logits == m1, lane, float(LANES)), axis=-1, keepdims=True)
    rest = jnp.where(lane == i1, NEG_INF, logits)
    m2 = jnp.max(rest, axis=-1, keepdims=True)
    i2 = jnp.min(jnp.where(rest == m2, lane, float(LANES)), axis=-1, keepdims=True)
    e2 = jnp.exp(m2 - m1)
    g1 = 1.0 / (1.0 + e2)
    sel = jnp.where((lane == i1) | (lane == i2), 1.0, 0.0)
    tm = sel.shape[0]
    tri = jnp.where(lax.broadcasted_iota(jnp.int32, (tm, tm), 0) > lax.broadcasted_iota(jnp.int32, (tm, tm), 1), 1.0, 0.0)
    cnt = _dot(tri.astype(bf16), sel.astype(bf16)) + carry_ref[...]
    c1 = jnp.sum(jnp.where(lane == i1, cnt, 0.0), axis=-1, keepdims=True)
    c2 = jnp.sum(jnp.where(lane == i2, cnt, 0.0), axis=-1, keepdims=True)
    rec = jnp.zeros_like(logits)
    for k, v in ((R_E1, i1), (R_E2, i2), (R_C1, c1), (R_C2, c2), (R_G1, g1), (R_G2, e2 * g1)):
        rec = rec + jnp.where(lane == float(k), v, 0.0)
    tok_ref[...] = rec
    rec_ref[...] = rec.T[:rec_ref.shape[0]]
    cb_ref[0] = jnp.zeros(cb_ref.shape[1:], f32)
    for s in range(tm // ROUTE_BLOCK):
        cb_ref[0, s:s + 1, :] = cnt[s * ROUTE_BLOCK:s * ROUTE_BLOCK + 1]
    total = carry_ref[...] + jnp.sum(sel, axis=0, keepdims=True)
    carry_ref[...] = total
    tot_ref[...] = total


def _router(x, a, sh, rw, rb, tm):
    b, l, d = x.shape
    e = rw.shape[1]
    nl = l // tm
    rwp = jnp.pad(rw.astype(f32), ((0, 0), (0, LANES - e)))
    rbp = jnp.pad(rb.astype(f32), (0, LANES - e), constant_values=NEG_INF)[None]
    row = lambda i, j: (i, j, 0)
    vec = lambda i, j: (i, 0, 0)
    return pl.pallas_call(
        _router_kernel,
        out_shape=[SDS((b, l, d), bf16), SDS((b * l, LANES), f32), SDS((8, b * l), f32),
                   SDS((b * nl, 8, LANES), f32), SDS((1, LANES), f32)],
        grid=(b, nl),
        in_specs=[pl.BlockSpec((1, tm, d), row), pl.BlockSpec((1, 1, d), vec), pl.BlockSpec((1, 1, d), vec),
                  _full_spec(rwp.shape), _full_spec(rbp.shape)],
        out_specs=[pl.BlockSpec((1, tm, d), row), pl.BlockSpec((tm, LANES), lambda i, j: (i * nl + j, 0)),
                   pl.BlockSpec((8, tm), lambda i, j: (0, i * nl + j)),
                   pl.BlockSpec((1, 8, LANES), lambda i, j: (i * nl + j, 0, 0)), _full_spec((1, LANES))],
        scratch_shapes=[pltpu.VMEM((1, LANES), f32)],
        compiler_params=_cparams("arbitrary", "arbitrary"),
        name="router",
    )(x, a, sh, rwp, rbp)


def _route_plan(cb, tot, n_tok, tm):
    ne, tg, tr = N_EXPERTS, ROUTE_BLOCK, EXPERT_ROWS
    i32 = jnp.int32
    cbl = cb[:, :tm // tg, :ne].reshape(-1, ne).astype(i32)
    cnt = tot[0, :ne].astype(i32)
    cbx = jnp.concatenate([cbl, cnt[None]], axis=0)
    padded = ((cnt + tr - 1) // tr) * tr
    ends = jnp.cumsum(padded)
    off = ends - padded
    n_tiles = 2 * n_tok // tr + ne
    n_act = (ends[-1] // tr).astype(i32)[None]
    owner = lambda row0: jnp.minimum(jnp.sum(ends[None, :] <= row0[:, None], axis=1), ne - 1).astype(i32)
    tile_e = owner(jnp.arange(n_tiles, dtype=i32) * tr)
    g0 = jnp.arange(n_tiles * (tr // tg), dtype=i32) * tg
    ge = owner(g0)
    rho0 = g0 - off[ge]
    cbe = cbx[:, ge]
    blo = jnp.sum(cbe[1:] <= rho0[None], axis=0)
    bhi = jnp.sum(cbe[:-1] < (rho0 + tg)[None], axis=0) - 1
    valid = (g0 < ends[-1]) & (rho0 < cnt[ge])
    nb = jnp.where(valid, jnp.maximum(bhi - blo + 1, 0), 0).astype(i32)
    blo = jnp.where(valid, blo, 0).astype(i32)
    w0 = ((off[None, :] + cbl) // tg).astype(i32).reshape(-1)
    return dict(off=off.astype(i32), n_act=n_act, tile_e=tile_e, blo=blo, nb=nb, w0=w0, n_rows=n_tiles * tr)


def _offset_of(idx, off_ref):
    o = jnp.zeros_like(idx)
    for e in range(N_EXPERTS):
        o = o + jnp.where(idx == float(e), off_ref[e].astype(f32), 0.0)
    return o


def _gather_kernel(blo_ref, nb_ref, off_ref, rec_ref, h_hbm, xs_ref, gs_ref, hbuf, sem, acc_ref, gacc_ref):
    j, nj = pl.program_id(0), pl.num_programs(0)
    tg, tb = xs_ref.shape[0], hbuf.shape[2]
    half = j % 2
    nb, b0 = nb_ref[j], blo_ref[j]
    rows = (lax.broadcasted_iota(jnp.int32, (tg, 1), 0) + j * tg).astype(f32)
    acc_ref[...] = jnp.zeros_like(acc_ref)
    gacc_ref[...] = jnp.zeros_like(gacc_ref)

    def copy(first, k, hf, slot):
        return pltpu.make_async_copy(h_hbm.at[pl.ds((first + k) * tb, tb)], hbuf.at[hf, slot], sem.at[hf, slot])

    def prefetch(tile, hf):
        first, count = blo_ref[tile], nb_ref[tile]
        for s in range(GATHER_SLOTS):
            @pl.when(s < count)
            def _():
                copy(first, s, hf, s).start()

    @pl.when(j == 0)
    def _():
        prefetch(0, 0)

    @pl.when(j + 1 < nj)
    def _():
        prefetch(j + 1, 1 - half)

    def body(k, carry):
        slot = k % GATHER_SLOTS

        @pl.when(k >= GATHER_SLOTS)
        def _():
            copy(b0, k, half, slot).start()

        copy(b0, k, half, slot).wait()
        rec = rec_ref[:, pl.ds(pl.multiple_of((b0 + k) * tb, tb), tb)]
        field = lambda r: rec[r:r + 1]
        m1 = (field(R_C1) + _offset_of(field(R_E1), off_ref)) == rows
        m2 = (field(R_C2) + _offset_of(field(R_E2), off_ref)) == rows
        acc_ref[...] += _dot(jnp.where(m1 | m2, 1.0, 0.0).astype(bf16), hbuf[half, slot])
        gacc_ref[...] += jnp.sum(jnp.where(m1, field(R_G1), 0.0) + jnp.where(m2, field(R_G2), 0.0), axis=-1, keepdims=True)
        return carry

    lax.fori_loop(0, nb, body, 0)
    xs_ref[...] = acc_ref[...].astype(xs_ref.dtype)
    gs_ref[...] = gacc_ref[...]


def _gather(h, rec, plan):
    t, d = h.shape
    tg = ROUTE_BLOCK
    n_rows = plan["n_rows"]
    return pl.pallas_call(
        _gather_kernel,
        out_shape=[SDS((n_rows, d), bf16), SDS((n_rows, 1), f32)],
        grid_spec=pltpu.PrefetchScalarGridSpec(
            num_scalar_prefetch=3,
            grid=(n_rows // tg,),
            in_specs=[pl.BlockSpec(rec.shape, lambda j, *_: (0, 0)), pl.BlockSpec(memory_space=pl.ANY)],
            out_specs=[pl.BlockSpec((tg, d), lambda j, *_: (j, 0)), pl.BlockSpec((tg, 1), lambda j, *_: (j, 0))],
            scratch_shapes=[pltpu.VMEM((2, GATHER_SLOTS, tg, d), bf16), pltpu.SemaphoreType.DMA((2, GATHER_SLOTS)),
                            pltpu.VMEM((tg, d), f32), pltpu.VMEM((tg, 1), f32)],
        ),
        compiler_params=_cparams("arbitrary"),
        name="moe_gather",
    )(plan["blo"], plan["nb"], plan["off"], rec, h)


def _expert_kernel(te_ref, na_ref, xs_ref, gs_ref, wg_ref, wu_ref, wd_ref, o_ref, acc_ref):
    j, f = pl.program_id(0), pl.program_id(1)
    last = f == pl.num_programs(1) - 1
    active = j < na_ref[0]

    @pl.when(active)
    def _():
        @pl.when(f == 0)
        def _():
            acc_ref[...] = jnp.zeros_like(acc_ref)

        xs = xs_ref[...]
        g = _dot(xs, wg_ref[0].astype(bf16))
        u = _dot(xs, wu_ref[0].astype(bf16))
        acc_ref[...] += _dot((g * jax.nn.sigmoid(g) * u).astype(bf16), wd_ref[0].astype(bf16))

        @pl.when(last)
        def _():
            o_ref[...] = (acc_ref[...] * gs_ref[...]).astype(o_ref.dtype)

    @pl.when(jnp.logical_not(active) & last)
    def _():
        o_ref[...] = jnp.zeros_like(o_ref)


def _experts(xs, gs, plan, wg, wu, wd, tf):
    n_rows, d = xs.shape
    ff = wg.shape[2]
    tr = EXPERT_ROWS
    nf = ff // tf

    def rows(j, f, te, na):
        return (jnp.minimum(j, na[0] - 1), 0)

    def w_in(j, f, te, na):
        return (te[jnp.minimum(j, na[0] - 1)], 0, jnp.where(j < na[0], f, nf - 1))

    def w_out(j, f, te, na):
        return (te[jnp.minimum(j, na[0] - 1)], jnp.where(j < na[0], f, nf - 1), 0)

    return pl.pallas_call(
        _expert_kernel,
        out_shape=SDS((n_rows, d), bf16),
        grid_spec=pltpu.PrefetchScalarGridSpec(
            num_scalar_prefetch=2,
            grid=(n_rows // tr, nf),
            in_specs=[pl.BlockSpec((tr, d), rows), pl.BlockSpec((tr, 1), rows),
                      pl.BlockSpec((1, d, tf), w_in), pl.BlockSpec((1, d, tf), w_in), pl.BlockSpec((1, tf, d), w_out)],
            out_specs=pl.BlockSpec((tr, d), lambda j, f, te, na: (j, 0)),
            scratch_shapes=[pltpu.VMEM((tr, d), f32)],
        ),
        compiler_params=_cparams("arbitrary", "arbitrary"),
        name="moe_experts",
    )(plan["tile_e"], plan["n_act"], xs, gs, wg, wu, wd)


def _combine_kernel(w0_ref, off_ref, x_ref, tok_ref, g_ref, *refs):
    ys_refs, o_ref = refs[:-1], refs[-1]
    b = pl.program_id(0)
    tb = x_ref.shape[0]
    tok = tok_ref[...]
    e1, e2 = tok[:, R_E1:R_E1 + 1], tok[:, R_E2:R_E2 + 1]
    p1 = tok[:, R_C1:R_C1 + 1] + _offset_of(e1, off_ref)
    p2 = tok[:, R_C2:R_C2 + 1] + _offset_of(e2, off_ref)
    lane = lax.broadcasted_iota(jnp.int32, (tb, 2 * tb), 1).astype(f32)
    acc = jnp.zeros(x_ref.shape, f32)
    for e in range(N_EXPERTS):
        base = (w0_ref[b * N_EXPERTS + e] * tb).astype(f32)
        hit1 = jnp.where(e1 == float(e), p1 - base, -1.0) == lane
        hit2 = jnp.where(e2 == float(e), p2 - base, -1.0) == lane
        ys = jnp.concatenate([ys_refs[2 * e][...], ys_refs[2 * e + 1][...]], axis=0)
        acc = acc + _dot(jnp.where(hit1 | hit2, 1.0, 0.0).astype(bf16), ys)
    o_ref[...] = x_ref[...] + g_ref[0] * acc


def _combine(x, tok, gate, ys, plan, l):
    t, d = x.shape
    tb = ROUTE_BLOCK
    nrb = ys.shape[0] // tb
    ne = N_EXPERTS

    def win(e, second):
        def index(i, w0, off):
            return (jnp.minimum(w0[i * ne + e] + second, nrb - 1), 0)
        return pl.BlockSpec((tb, d), index)

    ys_specs = [win(e, s) for e in range(ne) for s in (0, 1)]
    return pl.pallas_call(
        _combine_kernel,
        out_shape=SDS((t, d), f32),
        grid_spec=pltpu.PrefetchScalarGridSpec(
            num_scalar_prefetch=2,
            grid=(t // tb,),
            in_specs=[pl.BlockSpec((tb, d), lambda i, *_: (i, 0)), pl.BlockSpec((tb, LANES), lambda i, *_: (i, 0)),
                      pl.BlockSpec((1, 1, d), lambda i, *_: (i // (l // tb), 0, 0))] + ys_specs,
            out_specs=pl.BlockSpec((tb, d), lambda i, *_: (i, 0)),
        ),
        compiler_params=_cparams("arbitrary"),
        name="moe_combine",
    )(plan["w0"], plan["off"], x, tok, gate, *([ys] * (2 * ne)))


def _mod_params(c, c_ctx, w, b):
    nb = c.shape[0]
    cond = jnp.concatenate([c, c_ctx[None], jnp.zeros((8 - nb - 1, c.shape[1]), f32)], axis=0)
    m = _ada(cond, w, b)
    return [p[:, None, :] for p in jnp.split(m, N_MOD, axis=-1)], nb


def _even_tokens(s, a1, sh1, g1, a2, sh2, g2, w_in, hy_short, filt, hy_skip, w_out, wg, wu, wd, tm, tp, long):
    hw = hy_skip.shape[0]
    ph, pq = _inproj(s, a1, sh1, w_in, [(0, 3 * hw), (3 * hw, w_in.shape[1])], tp)
    u, x0 = _conv_call(functools.partial(_hyprep_kernel, width=hw), ph, hy_short, [hw, hw], tp, "hyena_prep")
    hfb, l1 = filt
    if long:
        y_h = _hyena_long(u, x0, hfb, l1, hy_skip)
        y_f = _fnet_long(pq)
    else:
        y_h = _hyena_short(u, x0, hfb, l1, hy_skip)
        y_f = _fnet_short(pq)
    s = _outproj(s, y_h, y_f, w_out, g1, tm)
    return _ffn(s, a2, sh2, g2, wg, wu, wd, tp, wg.shape[1] // 2)


def _rope_tables(l):
    quarter = HEAD_DIM // 4
    t = jnp.arange(l, dtype=jnp.int32)
    rows = (t // GRID_W).astype(f32)[:, None]
    cols = (t % GRID_W).astype(f32)[:, None]
    inv_freq = ROPE_BASE ** (-jnp.arange(quarter, dtype=f32) / quarter)
    ar, ac = rows * inv_freq[None, :], cols * inv_freq[None, :]
    zero = jnp.zeros_like(ar)
    head = lambda x, y: jnp.concatenate([x, y], axis=1)
    cos = head(head(jnp.cos(ar), jnp.cos(ar)), head(jnp.cos(ac), jnp.cos(ac)))
    sin_a = head(head(-jnp.sin(ar), zero), head(-jnp.sin(ac), zero))
    sin_b = head(head(zero, jnp.sin(ar)), head(zero, jnp.sin(ac)))
    two = lambda x: jnp.concatenate([x, x], axis=1)
    return two(cos), two(sin_a), two(sin_b)


def kernel(x, c, ctx, c_ctx, e_ada_w, e_ada_b, e_norm1, e_norm2, e_w_in, e_hy_short, e_hy_w1, e_hy_b1, e_hy_freq, e_hy_w2, e_hy_b2, e_hy_w3, e_hy_skip, e_w_out, e_ffn_wg, e_ffn_wu, e_ffn_wd, o_ada_w, o_ada_b, o_norm1, o_norm2, o_w_in, o_q_norm, o_k_norm, o_sink, o_sc_conv, o_w_out, o_router_w, o_router_b, o_moe_wg, o_moe_wu, o_moe_wd):
    b, l, d = x.shape
    n_ctx = ctx.shape[1]
    tm = 1024
    tp = 512

    (sh1, sc1, g1, sh2, sc2, g2), nb = _mod_params(c, c_ctx, e_ada_w[0], e_ada_b[0])
    a1 = e_norm1[0] * (1.0 + sc1)
    a2 = e_norm2[0] * (1.0 + sc2)
    hw = e_hy_skip.shape[1]
    fw = d - hw
    cg, sg = _cs(_outer(GROUP_DIM, GROUP_DIM), GROUP_DIM)
    eye = jnp.eye(fw // GROUP_DIM, dtype=f32)
    chan = jnp.concatenate([jnp.kron(eye, cg), jnp.kron(eye, sg)], axis=1) * (1.0 / math.sqrt(GROUP_DIM))
    w_in = jnp.concatenate([e_w_in[0][:, :3 * hw], _mm32(e_w_in[0][:, 3 * hw:], chan)], axis=1).astype(bf16)
    w_out = e_w_out[0].astype(bf16)
    wg, wu, wd = e_ffn_wg[0].astype(bf16), e_ffn_wu[0].astype(bf16), e_ffn_wd[0].astype(bf16)
    filt_args = (e_hy_w1[0], e_hy_b1[0], e_hy_freq[0], e_hy_w2[0], e_hy_b2[0], e_hy_w3[0])
    lat = lambda p: p[:nb]
    cx = lambda p: jnp.broadcast_to(p[nb:nb + 1], (b,) + p.shape[1:])
    x = _even_tokens(x, lat(a1), lat(sh1), lat(g1), lat(a2), lat(sh2), lat(g2), w_in, e_hy_short[0],
                     _hyena_filter(l, *filt_args, tl=1024), e_hy_skip[0], w_out, wg, wu, wd, tm, tp, True)
    ctx = _even_tokens(ctx, cx(a1), cx(sh1), cx(g1), cx(a2), cx(sh2), cx(g2), w_in, e_hy_short[0],
                       _hyena_filter(n_ctx, *filt_args, tl=n_ctx), e_hy_skip[0], w_out, wg, wu, wd, n_ctx, n_ctx, False)

    (sh1, sc1, g1, sh2, sc2, g2), nb = _mod_params(c, c_ctx, o_ada_w[0], o_ada_b[0])
    a1 = o_norm1[0] * (1.0 + sc1)
    a2 = o_norm2[0] * (1.0 + sc2)
    w_in = o_w_in[0].astype(bf16)
    ident = (jnp.ones((n_ctx, LANES), f32), jnp.zeros((n_ctx, LANES), f32), jnp.zeros((n_ctx, LANES), f32))
    _, kc, vc, _ = _inproj_odd(ctx, cx(a1), cx(sh1), w_in, ident, o_q_norm[0], o_k_norm[0], n_ctx)
    q, k, v, s = _inproj_odd(x, lat(a1), lat(sh1), w_in, _rope_tables(l), o_q_norm[0], o_k_norm[0], tp)
    y_att = _attention(q, k, v, kc, vc, o_sink[0])
    sw = s.shape[2] // 3
    (y_sc,) = _conv_call(functools.partial(_sgconv_kernel, width=sw), s, o_sc_conv[0], [sw], 512, "sgconv")
    x = _outproj(x, y_att, y_sc, o_w_out[0].astype(bf16), lat(g1), tm)
    h, tok, rec, cb, tot = _router(x, lat(a2), lat(sh2), o_router_w[0], o_router_b[0], tm)
    plan = _route_plan(cb, tot, b * l, tm)
    xs, gs = _gather(h.reshape(b * l, d), rec, plan)
    ys = _experts(xs, gs, plan, o_moe_wg[0], o_moe_wu[0], o_moe_wd[0], 512)
    return _combine(x.reshape(b * l, d), tok, lat(g2), ys, plan, l).reshape(b, l, d)
```

```python
import functools
import math

import jax
import jax.numpy as jnp
from jax import lax
from jax.experimental import pallas as pl
from jax.experimental.pallas import tpu as pltpu

f32 = jnp.float32
bf16 = jnp.bfloat16
SDS = jax.ShapeDtypeStruct

EPS = 1e-6
N_MOD = 6
GROUP_DIM = 64
HEAD_DIM = 64
N_Q_HEADS = 8
N_KV_HEADS = 2
Q_PER_KV = N_Q_HEADS // N_KV_HEADS
WINDOW = 128
GRID_W = 64
ROPE_BASE = 10000.0
N_EXPERTS = 8
NEG_INF = -1e30
HY_EMB_DIM = 33
HY_DECAY_TARGET = 1e-2
HY_FAST_DECAY_PCT = 0.3
HY_SLOW_DECAY_PCT = 1.5

LANES = 128
DFT_N2 = 128
K1_PER_STEP = 8
VMEM_LIMIT = 56 * 1024 * 1024


def _cparams(*sem):
    return pltpu.CompilerParams(dimension_semantics=sem, vmem_limit_bytes=VMEM_LIMIT)


def _dot(a, b, **kw):
    return jnp.dot(a, b, preferred_element_type=f32, **kw)


def _full_spec(shape):
    nd = len(shape)
    return pl.BlockSpec(shape, lambda *_: (0,) * nd)


def _ada_kernel(c_ref, w_ref, b_ref, o_ref):
    c = c_ref[...]
    s = (c * jax.nn.sigmoid(c)).astype(bf16)
    o_ref[...] = _dot(s, w_ref[...].astype(bf16)) + b_ref[...]


def _ada(cond, w, b):
    d, n = w.shape
    tn = n // 4
    return pl.pallas_call(
        _ada_kernel,
        out_shape=SDS((cond.shape[0], n), f32),
        grid=(n // tn,),
        in_specs=[
            pl.BlockSpec(cond.shape, lambda j: (0, 0)),
            pl.BlockSpec((d, tn), lambda j: (0, j)),
            pl.BlockSpec((1, tn), lambda j: (0, j)),
        ],
        out_specs=pl.BlockSpec((cond.shape[0], tn), lambda j: (0, j)),
        compiler_params=_cparams("arbitrary"),
        name="ada",
    )(cond, w, b.reshape(1, n))


def _modulated(x, a, sh):
    ms = jnp.mean(x * x, axis=-1, keepdims=True)
    return (x * lax.rsqrt(ms + EPS)) * a + sh


def _inproj_kernel(x_ref, a_ref, sh_ref, w_ref, *o_refs, splits):
    h = _modulated(x_ref[0], a_ref[0], sh_ref[0]).astype(bf16)
    for o_ref, (s, e) in zip(o_refs, splits):
        o_ref[0] = _dot(h, w_ref[:, s:e]).astype(o_ref.dtype)


def _inproj(x, a, sh, w, splits, tm):
    b, l, d = x.shape
    n = w.shape[1]
    return pl.pallas_call(
        functools.partial(_inproj_kernel, splits=splits),
        out_shape=[SDS((b, l, e - s), bf16) for s, e in splits],
        grid=(b, l // tm),
        in_specs=[
            pl.BlockSpec((1, tm, d), lambda i, j: (i, j, 0)),
            pl.BlockSpec((1, 1, d), lambda i, j: (i, 0, 0)),
            pl.BlockSpec((1, 1, d), lambda i, j: (i, 0, 0)),
            pl.BlockSpec((d, n), lambda i, j: (0, 0)),
        ],
        out_specs=[pl.BlockSpec((1, tm, e - s), lambda i, j: (i, j, 0)) for s, e in splits],
        compiler_params=_cparams("parallel", "parallel"),
        name="inproj",
    )(x, a, sh, w)


def _norm_rope(t, seg, gain, cos, sin_a, sin_b):
    ms = _dot((t * t).astype(bf16), seg)
    tn = t * lax.rsqrt(ms + EPS) * gain
    w = t.shape[1]
    reps = w // cos.shape[1]
    cos = jnp.concatenate([cos] * reps, axis=1)
    sin_a = jnp.concatenate([sin_a] * reps, axis=1)
    sin_b = jnp.concatenate([sin_b] * reps, axis=1)
    quarter = HEAD_DIM // 4
    return tn * cos + pltpu.roll(tn, w - quarter, 1) * sin_a + pltpu.roll(tn, quarter, 1) * sin_b


def _inproj_odd_kernel(x_ref, a_ref, sh_ref, w_ref, cos_ref, sa_ref, sb_ref, gq_ref, gk_ref,
                       segq_ref, segk_ref, q_ref, k_ref, v_ref, s_ref, *, wq, wk):
    h = _modulated(x_ref[0], a_ref[0], sh_ref[0]).astype(bf16)
    cos, sa, sb = cos_ref[...], sa_ref[...], sb_ref[...]
    q = _norm_rope(_dot(h, w_ref[:, :wq]), segq_ref[...], gq_ref[...], cos, sa, sb)
    q_ref[0] = (q * HEAD_DIM ** -0.5).astype(bf16)
    k = _norm_rope(_dot(h, w_ref[:, wq:wq + wk]), segk_ref[...], gk_ref[...], cos, sa, sb)
    k_ref[0] = k.astype(bf16)
    v_ref[0] = _dot(h, w_ref[:, wq + wk:wq + 2 * wk]).astype(bf16)
    s_ref[0] = _dot(h, w_ref[:, wq + 2 * wk:]).astype(bf16)


def _inproj_odd(x, a, sh, w, tables, gq, gk, tm):
    b, l, d = x.shape
    n = w.shape[1]
    wq, wk = N_Q_HEADS * HEAD_DIM, N_KV_HEADS * HEAD_DIM
    ws = n - wq - 2 * wk
    cos, sa, sb = tables
    seg = lambda width: jnp.kron(jnp.eye(width // HEAD_DIM, dtype=f32),
                                 jnp.full((HEAD_DIM, HEAD_DIM), 1.0 / HEAD_DIM, f32)).astype(bf16)
    row = lambda i, j: (i, j, 0)
    tab = pl.BlockSpec((tm, LANES), lambda i, j: (j, 0))
    return pl.pallas_call(
        functools.partial(_inproj_odd_kernel, wq=wq, wk=wk),
        out_shape=[SDS((b, l, wq), bf16), SDS((b, l, wk), bf16), SDS((b, l, wk), bf16), SDS((b, l, ws), bf16)],
        grid=(b, l // tm),
        in_specs=[
            pl.BlockSpec((1, tm, d), row),
            pl.BlockSpec((1, 1, d), lambda i, j: (i, 0, 0)),
            pl.BlockSpec((1, 1, d), lambda i, j: (i, 0, 0)),
            pl.BlockSpec((d, n), lambda i, j: (0, 0)),
            tab, tab, tab,
            _full_spec((1, wq)), _full_spec((1, wk)), _full_spec((wq, wq)), _full_spec((wk, wk)),
        ],
        out_specs=[pl.BlockSpec((1, tm, wq), row), pl.BlockSpec((1, tm, wk), row),
                   pl.BlockSpec((1, tm, wk), row), pl.BlockSpec((1, tm, ws), row)],
        compiler_params=_cparams("parallel", "parallel"),
        name="inproj_odd",
    )(x, a, sh, w, cos, sa, sb, jnp.tile(gq, wq // HEAD_DIM)[None], jnp.tile(gk, wk // HEAD_DIM)[None],
      seg(wq), seg(wk))


HALO = 16


def _conv3(p, prev_row, next_row, w):
    tl = p.shape[0]
    r = lax.broadcasted_iota(jnp.int32, p.shape, 0)
    pm1 = jnp.where(r == 0, prev_row, pltpu.roll(p, 1, 0))
    pp1 = jnp.where(r == tl - 1, next_row, pltpu.roll(p, tl - 1, 0))
    return pm1 * w[0:1] + p * w[1:2] + pp1 * w[2:3]


def _halo_rows(pp_ref, pn_ref):
    i, n = pl.program_id(1), pl.num_programs(1)
    prev = pp_ref[0].astype(f32)[HALO - 1:HALO]
    nxt = pn_ref[0].astype(f32)[0:1]
    return jnp.where(i > 0, prev, 0.0), jnp.where(i < n - 1, nxt, 0.0)


def _hyprep_kernel(p_ref, pp_ref, pn_ref, w_ref, u_ref, x0_ref, *, width):
    prev, nxt = _halo_rows(pp_ref, pn_ref)
    c = _conv3(p_ref[0].astype(f32), prev, nxt, w_ref[...])
    x0_ref[0] = c[:, :width].astype(bf16)
    u_ref[0] = (c[:, 2 * width:] * c[:, width:2 * width]).astype(bf16)


def _sgconv_kernel(s_ref, sp_ref, sn_ref, w_ref, o_ref, *, width):
    def inner(t):
        return t[:, width:2 * width] * t[:, 2 * width:]
    i, n = pl.program_id(1), pl.num_programs(1)
    s = s_ref[0].astype(f32)
    prev = jnp.where(i > 0, inner(sp_ref[0].astype(f32))[HALO - 1:HALO], 0.0)
    nxt = jnp.where(i < n - 1, inner(sn_ref[0].astype(f32))[0:1], 0.0)
    o_ref[0] = (s[:, :width] * _conv3(inner(s), prev, nxt, w_ref[...])).astype(bf16)


def _conv_call(kernel, p, w, out_widths, tl, name):
    b, l, c = p.shape
    r = tl // HALO
    nh = l // HALO
    row = lambda i, j: (i, j, 0)
    return pl.pallas_call(
        kernel,
        out_shape=[SDS((b, l, ow), bf16) for ow in out_widths],
        grid=(b, l // tl),
        in_specs=[
            pl.BlockSpec((1, tl, c), row),
            pl.BlockSpec((1, HALO, c), lambda i, j: (i, jnp.maximum(j * r - 1, 0), 0)),
            pl.BlockSpec((1, HALO, c), lambda i, j: (i, jnp.minimum((j + 1) * r, nh - 1), 0)),
            _full_spec(w.shape),
        ],
        out_specs=[pl.BlockSpec((1, tl, ow), row) for ow in out_widths],
        compiler_params=_cparams("parallel", "parallel"),
        name=name,
    )(p, p, p, w)


def _lmm_kernel(w_ref, x_ref, o_ref):
    o_ref[0] = _dot(w_ref[...], x_ref[0]).astype(o_ref.dtype)


def _lmm(w, x, tn, out_dtype=bf16):
    m, k = w.shape
    b, _, n = x.shape
    return pl.pallas_call(
        _lmm_kernel,
        out_shape=SDS((b, m, n), out_dtype),
        grid=(b, n // tn),
        in_specs=[_full_spec((m, k)), pl.BlockSpec((1, k, tn), lambda i, j: (i, 0, j))],
        out_specs=pl.BlockSpec((1, m, tn), lambda i, j: (i, 0, j)),
        compiler_params=_cparams("parallel", "parallel"),
        name="lmm",
    )(w, x)


def _filt_kernel(z_ref, t_ref, w1_ref, b1_ref, fr_ref, w2_ref, b2_ref, w3_ref, dl_ref, hfb_ref, l1_ref, *, width):
    i = pl.program_id(0)
    hp = lax.Precision.HIGHEST
    fr = fr_ref[...]
    h = jnp.sin(fr * (_dot(z_ref[...], w1_ref[...], precision=hp) + b1_ref[...]))
    h = jnp.sin(fr * (_dot(h, w2_ref[...], precision=hp) + b2_ref[...]))
    h = _dot(h.astype(bf16), w3_ref[...])
    decay = jnp.exp(-t_ref[...] * dl_ref[...])
    hf = h[:, :width] * decay
    hb = h[:, width:] * decay
    tl = hf.shape[0]
    row = lax.broadcasted_iota(jnp.int32, hb.shape, 0) + i * tl
    hb = jnp.where(row == 0, 0.0, hb)
    hfb_ref[0] = hf.astype(bf16)
    hfb_ref[1] = hb.astype(bf16)
    part = jnp.sum(jnp.abs(hf), axis=0, keepdims=True) + jnp.sum(jnp.abs(hb), axis=0, keepdims=True)

    @pl.when(i == 0)
    def _():
        l1_ref[...] = jnp.zeros_like(l1_ref)

    l1_ref[...] += part


def _hyena_filter(l, w1, b1, freq, w2, b2, w3, tl):
    hid = w1.shape[1]
    width = w3.shape[1] // 2
    t = jnp.linspace(0.0, 1.0, l, dtype=f32)[:, None]
    bands = (HY_EMB_DIM - 1) // 2
    ang = (2.0 * math.pi / l) * jnp.arange(l, dtype=f32)[:, None] * jnp.linspace(1e-4, bands - 1, bands, dtype=f32)[None, :]
    z = jnp.concatenate([t, jnp.cos(ang), -jnp.sin(ang)], axis=-1)
    z = jnp.pad(z, ((0, 0), (0, LANES - HY_EMB_DIM)))
    pad_h = LANES - hid
    w1p = jnp.pad(w1.astype(f32), ((0, LANES - HY_EMB_DIM), (0, pad_h)))
    b1p = jnp.pad(b1.astype(f32), (0, pad_h))[None]
    frp = jnp.pad(freq.astype(f32), (0, pad_h))[None]
    w2p = jnp.pad(w2.astype(f32), ((0, pad_h), (0, pad_h)))
    b2p = jnp.pad(b2.astype(f32), (0, pad_h))[None]
    w3p = jnp.pad(w3, ((0, pad_h), (0, 0))).astype(bf16)
    max_decay = math.log(HY_DECAY_TARGET) / HY_FAST_DECAY_PCT
    min_decay = math.log(HY_DECAY_TARGET) / HY_SLOW_DECAY_PCT
    deltas = jnp.abs(jnp.linspace(min_decay, max_decay, width, dtype=f32))[None]
    return pl.pallas_call(
        functools.partial(_filt_kernel, width=width),
        out_shape=[SDS((2, l, width), bf16), SDS((1, width), f32)],
        grid=(l // tl,),
        in_specs=[
            pl.BlockSpec((tl, LANES), lambda i: (i, 0)),
            pl.BlockSpec((tl, 1), lambda i: (i, 0)),
            _full_spec(w1p.shape), _full_spec(b1p.shape), _full_spec(frp.shape),
            _full_spec(w2p.shape), _full_spec(b2p.shape), _full_spec(w3p.shape), _full_spec(deltas.shape),
        ],
        out_specs=[pl.BlockSpec((2, tl, width), lambda i: (0, i, 0)), pl.BlockSpec((1, width), lambda i: (0, 0))],
        compiler_params=_cparams("arbitrary"),
        name="hyena_filter",
    )(z, t, w1p, b1p, frp, w2p, b2p, w3p, deltas)


def _cs(num, den):
    ang = (2.0 * math.pi / den) * (num % den).astype(f32)
    return jnp.cos(ang), jnp.sin(ang)


def _outer(n_rows, n_cols):
    return jnp.arange(n_rows, dtype=jnp.int32)[:, None] * jnp.arange(n_cols, dtype=jnp.int32)[None, :]


def _interleave_rows(a, b):
    return jnp.stack([a, b], axis=1).reshape(2 * a.shape[0], a.shape[1])


def _interleave_cols(a, b):
    return jnp.stack([a, b], axis=2).reshape(a.shape[0], 2 * a.shape[1])


def _inner_tables(n1, scale):
    c2, s2 = _cs(_outer(DFT_N2, DFT_N2), DFT_N2)
    ct, st = _cs(_outer(n1, DFT_N2), n1 * DFT_N2)
    return c2 * scale, -s2 * scale, ct[:, None, :], -st[:, None, :]


def _gmat(f2r, f2i, tr, ti):
    return f2r * tr - f2i * ti, f2r * ti + f2i * tr


def _gblock(gr, gi):
    return jnp.concatenate([jnp.concatenate([gr, -gi], axis=1), jnp.concatenate([gi, gr], axis=1)], axis=0)


N2_PER_STEP = 8


def _s1_kernel(fc_ref, fs_ref, x_ref, o_ref):
    fc, fs = fc_ref[...], fs_ref[...]
    for j in range(N2_PER_STEP):
        slab = x_ref[0, :, j, :]
        o_ref[0, :, 0, j, :] = _dot(fc, slab).astype(o_ref.dtype)
        o_ref[0, :, 1, j, :] = _dot(fs, slab).astype(o_ref.dtype)


def _s1(fc, fs, x):
    m, k = fc.shape
    b, _, n2, c = x.shape
    nb = N2_PER_STEP
    return pl.pallas_call(
        _s1_kernel,
        out_shape=SDS((b, m, 2, n2, c), bf16),
        grid=(b, n2 // nb),
        in_specs=[_full_spec((m, k)), _full_spec((m, k)), pl.BlockSpec((1, k, nb, c), lambda i, j: (i, 0, j, 0))],
        out_specs=pl.BlockSpec((1, m, 2, nb, c), lambda i, j: (i, 0, 0, j, 0)),
        compiler_params=_cparams("parallel", "parallel"),
        name="dft_outer",
    )(fc, fs, x)


def _s2f_kernel(a_ref, f2r_ref, f2i_ref, tr_ref, ti_ref, k_ref):
    f2r, f2i = f2r_ref[...], f2i_ref[...]
    h2 = DFT_N2
    for j in range(K1_PER_STEP):
        gr, gi = _gmat(f2r, f2i, tr_ref[j], ti_ref[j])
        g = _gblock(gr, gi).astype(bf16)
        hf, hb = _dot(g, a_ref[0, j]), _dot(g, a_ref[1, j])
        k_ref[j, :h2, :] = hf[:h2] + hb[:h2]
        k_ref[j, h2:, :] = hf[h2:] - hb[h2:]


def _s23_kernel(a_ref, kh_ref, f2r_ref, f2i_ref, tr_ref, ti_ref, o_ref):
    f2r, f2i = f2r_ref[...], f2i_ref[...]
    h2 = DFT_N2
    for j in range(K1_PER_STEP):
        gr, gi = _gmat(f2r, f2i, tr_ref[j], ti_ref[j])
        uh = _dot(_gblock(gr, gi).astype(bf16), a_ref[0, j])
        ur, ui = uh[:h2], uh[h2:]
        kr, ki = kh_ref[j, :h2, :], kh_ref[j, h2:, :]
        yh = jnp.concatenate([ur * kr - ui * ki, ur * ki + ui * kr], axis=0).astype(bf16)
        grt, git = gr.T, gi.T
        res = _dot(_gblock(grt, -git).astype(bf16), yh).astype(o_ref.dtype)
        o_ref[0, :, 2 * j, :] = res[:h2]
        o_ref[0, :, 2 * j + 1, :] = res[h2:]


def _s4t_kernel(f4_ref, bq_ref, u_ref, x0_ref, sc_ref, sk_ref, o_ref):
    f4, sc, sk = f4_ref[...], sc_ref[...], sk_ref[...]
    for j in range(N2_PER_STEP):
        y = _dot(f4, bq_ref[0, j])
        u = u_ref[0, :, j, :].astype(f32)
        o_ref[0, :, j, :] = (x0_ref[0, :, j, :].astype(f32) * (y * sc + u * sk)).astype(o_ref.dtype)


def _s4_kernel(f4_ref, bq_ref, u_ref, x0_ref, sc_ref, sk_ref, o_ref):
    y = _dot(f4_ref[...], bq_ref[0])
    u = u_ref[0].astype(f32)
    o_ref[0] = (x0_ref[0].astype(f32) * (y * sc_ref[...] + u * sk_ref[...])).astype(o_ref.dtype)


def _s4(f4, bq, u, x0, scale, skip, tn):
    m, k = f4.shape
    b, _, n = bq.shape
    reps = tn // scale.shape[1]
    col = lambda i, j: (i, 0, j)
    return pl.pallas_call(
        _s4_kernel,
        out_shape=SDS((b, m, n), bf16),
        grid=(b, n // tn),
        in_specs=[_full_spec((m, k)), pl.BlockSpec((1, k, tn), col), pl.BlockSpec((1, m, tn), col),
                  pl.BlockSpec((1, m, tn), col), _full_spec((1, tn)), _full_spec((1, tn))],
        out_specs=pl.BlockSpec((1, m, tn), col),
        compiler_params=_cparams("parallel", "parallel"),
        name="hyena_s4",
    )(f4, bq, u, x0, jnp.tile(scale, (1, reps)), jnp.tile(skip, (1, reps)))


def _hyena_long(u, x0, hfb, l1, skip):
    b, l, w = u.shape
    n2 = DFT_N2
    half = l // n2
    n1 = 2 * half
    kb = K1_PER_STEP
    nb = N2_PER_STEP
    c1, s1 = _cs(_outer(n1, half), n1)
    fc, fs = c1.astype(bf16), (-s1).astype(bf16)
    f2r, f2i, tr, ti = _inner_tables(n1, 1.0)
    tab_specs = [_full_spec((n2, n2)), _full_spec((n2, n2)),
                 pl.BlockSpec((kb, 1, n2), lambda i, *_: (i, 0, 0)), pl.BlockSpec((kb, 1, n2), lambda i, *_: (i, 0, 0))]

    a_f = _s1(fc, fs, hfb.reshape(2, half, n2, w)).reshape(2, n1, 2 * n2, w)
    kh = pl.pallas_call(
        _s2f_kernel,
        out_shape=SDS((n1, 2 * n2, w), f32),
        grid=(n1 // kb,),
        in_specs=[pl.BlockSpec((2, kb, 2 * n2, w), lambda i: (0, i, 0, 0))] + tab_specs,
        out_specs=pl.BlockSpec((kb, 2 * n2, w), lambda i: (i, 0, 0)),
        compiler_params=_cparams("parallel"),
        name="hyena_s2f",
    )(a_f, f2r, f2i, tr, ti)

    u4, x04 = u.reshape(b, half, n2, w), x0.reshape(b, half, n2, w)
    a_u = _s1(fc, fs, u4).reshape(b, n1, 2 * n2, w)
    bq = pl.pallas_call(
        _s23_kernel,
        out_shape=SDS((b, n2, 2 * n1, w), bf16),
        grid=(n1 // kb, b),
        in_specs=[pl.BlockSpec((1, kb, 2 * n2, w), lambda i, j: (j, i, 0, 0)),
                  pl.BlockSpec((kb, 2 * n2, w), lambda i, j: (i, 0, 0))] + tab_specs,
        out_specs=pl.BlockSpec((1, n2, 2 * kb, w), lambda i, j: (j, 0, i, 0)),
        compiler_params=_cparams("parallel", "parallel"),
        name="hyena_s23",
    )(a_u, kh, f2r, f2i, tr, ti)

    c4, s4 = _cs(_outer(half, n1), n1)
    f4 = (_interleave_cols(c4, -s4) * (1.0 / (n1 * n2))).astype(bf16)
    slab = pl.BlockSpec((1, half, nb, w), lambda i, j: (i, 0, j, 0))
    y = pl.pallas_call(
        _s4t_kernel,
        out_shape=SDS((b, half, n2, w), bf16),
        grid=(b, n2 // nb),
        in_specs=[_full_spec(f4.shape), pl.BlockSpec((1, nb, 2 * n1, w), lambda i, j: (i, j, 0, 0)), slab, slab,
                  _full_spec((1, w)), _full_spec((1, w))],
        out_specs=slab,
        compiler_params=_cparams("parallel", "parallel"),
        name="hyena_s4",
    )(f4, bq, u4, x04, 1.0 / l1, skip[None])
    return y.reshape(b, l, w)


def _cmul_kernel(uh_ref, hh_ref, o_ref, *, nf):
    ur, ui = uh_ref[0, :nf, :], uh_ref[0, nf:, :]
    kr = hh_ref[0, :nf, :] + hh_ref[1, :nf, :]
    ki = hh_ref[0, nf:, :] - hh_ref[1, nf:, :]
    o_ref[0, :nf, :] = (ur * kr - ui * ki).astype(o_ref.dtype)
    o_ref[0, nf:, :] = (ur * ki + ui * kr).astype(o_ref.dtype)


def _hyena_short(u, x0, hfb, l1, skip):
    b, l, w = u.shape
    nf = 2 * l
    c, s = _cs(_outer(nf, l), nf)
    fwd = jnp.concatenate([c, -s], axis=0).astype(bf16)
    uh = _lmm(fwd, u, tn=w, out_dtype=f32)
    hh = _lmm(fwd, hfb, tn=w, out_dtype=f32)
    yh = pl.pallas_call(
        functools.partial(_cmul_kernel, nf=nf),
        out_shape=SDS((b, 2 * nf, w), bf16),
        grid=(b,),
        in_specs=[pl.BlockSpec((1, 2 * nf, w), lambda i: (i, 0, 0)), _full_spec((2, 2 * nf, w))],
        out_specs=pl.BlockSpec((1, 2 * nf, w), lambda i: (i, 0, 0)),
        compiler_params=_cparams("parallel"),
        name="hyena_cmul",
    )(uh, hh)
    ci, si = _cs(_outer(l, nf), nf)
    inv = (jnp.concatenate([ci, -si], axis=1) * (1.0 / nf)).astype(bf16)
    return _s4(inv, yh, u, x0, 1.0 / l1, skip[None], tn=w)


def _f2_kernel(a_ref, f2r_ref, f2i_ref, tr_ref, ti_ref, o_ref, *, width):
    f2r, f2i = f2r_ref[...], f2i_ref[...]
    for j in range(K1_PER_STEP):
        gr, gi = _gmat(f2r, f2i, tr_ref[j], ti_ref[j])
        gcat = jnp.concatenate([gr, gi], axis=1).astype(bf16)
        s0 = a_ref[0, j, 0].astype(f32)
        s1 = a_ref[0, j, 1].astype(f32)
        x2 = jnp.concatenate([s0[:, :width] - s1[:, width:], s0[:, width:] + s1[:, :width]], axis=0).astype(bf16)
        o_ref[0, :, j, :] = _dot(gcat, x2).astype(o_ref.dtype)


def _fnet_long(pq):
    b, l, w2 = pq.shape
    w = w2 // 2
    n2 = DFT_N2
    n1 = l // n2
    kb = K1_PER_STEP
    c1, s1 = _cs(_outer(n1, n1), n1)
    a = _s1(c1.astype(bf16), s1.astype(bf16), pq.reshape(b, n1, n2, w2))
    f2r, f2i, tr, ti = _inner_tables(n1, 1.0 / math.sqrt(l))
    out = pl.pallas_call(
        functools.partial(_f2_kernel, width=w),
        out_shape=SDS((b, n2, n1, w), bf16),
        grid=(b, n1 // kb),
        in_specs=[pl.BlockSpec((1, kb, 2, n2, w2), lambda i, j: (i, j, 0, 0, 0)),
                  _full_spec((n2, n2)), _full_spec((n2, n2)),
                  pl.BlockSpec((kb, 1, n2), lambda i, j: (j, 0, 0)), pl.BlockSpec((kb, 1, n2), lambda i, j: (j, 0, 0))],
        out_specs=pl.BlockSpec((1, n2, kb, w), lambda i, j: (i, 0, j, 0)),
        compiler_params=_cparams("parallel", "parallel"),
        name="fnet_f2",
    )(a, f2r, f2i, tr, ti)
    return out.reshape(b, l, w)


def _fnet_short(pq):
    b, l, w2 = pq.shape
    w = w2 // 2
    c, s = _cs(_outer(l, l), l)
    m = (jnp.concatenate([c, -s], axis=1) * (1.0 / math.sqrt(l))).astype(bf16)
    x = jnp.concatenate([pq[..., :w], pq[..., w:]], axis=1)
    return _lmm(m, x, tn=w)


def _mm32_kernel(a_ref, b_ref, o_ref):
    o_ref[...] = _dot(a_ref[...], b_ref[...], precision=lax.Precision.HIGHEST)


def _mm32(a, b):
    return pl.pallas_call(
        _mm32_kernel,
        out_shape=SDS((a.shape[0], b.shape[1]), f32),
        in_specs=[_full_spec(a.shape), _full_spec(b.shape)],
        out_specs=_full_spec((a.shape[0], b.shape[1])),
        grid=(1,),
        compiler_params=_cparams("arbitrary"),
        name="mm32",
    )(a, b)


def _outproj_kernel(x_ref, ya_ref, yb_ref, wa_ref, wb_ref, g_ref, o_ref):
    acc = _dot(ya_ref[0], wa_ref[...]) + _dot(yb_ref[0], wb_ref[...])
    o_ref[0] = x_ref[0] + g_ref[0] * acc


def _outproj(x, ya, yb, w, gate, tm):
    b, l, d = x.shape
    ka, kb_ = ya.shape[2], yb.shape[2]
    row = lambda i, j: (i, j, 0)
    return pl.pallas_call(
        _outproj_kernel,
        out_shape=SDS((b, l, d), f32),
        grid=(b, l // tm),
        in_specs=[pl.BlockSpec((1, tm, d), row), pl.BlockSpec((1, tm, ka), row), pl.BlockSpec((1, tm, kb_), row),
                  _full_spec((ka, d)), _full_spec((kb_, d)), pl.BlockSpec((1, 1, d), lambda i, j: (i, 0, 0))],
        out_specs=pl.BlockSpec((1, tm, d), row),
        compiler_params=_cparams("parallel", "parallel"),
        name="outproj",
    )(x, ya, yb, w[:ka], w[ka:], gate)


def _ffn_kernel(x_ref, a_ref, sh_ref, g_ref, wg_ref, wu_ref, wd_ref, o_ref, h_ref, acc_ref):
    f = pl.program_id(2)

    @pl.when(f == 0)
    def _():
        h_ref[...] = _modulated(x_ref[0], a_ref[0], sh_ref[0]).astype(bf16)
        acc_ref[...] = jnp.zeros_like(acc_ref)

    h = h_ref[...]
    g = _dot(h, wg_ref[...])
    u = _dot(h, wu_ref[...])
    acc_ref[...] += _dot((g * jax.nn.sigmoid(g) * u).astype(bf16), wd_ref[...])

    @pl.when(f == pl.num_programs(2) - 1)
    def _():
        o_ref[0] = x_ref[0] + g_ref[0] * acc_ref[...]


def _ffn(x, a, sh, gate, wg, wu, wd, tm, tf):
    b, l, d = x.shape
    ff = wg.shape[1]
    row = lambda i, j, f: (i, j, 0)
    vec = lambda i, j, f: (i, 0, 0)
    return pl.pallas_call(
        _ffn_kernel,
        out_shape=SDS((b, l, d), f32),
        grid=(b, l // tm, ff // tf),
        in_specs=[pl.BlockSpec((1, tm, d), row), pl.BlockSpec((1, 1, d), vec), pl.BlockSpec((1, 1, d), vec),
                  pl.BlockSpec((1, 1, d), vec),
                  pl.BlockSpec((d, tf), lambda i, j, f: (0, f)), pl.BlockSpec((d, tf), lambda i, j, f: (0, f)),
                  pl.BlockSpec((tf, d), lambda i, j, f: (f, 0))],
        out_specs=pl.BlockSpec((1, tm, d), row),
        scratch_shapes=[pltpu.VMEM((tm, d), bf16), pltpu.VMEM((tm, d), f32)],
        compiler_params=_cparams("parallel", "parallel", "arbitrary"),
        name="ffn",
    )(x, a, sh, gate, wg, wu, wd)


def _attn_kernel(sink_ref, q_ref, kp_ref, kc_ref, kn_ref, vp_ref, vc_ref, vn_ref, ck_ref, cv_ref, o_ref):
    i, n = pl.program_id(1), pl.num_programs(1)
    tq, halo = q_ref.shape[1], kp_ref.shape[1]
    n_ctx = ck_ref.shape[1]
    n_loc = tq + 2 * halo
    r = lax.broadcasted_iota(jnp.int32, (tq, n_loc + n_ctx), 0)
    c = lax.broadcasted_iota(jnp.int32, (tq, n_loc + n_ctx), 1)
    lo = jnp.where(i > 0, 0, halo)
    hi = jnp.where(i < n - 1, n_loc, halo + tq)
    mask = ((c >= r) & (c <= r + 2 * WINDOW) & (c >= lo) & (c < hi)) | (c >= n_loc)
    keys = jnp.concatenate([kp_ref[0], kc_ref[0], kn_ref[0], ck_ref[0]], axis=0)
    vals = jnp.concatenate([vp_ref[0], vc_ref[0], vn_ref[0], cv_ref[0]], axis=0)
    q = q_ref[0]
    outs = []
    for hq in range(N_Q_HEADS):
        hk = hq // Q_PER_KV
        qh = q[:, hq * HEAD_DIM:(hq + 1) * HEAD_DIM]
        kh = keys[:, hk * HEAD_DIM:(hk + 1) * HEAD_DIM]
        vh = vals[:, hk * HEAD_DIM:(hk + 1) * HEAD_DIM]
        s = lax.dot_general(qh, kh, (((1,), (1,)), ((), ())), preferred_element_type=f32)
        s = jnp.where(mask, s, NEG_INF)
        sink = sink_ref[hq]
        m = jnp.maximum(jnp.max(s, axis=-1, keepdims=True), sink)
        p = jnp.exp(s - m)
        denom = jnp.sum(p, axis=-1, keepdims=True) + jnp.exp(sink - m)
        outs.append(_dot(p.astype(bf16), vh) / denom)
    o_ref[0] = jnp.concatenate(outs, axis=1).astype(o_ref.dtype)


def _attention(q, k, v, kc, vc, sink):
    b, l, wq = q.shape
    wk = k.shape[2]
    n_ctx = kc.shape[1]
    halo = WINDOW
    reps = 2
    tq = reps * halo
    nb = l // tq
    nh = l // halo
    cur = lambda i, j: (i, j, 0)
    prv = lambda i, j: (i, jnp.maximum(j * reps - 1, 0), 0)
    nxt = lambda i, j: (i, jnp.minimum((j + 1) * reps, nh - 1), 0)
    kv = lambda f: pl.BlockSpec((1, tq if f is cur else halo, wk), f)
    ctx = pl.BlockSpec((1, n_ctx, wk), lambda i, j: (i, 0, 0))
    return pl.pallas_call(
        _attn_kernel,
        out_shape=SDS((b, l, wq), bf16),
        grid=(b, nb),
        in_specs=[pl.BlockSpec(memory_space=pltpu.SMEM), pl.BlockSpec((1, tq, wq), cur),
                  kv(prv), kv(cur), kv(nxt), kv(prv), kv(cur), kv(nxt), ctx, ctx],
        out_specs=pl.BlockSpec((1, tq, wq), cur),
        compiler_params=_cparams("parallel", "parallel"),
        name="attention",
    )(sink.astype(f32), q, k, k, k, v, v, v, kc, vc)


COMBINE_WIN = 128
GATHER_SLOTS = 8
ROUTE_BLOCK = 256
EXPERT_ROWS = 1024
R_E1, R_E2, R_C1, R_C2, R_G1, R_G2 = range(6)


def _router_kernel(x_ref, a_ref, sh_ref, rw_ref, rb_ref, h_ref, tok_ref, rec_ref, cb_ref, tot_ref, carry_ref):
    @pl.when((pl.program_id(0) == 0) & (pl.program_id(1) == 0))
    def _():
        carry_ref[...] = jnp.zeros_like(carry_ref)

    h = _modulated(x_ref[0], a_ref[0], sh_ref[0])
    h_ref[0] = h.astype(bf16)
    logits = _dot(h, rw_ref[...], precision=lax.Precision.HIGHEST) + rb_ref[...]
    lane = lax.broadcasted_iota(jnp.int32, logits.shape, 1).astype(f32)
    m1 = jnp.max(logits, axis=-1, keepdims=True)
    i1 = jnp.min(jnp.where(logits == m1, lane, float(LANES)), axis=-1, keepdims=True)
    rest = jnp.where(lane == i1, NEG_INF, logits)
    m2 = jnp.max(rest, axis=-1, keepdims=True)
    i2 = jnp.min(jnp.where(rest == m2, lane, float(LANES)), axis=-1, keepdims=True)
    e2 = jnp.exp(m2 - m1)
    g1 = 1.0 / (1.0 + e2)
    sel = jnp.where((lane == i1) | (lane == i2), 1.0, 0.0)
    tm = sel.shape[0]
    tri = jnp.where(lax.broadcasted_iota(jnp.int32, (tm, tm), 0) > lax.broadcasted_iota(jnp.int32, (tm, tm), 1), 1.0, 0.0)
    cnt = _dot(tri.astype(bf16), sel.astype(bf16)) + carry_ref[...]
    c1 = jnp.sum(jnp.where(lane == i1, cnt, 0.0), axis=-1, keepdims=True)
    c2 = jnp.sum(jnp.where(lane == i2, cnt, 0.0), axis=-1, keepdims=True)
    rec = jnp.zeros_like(logits)
    for k, v in ((R_E1, i1), (R_E2, i2), (R_C1, c1), (R_C2, c2), (R_G1, g1), (R_G2, e2 * g1)):
        rec = rec + jnp.where(lane == float(k), v, 0.0)
    tok_ref[...] = rec
    rec_ref[...] = rec.T[:rec_ref.shape[0]]
    cb_ref[0] = jnp.zeros(cb_ref.shape[1:], f32)
    for s in range(tm // ROUTE_BLOCK):
        cb_ref[0, s:s + 1, :] = cnt[s * ROUTE_BLOCK:s * ROUTE_BLOCK + 1]
    total = carry_ref[...] + jnp.sum(sel, axis=0, keepdims=True)
    carry_ref[...] = total
    tot_ref[...] = total


def _router(x, a, sh, rw, rb, tm):
    b, l, d = x.shape
    e = rw.shape[1]
    nl = l // tm
    rwp = jnp.pad(rw.astype(f32), ((0, 0), (0, LANES - e)))
    rbp = jnp.pad(rb.astype(f32), (0, LANES - e), constant_values=NEG_INF)[None]
    row = lambda i, j: (i, j, 0)
    vec = lambda i, j: (i, 0, 0)
    return pl.pallas_call(
        _router_kernel,
        out_shape=[SDS((b, l, d), bf16), SDS((b * l, LANES), f32), SDS((8, b * l), f32),
                   SDS((b * nl, 8, LANES), f32), SDS((1, LANES), f32)],
        grid=(b, nl),
        in_specs=[pl.BlockSpec((1, tm, d), row), pl.BlockSpec((1, 1, d), vec), pl.BlockSpec((1, 1, d), vec),
                  _full_spec(rwp.shape), _full_spec(rbp.shape)],
        out_specs=[pl.BlockSpec((1, tm, d), row), pl.BlockSpec((tm, LANES), lambda i, j: (i * nl + j, 0)),
                   pl.BlockSpec((8, tm), lambda i, j: (0, i * nl + j)),
                   pl.BlockSpec((1, 8, LANES), lambda i, j: (i * nl + j, 0, 0)), _full_spec((1, LANES))],
        scratch_shapes=[pltpu.VMEM((1, LANES), f32)],
        compiler_params=_cparams("arbitrary", "arbitrary"),
        name="router",
    )(x, a, sh, rwp, rbp)


def _route_plan(cb, tot, n_tok, tm):
    ne, tg, tr = N_EXPERTS, ROUTE_BLOCK, EXPERT_ROWS
    i32 = jnp.int32
    cbl = cb[:, :tm // tg, :ne].reshape(-1, ne).astype(i32)
    cnt = tot[0, :ne].astype(i32)
    cbx = jnp.concatenate([cbl, cnt[None]], axis=0)
    padded = ((cnt + tr - 1) // tr) * tr
    ends = jnp.cumsum(padded)
    off = ends - padded
    n_tiles = 2 * n_tok // tr + ne
    n_act = (ends[-1] // tr).astype(i32)[None]
    owner = lambda row0: jnp.minimum(jnp.sum(ends[None, :] <= row0[:, None], axis=1), ne - 1).astype(i32)
    tile_e = owner(jnp.arange(n_tiles, dtype=i32) * tr)
    g0 = jnp.arange(n_tiles * (tr // tg), dtype=i32) * tg
    ge = owner(g0)
    rho0 = g0 - off[ge]
    cbe = cbx[:, ge]
    blo = jnp.sum(cbe[1:] <= rho0[None], axis=0)
    bhi = jnp.sum(cbe[:-1] < (rho0 + tg)[None], axis=0) - 1
    valid = (g0 < ends[-1]) & (rho0 < cnt[ge])
    nb = jnp.where(valid, jnp.maximum(bhi - blo + 1, 0), 0).astype(i32)
    blo = jnp.where(valid, blo, 0).astype(i32)
    w0 = ((off[None, :] + cbl) // COMBINE_WIN).astype(i32).reshape(-1)
    return dict(off=off.astype(i32), n_act=n_act, tile_e=tile_e, blo=blo, nb=nb, w0=w0, n_rows=n_tiles * tr)


def _offset_of(idx, off_ref):
    o = jnp.zeros_like(idx)
    for e in range(N_EXPERTS):
        o = o + jnp.where(idx == float(e), off_ref[e].astype(f32), 0.0)
    return o


def _gather_kernel(blo_ref, nb_ref, off_ref, rec_ref, h_hbm, xs_ref, gs_ref, hbuf, sem, acc_ref, gacc_ref):
    j, nj = pl.program_id(0), pl.num_programs(0)
    tg, tb = xs_ref.shape[0], hbuf.shape[2]
    half = j % 2
    nb, b0 = nb_ref[j], blo_ref[j]
    rows = (lax.broadcasted_iota(jnp.int32, (tg, 1), 0) + j * tg).astype(f32)
    acc_ref[...] = jnp.zeros_like(acc_ref)
    gacc_ref[...] = jnp.zeros_like(gacc_ref)

    def copy(first, k, hf, slot):
        return pltpu.make_async_copy(h_hbm.at[pl.ds((first + k) * tb, tb)], hbuf.at[hf, slot], sem.at[hf, slot])

    def prefetch(tile, hf):
        first, count = blo_ref[tile], nb_ref[tile]
        for s in range(GATHER_SLOTS):
            @pl.when(s < count)
            def _():
                copy(first, s, hf, s).start()

    @pl.when(j == 0)
    def _():
        prefetch(0, 0)

    @pl.when(j + 1 < nj)
    def _():
        prefetch(j + 1, 1 - half)

    def body(k, carry):
        slot = k % GATHER_SLOTS

        @pl.when(k >= GATHER_SLOTS)
        def _():
            copy(b0, k, half, slot).start()

        copy(b0, k, half, slot).wait()
        rec = rec_ref[:, pl.ds(pl.multiple_of((b0 + k) * tb, tb), tb)]
        field = lambda r: rec[r:r + 1]
        m1 = (field(R_C1) + _offset_of(field(R_E1), off_ref)) == rows
        m2 = (field(R_C2) + _offset_of(field(R_E2), off_ref)) == rows
        acc_ref[...] += _dot(jnp.where(m1 | m2, 1.0, 0.0).astype(bf16), hbuf[half, slot])
        gacc_ref[...] += jnp.sum(jnp.where(m1, field(R_G1), 0.0) + jnp.where(m2, field(R_G2), 0.0), axis=-1, keepdims=True)
        return carry

    lax.fori_loop(0, nb, body, 0)
    xs_ref[...] = acc_ref[...].astype(xs_ref.dtype)
    gs_ref[...] = gacc_ref[...]


def _gather(h, rec, plan):
    t, d = h.shape
    tg = ROUTE_BLOCK
    n_rows = plan["n_rows"]
    return pl.pallas_call(
        _gather_kernel,
        out_shape=[SDS((n_rows, d), bf16), SDS((n_rows, 1), f32)],
        grid_spec=pltpu.PrefetchScalarGridSpec(
            num_scalar_prefetch=3,
            grid=(n_rows // tg,),
            in_specs=[pl.BlockSpec(rec.shape, lambda j, *_: (0, 0)), pl.BlockSpec(memory_space=pl.ANY)],
            out_specs=[pl.BlockSpec((tg, d), lambda j, *_: (j, 0)), pl.BlockSpec((tg, 1), lambda j, *_: (j, 0))],
            scratch_shapes=[pltpu.VMEM((2, GATHER_SLOTS, tg, d), bf16), pltpu.SemaphoreType.DMA((2, GATHER_SLOTS)),
                            pltpu.VMEM((tg, d), f32), pltpu.VMEM((tg, 1), f32)],
        ),
        compiler_params=_cparams("arbitrary"),
        name="moe_gather",
    )(plan["blo"], plan["nb"], plan["off"], rec, h)


def _expert_kernel(te_ref, na_ref, xs_ref, gs_ref, wg_ref, wu_ref, wd_ref, o_ref, acc_ref):
    j, f = pl.program_id(0), pl.program_id(1)
    last = f == pl.num_programs(1) - 1
    active = j < na_ref[0]

    @pl.when(active)
    def _():
        @pl.when(f == 0)
        def _():
            acc_ref[...] = jnp.zeros_like(acc_ref)

        xs = xs_ref[...]
        g = _dot(xs, wg_ref[0].astype(bf16))
        u = _dot(xs, wu_ref[0].astype(bf16))
        acc_ref[...] += _dot((g * jax.nn.sigmoid(g) * u).astype(bf16), wd_ref[0].astype(bf16))

        @pl.when(last)
        def _():
            o_ref[...] = (acc_ref[...] * gs_ref[...]).astype(o_ref.dtype)

    @pl.when(jnp.logical_not(active) & last)
    def _():
        o_ref[...] = jnp.zeros_like(o_ref)


def _experts(xs, gs, plan, wg, wu, wd, tf):
    n_rows, d = xs.shape
    ff = wg.shape[2]
    tr = EXPERT_ROWS
    nf = ff // tf

    def live(j, na):
        return jnp.maximum(jnp.minimum(j, na[0] - 1), 0)

    def rows(j, f, te, na):
        return (live(j, na), 0)

    def w_in(j, f, te, na):
        return (te[live(j, na)], 0, jnp.where(j < na[0], f, nf - 1))

    def w_out(j, f, te, na):
        return (te[live(j, na)], jnp.where(j < na[0], f, nf - 1), 0)

    return pl.pallas_call(
        _expert_kernel,
        out_shape=SDS((n_rows, d), bf16),
        grid_spec=pltpu.PrefetchScalarGridSpec(
            num_scalar_prefetch=2,
            grid=(n_rows // tr, nf),
            in_specs=[pl.BlockSpec((tr, d), rows), pl.BlockSpec((tr, 1), rows),
                      pl.BlockSpec((1, d, tf), w_in), pl.BlockSpec((1, d, tf), w_in), pl.BlockSpec((1, tf, d), w_out)],
            out_specs=pl.BlockSpec((tr, d), lambda j, f, te, na: (j, 0)),
            scratch_shapes=[pltpu.VMEM((tr, d), f32)],
        ),
        compiler_params=_cparams("arbitrary", "arbitrary"),
        name="moe_experts",
    )(plan["tile_e"], plan["n_act"], xs, gs, wg, wu, wd)


def _combine_kernel(w0_ref, off_ref, x_ref, tok_ref, g_ref, *refs):
    ys_refs, o_ref = refs[:-1], refs[-1]
    b = pl.program_id(0)
    tb = x_ref.shape[0]
    tok = tok_ref[...]
    e1, e2 = tok[:, R_E1:R_E1 + 1], tok[:, R_E2:R_E2 + 1]
    p1 = tok[:, R_C1:R_C1 + 1] + _offset_of(e1, off_ref)
    p2 = tok[:, R_C2:R_C2 + 1] + _offset_of(e2, off_ref)
    nw = len(ys_refs) // N_EXPERTS
    lane = lax.broadcasted_iota(jnp.int32, (tb, nw * COMBINE_WIN), 1).astype(f32)
    acc = jnp.zeros(x_ref.shape, f32)
    for e in range(N_EXPERTS):
        base = (w0_ref[b * N_EXPERTS + e] * COMBINE_WIN).astype(f32)
        hit1 = jnp.where(e1 == float(e), p1 - base, -1.0) == lane
        hit2 = jnp.where(e2 == float(e), p2 - base, -1.0) == lane
        ys = jnp.concatenate([ys_refs[nw * e + s][...] for s in range(nw)], axis=0)
        acc = acc + _dot(jnp.where(hit1 | hit2, 1.0, 0.0).astype(bf16), ys)
    o_ref[...] = x_ref[...] + g_ref[0] * acc


def _combine(x, tok, gate, ys, plan, l):
    t, d = x.shape
    tb = ROUTE_BLOCK
    nrb = ys.shape[0] // COMBINE_WIN
    ne = N_EXPERTS
    nw = tb // COMBINE_WIN + 1

    def win(e, s):
        def index(i, w0, off):
            return (jnp.minimum(w0[i * ne + e] + s, nrb - 1), 0)
        return pl.BlockSpec((COMBINE_WIN, d), index)

    ys_specs = [win(e, s) for e in range(ne) for s in range(nw)]
    return pl.pallas_call(
        _combine_kernel,
        out_shape=SDS((t, d), f32),
        grid_spec=pltpu.PrefetchScalarGridSpec(
            num_scalar_prefetch=2,
            grid=(t // tb,),
            in_specs=[pl.BlockSpec((tb, d), lambda i, *_: (i, 0)), pl.BlockSpec((tb, LANES), lambda i, *_: (i, 0)),
                      pl.BlockSpec((1, 1, d), lambda i, *_: (i // (l // tb), 0, 0))] + ys_specs,
            out_specs=pl.BlockSpec((tb, d), lambda i, *_: (i, 0)),
        ),
        compiler_params=_cparams("arbitrary"),
        name="moe_combine",
    )(plan["w0"], plan["off"], x, tok, gate, *([ys] * (nw * ne)))


def _mod_params(c, c_ctx, w, b):
    nb = c.shape[0]
    cond = jnp.concatenate([c, c_ctx[None], jnp.zeros((8 - nb - 1, c.shape[1]), f32)], axis=0)
    m = _ada(cond, w, b)
    return [p[:, None, :] for p in jnp.split(m, N_MOD, axis=-1)], nb


def _even_tokens(s, a1, sh1, g1, a2, sh2, g2, w_in, hy_short, filt, hy_skip, w_out, wg, wu, wd, tm, tp, long):
    hw = hy_skip.shape[0]
    ph, pq = _inproj(s, a1, sh1, w_in, [(0, 3 * hw), (3 * hw, w_in.shape[1])], tp)
    u, x0 = _conv_call(functools.partial(_hyprep_kernel, width=hw), ph, hy_short, [hw, hw], tp, "hyena_prep")
    hfb, l1 = filt
    if long:
        y_h = _hyena_long(u, x0, hfb, l1, hy_skip)
        y_f = _fnet_long(pq)
    else:
        y_h = _hyena_short(u, x0, hfb, l1, hy_skip)
        y_f = _fnet_short(pq)
    s = _outproj(s, y_h, y_f, w_out, g1, tm)
    return _ffn(s, a2, sh2, g2, wg, wu, wd, tp, wg.shape[1] // 2)


def _rope_tables(l):
    quarter = HEAD_DIM // 4
    n_rows = l // GRID_W
    inv_freq = ROPE_BASE ** (-jnp.arange(quarter, dtype=f32) / quarter)
    ur = jnp.arange(n_rows, dtype=jnp.int32).astype(f32)[:, None] * inv_freq[None, :]
    uc = jnp.arange(GRID_W, dtype=jnp.int32).astype(f32)[:, None] * inv_freq[None, :]
    cr, sr, cc, sc = jnp.cos(ur), jnp.sin(ur), jnp.cos(uc), jnp.sin(uc)
    by_row = lambda x: jnp.repeat(x, GRID_W, axis=0)
    by_col = lambda x: jnp.tile(x, (n_rows, 1))
    cr, sr, cc, sc = by_row(cr), by_row(sr), by_col(cc), by_col(sc)
    zero = jnp.zeros_like(cr)
    head = lambda x, y: jnp.concatenate([x, y], axis=1)
    cos = head(head(cr, cr), head(cc, cc))
    sin_a = head(head(-sr, zero), head(-sc, zero))
    sin_b = head(head(zero, sr), head(zero, sc))
    two = lambda x: jnp.concatenate([x, x], axis=1)
    return two(cos), two(sin_a), two(sin_b)


def kernel(x, c, ctx, c_ctx, e_ada_w, e_ada_b, e_norm1, e_norm2, e_w_in, e_hy_short, e_hy_w1, e_hy_b1, e_hy_freq, e_hy_w2, e_hy_b2, e_hy_w3, e_hy_skip, e_w_out, e_ffn_wg, e_ffn_wu, e_ffn_wd, o_ada_w, o_ada_b, o_norm1, o_norm2, o_w_in, o_q_norm, o_k_norm, o_sink, o_sc_conv, o_w_out, o_router_w, o_router_b, o_moe_wg, o_moe_wu, o_moe_wd):
    b, l, d = x.shape
    n_ctx = ctx.shape[1]
    tm = 1024
    tp = 512

    (sh1, sc1, g1, sh2, sc2, g2), nb = _mod_params(c, c_ctx, e_ada_w[0], e_ada_b[0])
    a1 = e_norm1[0] * (1.0 + sc1)
    a2 = e_norm2[0] * (1.0 + sc2)
    hw = e_hy_skip.shape[1]
    fw = d - hw
    cg, sg = _cs(_outer(GROUP_DIM, GROUP_DIM), GROUP_DIM)
    eye = jnp.eye(fw // GROUP_DIM, dtype=f32)
    chan = jnp.concatenate([jnp.kron(eye, cg), jnp.kron(eye, sg)], axis=1) * (1.0 / math.sqrt(GROUP_DIM))
    w_in = jnp.concatenate([e_w_in[0][:, :3 * hw], _mm32(e_w_in[0][:, 3 * hw:], chan)], axis=1).astype(bf16)
    w_out = e_w_out[0].astype(bf16)
    wg, wu, wd = e_ffn_wg[0].astype(bf16), e_ffn_wu[0].astype(bf16), e_ffn_wd[0].astype(bf16)
    filt_args = (e_hy_w1[0], e_hy_b1[0], e_hy_freq[0], e_hy_w2[0], e_hy_b2[0], e_hy_w3[0])
    lat = lambda p: p[:nb]
    cx = lambda p: jnp.broadcast_to(p[nb:nb + 1], (b,) + p.shape[1:])
    x = _even_tokens(x, lat(a1), lat(sh1), lat(g1), lat(a2), lat(sh2), lat(g2), w_in, e_hy_short[0],
                     _hyena_filter(l, *filt_args, tl=1024), e_hy_skip[0], w_out, wg, wu, wd, tm, tp, True)
    ctx = _even_tokens(ctx, cx(a1), cx(sh1), cx(g1), cx(a2), cx(sh2), cx(g2), w_in, e_hy_short[0],
                       _hyena_filter(n_ctx, *filt_args, tl=n_ctx), e_hy_skip[0], w_out, wg, wu, wd, n_ctx, n_ctx, False)

    (sh1, sc1, g1, sh2, sc2, g2), nb = _mod_params(c, c_ctx, o_ada_w[0], o_ada_b[0])
    a1 = o_norm1[0] * (1.0 + sc1)
    a2 = o_norm2[0] * (1.0 + sc2)
    w_in = o_w_in[0].astype(bf16)
    ident = (jnp.ones((n_ctx, LANES), f32), jnp.zeros((n_ctx, LANES), f32), jnp.zeros((n_ctx, LANES), f32))
    _, kc, vc, _ = _inproj_odd(ctx, cx(a1), cx(sh1), w_in, ident, o_q_norm[0], o_k_norm[0], n_ctx)
    q, k, v, s = _inproj_odd(x, lat(a1), lat(sh1), w_in, _rope_tables(l), o_q_norm[0], o_k_norm[0], tp)
    y_att = _attention(q, k, v, kc, vc, o_sink[0])
    sw = s.shape[2] // 3
    (y_sc,) = _conv_call(functools.partial(_sgconv_kernel, width=sw), s, o_sc_conv[0], [sw], 512, "sgconv")
    x = _outproj(x, y_att, y_sc, o_w_out[0].astype(bf16), lat(g1), tm)
    h, tok, rec, cb, tot = _router(x, lat(a2), lat(sh2), o_router_w[0], o_router_b[0], tm)
    plan = _route_plan(cb, tot, b * l, tm)
    xs, gs = _gather(h.reshape(b * l, d), rec, plan)
    ys = _experts(xs, gs, plan, o_moe_wg[0], o_moe_wu[0], o_moe_wd[0], 512)
    return _combine(x.reshape(b * l, d), tok, lat(g2), ys, plan, l).reshape(b, l, d)
```

```python
import functools
import math

import jax
import jax.numpy as jnp
from jax import lax
from jax.experimental import pallas as pl
from jax.experimental.pallas import tpu as pltpu

f32 = jnp.float32
bf16 = jnp.bfloat16
SDS = jax.ShapeDtypeStruct

EPS = 1e-6
N_MOD = 6
GROUP_DIM = 64
HEAD_DIM = 64
N_Q_HEADS = 8
N_KV_HEADS = 2
Q_PER_KV = N_Q_HEADS // N_KV_HEADS
WINDOW = 128
GRID_W = 64
ROPE_BASE = 10000.0
N_EXPERTS = 8
NEG_INF = -1e30
HY_EMB_DIM = 33
HY_DECAY_TARGET = 1e-2
HY_FAST_DECAY_PCT = 0.3
HY_SLOW_DECAY_PCT = 1.5

LANES = 128
DFT_N2 = 128
K1_PER_STEP = 8
VMEM_LIMIT = 56 * 1024 * 1024


def _cparams(*sem):
    return pltpu.CompilerParams(dimension_semantics=sem, vmem_limit_bytes=VMEM_LIMIT)


def _dot(a, b, **kw):
    return jnp.dot(a, b, preferred_element_type=f32, **kw)


def _full_spec(shape):
    nd = len(shape)
    return pl.BlockSpec(shape, lambda *_: (0,) * nd)


def _ada_kernel(c_ref, w_ref, b_ref, o_ref):
    c = c_ref[...]
    s = (c * jax.nn.sigmoid(c)).astype(bf16)
    o_ref[...] = _dot(s, w_ref[...].astype(bf16)) + b_ref[...]


def _ada(cond, w, b):
    d, n = w.shape
    tn = n // 4
    return pl.pallas_call(
        _ada_kernel,
        out_shape=SDS((cond.shape[0], n), f32),
        grid=(n // tn,),
        in_specs=[
            pl.BlockSpec(cond.shape, lambda j: (0, 0)),
            pl.BlockSpec((d, tn), lambda j: (0, j)),
            pl.BlockSpec((1, tn), lambda j: (0, j)),
        ],
        out_specs=pl.BlockSpec((cond.shape[0], tn), lambda j: (0, j)),
        compiler_params=_cparams("arbitrary"),
        name="ada",
    )(cond, w, b.reshape(1, n))


def _modulated(x, a, sh):
    ms = jnp.mean(x * x, axis=-1, keepdims=True)
    return (x * lax.rsqrt(ms + EPS)) * a + sh


def _inproj_kernel(x_ref, a_ref, sh_ref, w_ref, *o_refs, splits):
    h = _modulated(x_ref[0], a_ref[0], sh_ref[0]).astype(bf16)
    for o_ref, (s, e) in zip(o_refs, splits):
        o_ref[0] = _dot(h, w_ref[:, s:e]).astype(o_ref.dtype)


def _inproj(x, a, sh, w, splits, tm):
    b, l, d = x.shape
    n = w.shape[1]
    return pl.pallas_call(
        functools.partial(_inproj_kernel, splits=splits),
        out_shape=[SDS((b, l, e - s), bf16) for s, e in splits],
        grid=(b, l // tm),
        in_specs=[
            pl.BlockSpec((1, tm, d), lambda i, j: (i, j, 0)),
            pl.BlockSpec((1, 1, d), lambda i, j: (i, 0, 0)),
            pl.BlockSpec((1, 1, d), lambda i, j: (i, 0, 0)),
            pl.BlockSpec((d, n), lambda i, j: (0, 0)),
        ],
        out_specs=[pl.BlockSpec((1, tm, e - s), lambda i, j: (i, j, 0)) for s, e in splits],
        compiler_params=_cparams("parallel", "parallel"),
        name="inproj",
    )(x, a, sh, w)


def _norm_rope(t, seg, gain, cos, sin_a, sin_b):
    ms = _dot((t * t).astype(bf16), seg)
    tn = t * lax.rsqrt(ms + EPS) * gain
    w = t.shape[1]
    reps = w // cos.shape[1]
    cos = jnp.concatenate([cos] * reps, axis=1)
    sin_a = jnp.concatenate([sin_a] * reps, axis=1)
    sin_b = jnp.concatenate([sin_b] * reps, axis=1)
    quarter = HEAD_DIM // 4
    return tn * cos + pltpu.roll(tn, w - quarter, 1) * sin_a + pltpu.roll(tn, quarter, 1) * sin_b


def _token_tables(rt_ref, ct_ref):
    g = rt_ref.shape[1]
    out = []
    for k in range(rt_ref.shape[0]):
        by_row = jnp.broadcast_to(rt_ref[k][:, None, :], (g, GRID_W, LANES)).reshape(g * GRID_W, LANES)
        by_col = jnp.concatenate([ct_ref[k]] * g, axis=0)
        out.append(by_row + by_col)
    return out


def _inproj_odd_kernel(x_ref, a_ref, sh_ref, w_ref, rt_ref, ct_ref, gq_ref, gk_ref,
                       segq_ref, segk_ref, q_ref, k_ref, v_ref, s_ref, *, wq, wk):
    h = _modulated(x_ref[0], a_ref[0], sh_ref[0]).astype(bf16)
    cos, sa, sb = _token_tables(rt_ref, ct_ref)
    q = _norm_rope(_dot(h, w_ref[:, :wq]), segq_ref[...], gq_ref[...], cos, sa, sb)
    q_ref[0] = (q * HEAD_DIM ** -0.5).astype(bf16)
    k = _norm_rope(_dot(h, w_ref[:, wq:wq + wk]), segk_ref[...], gk_ref[...], cos, sa, sb)
    k_ref[0] = k.astype(bf16)
    v_ref[0] = _dot(h, w_ref[:, wq + wk:wq + 2 * wk]).astype(bf16)
    s_ref[0] = _dot(h, w_ref[:, wq + 2 * wk:]).astype(bf16)


def _inproj_odd(x, a, sh, w, tables, gq, gk, tm):
    b, l, d = x.shape
    n = w.shape[1]
    wq, wk = N_Q_HEADS * HEAD_DIM, N_KV_HEADS * HEAD_DIM
    ws = n - wq - 2 * wk
    rt, ct = tables
    g = tm // GRID_W
    seg = lambda width: jnp.kron(jnp.eye(width // HEAD_DIM, dtype=f32),
                                 jnp.full((HEAD_DIM, HEAD_DIM), 1.0 / HEAD_DIM, f32)).astype(bf16)
    row = lambda i, j: (i, j, 0)
    return pl.pallas_call(
        functools.partial(_inproj_odd_kernel, wq=wq, wk=wk),
        out_shape=[SDS((b, l, wq), bf16), SDS((b, l, wk), bf16), SDS((b, l, wk), bf16), SDS((b, l, ws), bf16)],
        grid=(b, l // tm),
        in_specs=[
            pl.BlockSpec((1, tm, d), row),
            pl.BlockSpec((1, 1, d), lambda i, j: (i, 0, 0)),
            pl.BlockSpec((1, 1, d), lambda i, j: (i, 0, 0)),
            pl.BlockSpec((d, n), lambda i, j: (0, 0)),
            pl.BlockSpec((rt.shape[0], g, LANES), lambda i, j: (0, j, 0)), _full_spec(ct.shape),
            _full_spec((1, wq)), _full_spec((1, wk)), _full_spec((wq, wq)), _full_spec((wk, wk)),
        ],
        out_specs=[pl.BlockSpec((1, tm, wq), row), pl.BlockSpec((1, tm, wk), row),
                   pl.BlockSpec((1, tm, wk), row), pl.BlockSpec((1, tm, ws), row)],
        compiler_params=_cparams("parallel", "parallel"),
        name="inproj_odd",
    )(x, a, sh, w, rt, ct, jnp.tile(gq, wq // HEAD_DIM)[None], jnp.tile(gk, wk // HEAD_DIM)[None],
      seg(wq), seg(wk))


HALO = 16


def _conv3(p, prev_row, next_row, w):
    tl = p.shape[0]
    r = lax.broadcasted_iota(jnp.int32, p.shape, 0)
    pm1 = jnp.where(r == 0, prev_row, pltpu.roll(p, 1, 0))
    pp1 = jnp.where(r == tl - 1, next_row, pltpu.roll(p, tl - 1, 0))
    return pm1 * w[0:1] + p * w[1:2] + pp1 * w[2:3]


def _halo_rows(pp_ref, pn_ref):
    i, n = pl.program_id(1), pl.num_programs(1)
    prev = pp_ref[0].astype(f32)[HALO - 1:HALO]
    nxt = pn_ref[0].astype(f32)[0:1]
    return jnp.where(i > 0, prev, 0.0), jnp.where(i < n - 1, nxt, 0.0)


def _hyprep_kernel(p_ref, pp_ref, pn_ref, w_ref, u_ref, x0_ref, *, width):
    prev, nxt = _halo_rows(pp_ref, pn_ref)
    c = _conv3(p_ref[0].astype(f32), prev, nxt, w_ref[...])
    x0_ref[0] = c[:, :width].astype(bf16)
    u_ref[0] = (c[:, 2 * width:] * c[:, width:2 * width]).astype(bf16)


def _sgconv_kernel(s_ref, sp_ref, sn_ref, w_ref, o_ref, *, width):
    def inner(t):
        return t[:, width:2 * width] * t[:, 2 * width:]
    i, n = pl.program_id(1), pl.num_programs(1)
    s = s_ref[0].astype(f32)
    prev = jnp.where(i > 0, inner(sp_ref[0].astype(f32))[HALO - 1:HALO], 0.0)
    nxt = jnp.where(i < n - 1, inner(sn_ref[0].astype(f32))[0:1], 0.0)
    o_ref[0] = (s[:, :width] * _conv3(inner(s), prev, nxt, w_ref[...])).astype(bf16)


def _conv_call(kernel, p, w, out_widths, tl, name):
    b, l, c = p.shape
    r = tl // HALO
    nh = l // HALO
    row = lambda i, j: (i, j, 0)
    return pl.pallas_call(
        kernel,
        out_shape=[SDS((b, l, ow), bf16) for ow in out_widths],
        grid=(b, l // tl),
        in_specs=[
            pl.BlockSpec((1, tl, c), row),
            pl.BlockSpec((1, HALO, c), lambda i, j: (i, jnp.maximum(j * r - 1, 0), 0)),
            pl.BlockSpec((1, HALO, c), lambda i, j: (i, jnp.minimum((j + 1) * r, nh - 1), 0)),
            _full_spec(w.shape),
        ],
        out_specs=[pl.BlockSpec((1, tl, ow), row) for ow in out_widths],
        compiler_params=_cparams("parallel", "parallel"),
        name=name,
    )(p, p, p, w)


def _lmm_kernel(w_ref, x_ref, o_ref):
    o_ref[0] = _dot(w_ref[...], x_ref[0]).astype(o_ref.dtype)


def _lmm(w, x, tn, out_dtype=bf16):
    m, k = w.shape
    b, _, n = x.shape
    return pl.pallas_call(
        _lmm_kernel,
        out_shape=SDS((b, m, n), out_dtype),
        grid=(b, n // tn),
        in_specs=[_full_spec((m, k)), pl.BlockSpec((1, k, tn), lambda i, j: (i, 0, j))],
        out_specs=pl.BlockSpec((1, m, tn), lambda i, j: (i, 0, j)),
        compiler_params=_cparams("parallel", "parallel"),
        name="lmm",
    )(w, x)


def _filt_kernel(z_ref, t_ref, w1_ref, b1_ref, fr_ref, w2_ref, b2_ref, w3_ref, dl_ref, hfb_ref, l1_ref, *, width):
    i = pl.program_id(0)
    hp = lax.Precision.HIGHEST
    fr = fr_ref[...]
    h = jnp.sin(fr * (_dot(z_ref[...], w1_ref[...], precision=hp) + b1_ref[...]))
    h = jnp.sin(fr * (_dot(h, w2_ref[...], precision=hp) + b2_ref[...]))
    h = _dot(h.astype(bf16), w3_ref[...])
    decay = jnp.exp(-t_ref[...] * dl_ref[...])
    hf = h[:, :width] * decay
    hb = h[:, width:] * decay
    tl = hf.shape[0]
    row = lax.broadcasted_iota(jnp.int32, hb.shape, 0) + i * tl
    hb = jnp.where(row == 0, 0.0, hb)
    hfb_ref[0] = hf.astype(bf16)
    hfb_ref[1] = hb.astype(bf16)
    part = jnp.sum(jnp.abs(hf), axis=0, keepdims=True) + jnp.sum(jnp.abs(hb), axis=0, keepdims=True)

    @pl.when(i == 0)
    def _():
        l1_ref[...] = jnp.zeros_like(l1_ref)

    l1_ref[...] += part


def _hyena_filter(l, w1, b1, freq, w2, b2, w3, tl):
    hid = w1.shape[1]
    width = w3.shape[1] // 2
    t = jnp.linspace(0.0, 1.0, l, dtype=f32)[:, None]
    bands = (HY_EMB_DIM - 1) // 2
    ang = (2.0 * math.pi / l) * jnp.arange(l, dtype=f32)[:, None] * jnp.linspace(1e-4, bands - 1, bands, dtype=f32)[None, :]
    z = jnp.concatenate([t, jnp.cos(ang), -jnp.sin(ang)], axis=-1)
    z = jnp.pad(z, ((0, 0), (0, LANES - HY_EMB_DIM)))
    pad_h = LANES - hid
    w1p = jnp.pad(w1.astype(f32), ((0, LANES - HY_EMB_DIM), (0, pad_h)))
    b1p = jnp.pad(b1.astype(f32), (0, pad_h))[None]
    frp = jnp.pad(freq.astype(f32), (0, pad_h))[None]
    w2p = jnp.pad(w2.astype(f32), ((0, pad_h), (0, pad_h)))
    b2p = jnp.pad(b2.astype(f32), (0, pad_h))[None]
    w3p = jnp.pad(w3, ((0, pad_h), (0, 0))).astype(bf16)
    max_decay = math.log(HY_DECAY_TARGET) / HY_FAST_DECAY_PCT
    min_decay = math.log(HY_DECAY_TARGET) / HY_SLOW_DECAY_PCT
    deltas = jnp.abs(jnp.linspace(min_decay, max_decay, width, dtype=f32))[None]
    return pl.pallas_call(
        functools.partial(_filt_kernel, width=width),
        out_shape=[SDS((2, l, width), bf16), SDS((1, width), f32)],
        grid=(l // tl,),
        in_specs=[
            pl.BlockSpec((tl, LANES), lambda i: (i, 0)),
            pl.BlockSpec((tl, 1), lambda i: (i, 0)),
            _full_spec(w1p.shape), _full_spec(b1p.shape), _full_spec(frp.shape),
            _full_spec(w2p.shape), _full_spec(b2p.shape), _full_spec(w3p.shape), _full_spec(deltas.shape),
        ],
        out_specs=[pl.BlockSpec((2, tl, width), lambda i: (0, i, 0)), pl.BlockSpec((1, width), lambda i: (0, 0))],
        compiler_params=_cparams("arbitrary"),
        name="hyena_filter",
    )(z, t, w1p, b1p, frp, w2p, b2p, w3p, deltas)


def _cs(num, den):
    ang = (2.0 * math.pi / den) * (num % den).astype(f32)
    return jnp.cos(ang), jnp.sin(ang)


def _outer(n_rows, n_cols):
    return jnp.arange(n_rows, dtype=jnp.int32)[:, None] * jnp.arange(n_cols, dtype=jnp.int32)[None, :]


def _interleave_rows(a, b):
    return jnp.stack([a, b], axis=1).reshape(2 * a.shape[0], a.shape[1])


def _interleave_cols(a, b):
    return jnp.stack([a, b], axis=2).reshape(a.shape[0], 2 * a.shape[1])


def _inner_tables(n1, scale):
    c2, s2 = _cs(_outer(DFT_N2, DFT_N2), DFT_N2)
    ct, st = _cs(_outer(n1, DFT_N2), n1 * DFT_N2)
    return c2 * scale, -s2 * scale, ct[:, None, :], -st[:, None, :]


def _gmat(f2r, f2i, tr, ti):
    return f2r * tr - f2i * ti, f2r * ti + f2i * tr


def _gblock(gr, gi):
    return jnp.concatenate([jnp.concatenate([gr, -gi], axis=1), jnp.concatenate([gi, gr], axis=1)], axis=0)


N2_PER_STEP = 8


def _s1_kernel(fc_ref, fs_ref, x_ref, o_ref):
    fc, fs = fc_ref[...], fs_ref[...]
    for j in range(N2_PER_STEP):
        slab = x_ref[0, :, j, :]
        o_ref[0, :, 0, j, :] = _dot(fc, slab).astype(o_ref.dtype)
        o_ref[0, :, 1, j, :] = _dot(fs, slab).astype(o_ref.dtype)


def _s1(fc, fs, x):
    m, k = fc.shape
    b, _, n2, c = x.shape
    nb = N2_PER_STEP
    return pl.pallas_call(
        _s1_kernel,
        out_shape=SDS((b, m, 2, n2, c), bf16),
        grid=(b, n2 // nb),
        in_specs=[_full_spec((m, k)), _full_spec((m, k)), pl.BlockSpec((1, k, nb, c), lambda i, j: (i, 0, j, 0))],
        out_specs=pl.BlockSpec((1, m, 2, nb, c), lambda i, j: (i, 0, 0, j, 0)),
        compiler_params=_cparams("parallel", "parallel"),
        name="dft_outer",
    )(fc, fs, x)


def _s2f_kernel(a_ref, f2r_ref, f2i_ref, tr_ref, ti_ref, k_ref):
    f2r, f2i = f2r_ref[...], f2i_ref[...]
    h2 = DFT_N2
    for j in range(K1_PER_STEP):
        gr, gi = _gmat(f2r, f2i, tr_ref[j], ti_ref[j])
        g = _gblock(gr, gi).astype(bf16)
        hf, hb = _dot(g, a_ref[0, j]), _dot(g, a_ref[1, j])
        k_ref[j, :h2, :] = hf[:h2] + hb[:h2]
        k_ref[j, h2:, :] = hf[h2:] - hb[h2:]


def _s23_kernel(a_ref, kh_ref, f2r_ref, f2i_ref, tr_ref, ti_ref, o_ref):
    f2r, f2i = f2r_ref[...], f2i_ref[...]
    h2 = DFT_N2
    for j in range(K1_PER_STEP):
        gr, gi = _gmat(f2r, f2i, tr_ref[j], ti_ref[j])
        uh = _dot(_gblock(gr, gi).astype(bf16), a_ref[0, j])
        ur, ui = uh[:h2], uh[h2:]
        kr, ki = kh_ref[j, :h2, :], kh_ref[j, h2:, :]
        yh = jnp.concatenate([ur * kr - ui * ki, ur * ki + ui * kr], axis=0).astype(bf16)
        grt, git = gr.T, gi.T
        res = _dot(_gblock(grt, -git).astype(bf16), yh).astype(o_ref.dtype)
        o_ref[0, :, 2 * j, :] = res[:h2]
        o_ref[0, :, 2 * j + 1, :] = res[h2:]


def _s4t_kernel(f4_ref, bq_ref, u_ref, x0_ref, sc_ref, sk_ref, o_ref):
    f4, sc, sk = f4_ref[...], sc_ref[...], sk_ref[...]
    for j in range(N2_PER_STEP):
        y = _dot(f4, bq_ref[0, j])
        u = u_ref[0, :, j, :].astype(f32)
        o_ref[0, :, j, :] = (x0_ref[0, :, j, :].astype(f32) * (y * sc + u * sk)).astype(o_ref.dtype)


def _s4_kernel(f4_ref, bq_ref, u_ref, x0_ref, sc_ref, sk_ref, o_ref):
    y = _dot(f4_ref[...], bq_ref[0])
    u = u_ref[0].astype(f32)
    o_ref[0] = (x0_ref[0].astype(f32) * (y * sc_ref[...] + u * sk_ref[...])).astype(o_ref.dtype)


def _s4(f4, bq, u, x0, scale, skip, tn):
    m, k = f4.shape
    b, _, n = bq.shape
    reps = tn // scale.shape[1]
    col = lambda i, j: (i, 0, j)
    return pl.pallas_call(
        _s4_kernel,
        out_shape=SDS((b, m, n), bf16),
        grid=(b, n // tn),
        in_specs=[_full_spec((m, k)), pl.BlockSpec((1, k, tn), col), pl.BlockSpec((1, m, tn), col),
                  pl.BlockSpec((1, m, tn), col), _full_spec((1, tn)), _full_spec((1, tn))],
        out_specs=pl.BlockSpec((1, m, tn), col),
        compiler_params=_cparams("parallel", "parallel"),
        name="hyena_s4",
    )(f4, bq, u, x0, jnp.tile(scale, (1, reps)), jnp.tile(skip, (1, reps)))


def _hyena_long(u, x0, hfb, l1, skip):
    b, l, w = u.shape
    n2 = DFT_N2
    half = l // n2
    n1 = 2 * half
    kb = K1_PER_STEP
    nb = N2_PER_STEP
    c1, s1 = _cs(_outer(n1, half), n1)
    fc, fs = c1.astype(bf16), (-s1).astype(bf16)
    f2r, f2i, tr, ti = _inner_tables(n1, 1.0)
    tab_specs = [_full_spec((n2, n2)), _full_spec((n2, n2)),
                 pl.BlockSpec((kb, 1, n2), lambda i, *_: (i, 0, 0)), pl.BlockSpec((kb, 1, n2), lambda i, *_: (i, 0, 0))]

    a_f = _s1(fc, fs, hfb.reshape(2, half, n2, w)).reshape(2, n1, 2 * n2, w)
    kh = pl.pallas_call(
        _s2f_kernel,
        out_shape=SDS((n1, 2 * n2, w), f32),
        grid=(n1 // kb,),
        in_specs=[pl.BlockSpec((2, kb, 2 * n2, w), lambda i: (0, i, 0, 0))] + tab_specs,
        out_specs=pl.BlockSpec((kb, 2 * n2, w), lambda i: (i, 0, 0)),
        compiler_params=_cparams("parallel"),
        name="hyena_s2f",
    )(a_f, f2r, f2i, tr, ti)

    u4, x04 = u.reshape(b, half, n2, w), x0.reshape(b, half, n2, w)
    a_u = _s1(fc, fs, u4).reshape(b, n1, 2 * n2, w)
    bq = pl.pallas_call(
        _s23_kernel,
        out_shape=SDS((b, n2, 2 * n1, w), bf16),
        grid=(n1 // kb, b),
        in_specs=[pl.BlockSpec((1, kb, 2 * n2, w), lambda i, j: (j, i, 0, 0)),
                  pl.BlockSpec((kb, 2 * n2, w), lambda i, j: (i, 0, 0))] + tab_specs,
        out_specs=pl.BlockSpec((1, n2, 2 * kb, w), lambda i, j: (j, 0, i, 0)),
        compiler_params=_cparams("parallel", "parallel"),
        name="hyena_s23",
    )(a_u, kh, f2r, f2i, tr, ti)

    c4, s4 = _cs(_outer(half, n1), n1)
    f4 = (_interleave_cols(c4, -s4) * (1.0 / (n1 * n2))).astype(bf16)
    slab = pl.BlockSpec((1, half, nb, w), lambda i, j: (i, 0, j, 0))
    y = pl.pallas_call(
        _s4t_kernel,
        out_shape=SDS((b, half, n2, w), bf16),
        grid=(b, n2 // nb),
        in_specs=[_full_spec(f4.shape), pl.BlockSpec((1, nb, 2 * n1, w), lambda i, j: (i, j, 0, 0)), slab, slab,
                  _full_spec((1, w)), _full_spec((1, w))],
        out_specs=slab,
        compiler_params=_cparams("parallel", "parallel"),
        name="hyena_s4",
    )(f4, bq, u4, x04, 1.0 / l1, skip[None])
    return y.reshape(b, l, w)


def _cmul_kernel(uh_ref, hh_ref, o_ref, *, nf):
    ur, ui = uh_ref[0, :nf, :], uh_ref[0, nf:, :]
    kr = hh_ref[0, :nf, :] + hh_ref[1, :nf, :]
    ki = hh_ref[0, nf:, :] - hh_ref[1, nf:, :]
    o_ref[0, :nf, :] = (ur * kr - ui * ki).astype(o_ref.dtype)
    o_ref[0, nf:, :] = (ur * ki + ui * kr).astype(o_ref.dtype)


def _hyena_short(u, x0, hfb, l1, skip):
    b, l, w = u.shape
    nf = 2 * l
    c, s = _cs(_outer(nf, l), nf)
    fwd = jnp.concatenate([c, -s], axis=0).astype(bf16)
    uh = _lmm(fwd, u, tn=w, out_dtype=f32)
    hh = _lmm(fwd, hfb, tn=w, out_dtype=f32)
    yh = pl.pallas_call(
        functools.partial(_cmul_kernel, nf=nf),
        out_shape=SDS((b, 2 * nf, w), bf16),
        grid=(b,),
        in_specs=[pl.BlockSpec((1, 2 * nf, w), lambda i: (i, 0, 0)), _full_spec((2, 2 * nf, w))],
        out_specs=pl.BlockSpec((1, 2 * nf, w), lambda i: (i, 0, 0)),
        compiler_params=_cparams("parallel"),
        name="hyena_cmul",
    )(uh, hh)
    ci, si = _cs(_outer(l, nf), nf)
    inv = (jnp.concatenate([ci, -si], axis=1) * (1.0 / nf)).astype(bf16)
    return _s4(inv, yh, u, x0, 1.0 / l1, skip[None], tn=w)


def _f2_kernel(a_ref, f2r_ref, f2i_ref, tr_ref, ti_ref, o_ref, *, width):
    f2r, f2i = f2r_ref[...], f2i_ref[...]
    for j in range(K1_PER_STEP):
        gr, gi = _gmat(f2r, f2i, tr_ref[j], ti_ref[j])
        gcat = jnp.concatenate([gr, gi], axis=1).astype(bf16)
        s0 = a_ref[0, j, 0].astype(f32)
        s1 = a_ref[0, j, 1].astype(f32)
        x2 = jnp.concatenate([s0[:, :width] - s1[:, width:], s0[:, width:] + s1[:, :width]], axis=0).astype(bf16)
        o_ref[0, :, j, :] = _dot(gcat, x2).astype(o_ref.dtype)


def _fnet_long(pq):
    b, l, w2 = pq.shape
    w = w2 // 2
    n2 = DFT_N2
    n1 = l // n2
    kb = K1_PER_STEP
    c1, s1 = _cs(_outer(n1, n1), n1)
    a = _s1(c1.astype(bf16), s1.astype(bf16), pq.reshape(b, n1, n2, w2))
    f2r, f2i, tr, ti = _inner_tables(n1, 1.0 / math.sqrt(l))
    out = pl.pallas_call(
        functools.partial(_f2_kernel, width=w),
        out_shape=SDS((b, n2, n1, w), bf16),
        grid=(b, n1 // kb),
        in_specs=[pl.BlockSpec((1, kb, 2, n2, w2), lambda i, j: (i, j, 0, 0, 0)),
                  _full_spec((n2, n2)), _full_spec((n2, n2)),
                  pl.BlockSpec((kb, 1, n2), lambda i, j: (j, 0, 0)), pl.BlockSpec((kb, 1, n2), lambda i, j: (j, 0, 0))],
        out_specs=pl.BlockSpec((1, n2, kb, w), lambda i, j: (i, 0, j, 0)),
        compiler_params=_cparams("parallel", "parallel"),
        name="fnet_f2",
    )(a, f2r, f2i, tr, ti)
    return out.reshape(b, l, w)


def _fnet_short(pq):
    b, l, w2 = pq.shape
    w = w2 // 2
    c, s = _cs(_outer(l, l), l)
    m = (jnp.concatenate([c, -s], axis=1) * (1.0 / math.sqrt(l))).astype(bf16)
    x = jnp.concatenate([pq[..., :w], pq[..., w:]], axis=1)
    return _lmm(m, x, tn=w)


def _mm32_kernel(a_ref, b_ref, o_ref):
    o_ref[...] = _dot(a_ref[...], b_ref[...], precision=lax.Precision.HIGHEST)


def _mm32(a, b):
    return pl.pallas_call(
        _mm32_kernel,
        out_shape=SDS((a.shape[0], b.shape[1]), f32),
        in_specs=[_full_spec(a.shape), _full_spec(b.shape)],
        out_specs=_full_spec((a.shape[0], b.shape[1])),
        grid=(1,),
        compiler_params=_cparams("arbitrary"),
        name="mm32",
    )(a, b)


def _outproj_kernel(x_ref, ya_ref, yb_ref, wa_ref, wb_ref, g_ref, o_ref):
    acc = _dot(ya_ref[0], wa_ref[...]) + _dot(yb_ref[0], wb_ref[...])
    o_ref[0] = x_ref[0] + g_ref[0] * acc


def _outproj(x, ya, yb, w, gate, tm):
    b, l, d = x.shape
    ka, kb_ = ya.shape[2], yb.shape[2]
    row = lambda i, j: (i, j, 0)
    return pl.pallas_call(
        _outproj_kernel,
        out_shape=SDS((b, l, d), f32),
        grid=(b, l // tm),
        in_specs=[pl.BlockSpec((1, tm, d), row), pl.BlockSpec((1, tm, ka), row), pl.BlockSpec((1, tm, kb_), row),
                  _full_spec((ka, d)), _full_spec((kb_, d)), pl.BlockSpec((1, 1, d), lambda i, j: (i, 0, 0))],
        out_specs=pl.BlockSpec((1, tm, d), row),
        compiler_params=_cparams("parallel", "parallel"),
        name="outproj",
    )(x, ya, yb, w[:ka], w[ka:], gate)


def _ffn_kernel(x_ref, a_ref, sh_ref, g_ref, wg_ref, wu_ref, wd_ref, o_ref, h_ref, acc_ref):
    f = pl.program_id(2)

    @pl.when(f == 0)
    def _():
        h_ref[...] = _modulated(x_ref[0], a_ref[0], sh_ref[0]).astype(bf16)
        acc_ref[...] = jnp.zeros_like(acc_ref)

    h = h_ref[...]
    g = _dot(h, wg_ref[...])
    u = _dot(h, wu_ref[...])
    acc_ref[...] += _dot((g * jax.nn.sigmoid(g) * u).astype(bf16), wd_ref[...])

    @pl.when(f == pl.num_programs(2) - 1)
    def _():
        o_ref[0] = x_ref[0] + g_ref[0] * acc_ref[...]


def _ffn(x, a, sh, gate, wg, wu, wd, tm, tf):
    b, l, d = x.shape
    ff = wg.shape[1]
    row = lambda i, j, f: (i, j, 0)
    vec = lambda i, j, f: (i, 0, 0)
    return pl.pallas_call(
        _ffn_kernel,
        out_shape=SDS((b, l, d), f32),
        grid=(b, l // tm, ff // tf),
        in_specs=[pl.BlockSpec((1, tm, d), row), pl.BlockSpec((1, 1, d), vec), pl.BlockSpec((1, 1, d), vec),
                  pl.BlockSpec((1, 1, d), vec),
                  pl.BlockSpec((d, tf), lambda i, j, f: (0, f)), pl.BlockSpec((d, tf), lambda i, j, f: (0, f)),
                  pl.BlockSpec((tf, d), lambda i, j, f: (f, 0))],
        out_specs=pl.BlockSpec((1, tm, d), row),
        scratch_shapes=[pltpu.VMEM((tm, d), bf16), pltpu.VMEM((tm, d), f32)],
        compiler_params=_cparams("parallel", "parallel", "arbitrary"),
        name="ffn",
    )(x, a, sh, gate, wg, wu, wd)


def _attn_kernel(sink_ref, q_ref, kp_ref, kc_ref, kn_ref, vp_ref, vc_ref, vn_ref, ck_ref, cv_ref, o_ref):
    i, n = pl.program_id(1), pl.num_programs(1)
    tq, halo = q_ref.shape[1], kp_ref.shape[1]
    n_ctx = ck_ref.shape[1]
    n_loc = tq + 2 * halo
    r = lax.broadcasted_iota(jnp.int32, (tq, n_loc + n_ctx), 0)
    c = lax.broadcasted_iota(jnp.int32, (tq, n_loc + n_ctx), 1)
    lo = jnp.where(i > 0, 0, halo)
    hi = jnp.where(i < n - 1, n_loc, halo + tq)
    mask = ((c >= r) & (c <= r + 2 * WINDOW) & (c >= lo) & (c < hi)) | (c >= n_loc)
    keys = jnp.concatenate([kp_ref[0], kc_ref[0], kn_ref[0], ck_ref[0]], axis=0)
    vals = jnp.concatenate([vp_ref[0], vc_ref[0], vn_ref[0], cv_ref[0]], axis=0)
    q = q_ref[0]
    outs = []
    for hq in range(N_Q_HEADS):
        hk = hq // Q_PER_KV
        qh = q[:, hq * HEAD_DIM:(hq + 1) * HEAD_DIM]
        kh = keys[:, hk * HEAD_DIM:(hk + 1) * HEAD_DIM]
        vh = vals[:, hk * HEAD_DIM:(hk + 1) * HEAD_DIM]
        s = lax.dot_general(qh, kh, (((1,), (1,)), ((), ())), preferred_element_type=f32)
        s = jnp.where(mask, s, NEG_INF)
        sink = sink_ref[hq]
        m = jnp.maximum(jnp.max(s, axis=-1, keepdims=True), sink)
        p = jnp.exp(s - m)
        denom = jnp.sum(p, axis=-1, keepdims=True) + jnp.exp(sink - m)
        outs.append(_dot(p.astype(bf16), vh) / denom)
    o_ref[0] = jnp.concatenate(outs, axis=1).astype(o_ref.dtype)


def _attention(q, k, v, kc, vc, sink):
    b, l, wq = q.shape
    wk = k.shape[2]
    n_ctx = kc.shape[1]
    halo = WINDOW
    reps = 2
    tq = reps * halo
    nb = l // tq
    nh = l // halo
    cur = lambda i, j: (i, j, 0)
    prv = lambda i, j: (i, jnp.maximum(j * reps - 1, 0), 0)
    nxt = lambda i, j: (i, jnp.minimum((j + 1) * reps, nh - 1), 0)
    kv = lambda f: pl.BlockSpec((1, tq if f is cur else halo, wk), f)
    ctx = pl.BlockSpec((1, n_ctx, wk), lambda i, j: (i, 0, 0))
    return pl.pallas_call(
        _attn_kernel,
        out_shape=SDS((b, l, wq), bf16),
        grid=(b, nb),
        in_specs=[pl.BlockSpec(memory_space=pltpu.SMEM), pl.BlockSpec((1, tq, wq), cur),
                  kv(prv), kv(cur), kv(nxt), kv(prv), kv(cur), kv(nxt), ctx, ctx],
        out_specs=pl.BlockSpec((1, tq, wq), cur),
        compiler_params=_cparams("parallel", "parallel"),
        name="attention",
    )(sink.astype(f32), q, k, k, k, v, v, v, kc, vc)


COMBINE_WIN = 256
GATHER_SLOTS = 8
ROUTE_BLOCK = 256
EXPERT_ROWS = 1024
R_E1, R_E2, R_C1, R_C2, R_G1, R_G2 = range(6)


def _router_kernel(x_ref, a_ref, sh_ref, rw_ref, rb_ref, h_ref, tok_ref, rec_ref, cb_ref, tot_ref, carry_ref):
    @pl.when((pl.program_id(0) == 0) & (pl.program_id(1) == 0))
    def _():
        carry_ref[...] = jnp.zeros_like(carry_ref)

    h = _modulated(x_ref[0], a_ref[0], sh_ref[0])
    h_ref[0] = h.astype(bf16)
    h_hi = h.astype(bf16)
    h_lo = (h - h_hi.astype(f32)).astype(bf16)
    logits = _dot(h_hi, rw_ref[0]) + _dot(h_lo, rw_ref[0]) + _dot(h_hi, rw_ref[1]) + rb_ref[...]
    lane = lax.broadcasted_iota(jnp.int32, logits.shape, 1).astype(f32)
    m1 = jnp.max(logits, axis=-1, keepdims=True)
    i1 = jnp.min(jnp.where(logits == m1, lane, float(LANES)), axis=-1, keepdims=True)
    rest = jnp.where(lane == i1, NEG_INF, logits)
    m2 = jnp.max(rest, axis=-1, keepdims=True)
    i2 = jnp.min(jnp.where(rest == m2, lane, float(LANES)), axis=-1, keepdims=True)
    e2 = jnp.exp(m2 - m1)
    g1 = 1.0 / (1.0 + e2)
    sel = jnp.where((lane == i1) | (lane == i2), 1.0, 0.0)
    tm = sel.shape[0]
    tri = jnp.where(lax.broadcasted_iota(jnp.int32, (tm, tm), 0) > lax.broadcasted_iota(jnp.int32, (tm, tm), 1), 1.0, 0.0)
    cnt = _dot(tri.astype(bf16), sel.astype(bf16)) + carry_ref[...]
    c1 = jnp.sum(jnp.where(lane == i1, cnt, 0.0), axis=-1, keepdims=True)
    c2 = jnp.sum(jnp.where(lane == i2, cnt, 0.0), axis=-1, keepdims=True)
    rec = jnp.zeros_like(logits)
    for k, v in ((R_E1, i1), (R_E2, i2), (R_C1, c1), (R_C2, c2), (R_G1, g1), (R_G2, e2 * g1)):
        rec = rec + jnp.where(lane == float(k), v, 0.0)
    tok_ref[...] = rec
    rec_ref[...] = rec.T[:rec_ref.shape[0]]
    cb_ref[0] = jnp.zeros(cb_ref.shape[1:], f32)
    for s in range(tm // ROUTE_BLOCK):
        cb_ref[0, s:s + 1, :] = cnt[s * ROUTE_BLOCK:s * ROUTE_BLOCK + 1]
    total = carry_ref[...] + jnp.sum(sel, axis=0, keepdims=True)
    carry_ref[...] = total
    tot_ref[...] = total


def _router(x, a, sh, rw, rb, tm):
    b, l, d = x.shape
    e = rw.shape[1]
    nl = l // tm
    rwp = jnp.pad(rw.astype(f32), ((0, 0), (0, LANES - e)))
    rw_hi = rwp.astype(bf16)
    rwp = jnp.stack([rw_hi, (rwp - rw_hi.astype(f32)).astype(bf16)])
    rbp = jnp.pad(rb.astype(f32), (0, LANES - e), constant_values=NEG_INF)[None]
    row = lambda i, j: (i, j, 0)
    vec = lambda i, j: (i, 0, 0)
    return pl.pallas_call(
        _router_kernel,
        out_shape=[SDS((b, l, d), bf16), SDS((b * l, LANES), f32), SDS((8, b * l), f32),
                   SDS((b * nl, 8, LANES), f32), SDS((1, LANES), f32)],
        grid=(b, nl),
        in_specs=[pl.BlockSpec((1, tm, d), row), pl.BlockSpec((1, 1, d), vec), pl.BlockSpec((1, 1, d), vec),
                  _full_spec(rwp.shape), _full_spec(rbp.shape)],
        out_specs=[pl.BlockSpec((1, tm, d), row), pl.BlockSpec((tm, LANES), lambda i, j: (i * nl + j, 0)),
                   pl.BlockSpec((8, tm), lambda i, j: (0, i * nl + j)),
                   pl.BlockSpec((1, 8, LANES), lambda i, j: (i * nl + j, 0, 0)), _full_spec((1, LANES))],
        scratch_shapes=[pltpu.VMEM((1, LANES), f32)],
        compiler_params=_cparams("arbitrary", "arbitrary"),
        name="router",
    )(x, a, sh, rwp, rbp)


def _route_plan(cb, tot, n_tok, tm):
    ne, tg, tr = N_EXPERTS, ROUTE_BLOCK, EXPERT_ROWS
    i32 = jnp.int32
    cbl = cb[:, :tm // tg, :ne].reshape(-1, ne).astype(i32)
    cnt = tot[0, :ne].astype(i32)
    cbx = jnp.concatenate([cbl, cnt[None]], axis=0)
    padded = ((cnt + tr - 1) // tr) * tr
    ends = jnp.cumsum(padded)
    off = ends - padded
    n_tiles = 2 * n_tok // tr + ne
    n_act = (ends[-1] // tr).astype(i32)[None]
    owner = lambda row0: jnp.minimum(jnp.sum(ends[None, :] <= row0[:, None], axis=1), ne - 1).astype(i32)
    tile_e = owner(jnp.arange(n_tiles, dtype=i32) * tr)
    g0 = jnp.arange(n_tiles * (tr // tg), dtype=i32) * tg
    ge = owner(g0)
    rho0 = g0 - off[ge]
    cbe = cbx[:, ge]
    blo = jnp.sum(cbe[1:] <= rho0[None], axis=0)
    bhi = jnp.sum(cbe[:-1] < (rho0 + tg)[None], axis=0) - 1
    valid = (g0 < ends[-1]) & (rho0 < cnt[ge])
    nb = jnp.where(valid, jnp.maximum(bhi - blo + 1, 0), 0).astype(i32)
    blo = jnp.where(valid, blo, 0).astype(i32)
    w0 = ((off[None, :] + cbl) // COMBINE_WIN).astype(i32).reshape(-1)
    return dict(off=off.astype(i32), n_act=n_act, tile_e=tile_e, blo=blo, nb=nb, w0=w0, n_rows=n_tiles * tr)


def _offset_of(idx, off_ref):
    o = jnp.zeros_like(idx)
    for e in range(N_EXPERTS):
        o = o + jnp.where(idx == float(e), off_ref[e].astype(f32), 0.0)
    return o


def _gather_kernel(blo_ref, nb_ref, off_ref, rec_ref, h_hbm, xs_ref, gs_ref, hbuf, sem, acc_ref, gacc_ref):
    j, nj = pl.program_id(0), pl.num_programs(0)
    tg, tb = xs_ref.shape[0], hbuf.shape[2]
    half = j % 2
    nb, b0 = nb_ref[j], blo_ref[j]
    rows = (lax.broadcasted_iota(jnp.int32, (tg, 1), 0) + j * tg).astype(f32)
    acc_ref[...] = jnp.zeros_like(acc_ref)
    gacc_ref[...] = jnp.zeros_like(gacc_ref)

    def copy(first, k, hf, slot):
        return pltpu.make_async_copy(h_hbm.at[pl.ds((first + k) * tb, tb)], hbuf.at[hf, slot], sem.at[hf, slot])

    def prefetch(tile, hf):
        first, count = blo_ref[tile], nb_ref[tile]
        for s in range(GATHER_SLOTS):
            @pl.when(s < count)
            def _():
                copy(first, s, hf, s).start()

    @pl.when(j == 0)
    def _():
        prefetch(0, 0)

    @pl.when(j + 1 < nj)
    def _():
        prefetch(j + 1, 1 - half)

    def body(k, carry):
        slot = k % GATHER_SLOTS

        @pl.when(k >= GATHER_SLOTS)
        def _():
            copy(b0, k, half, slot).start()

        copy(b0, k, half, slot).wait()
        rec = rec_ref[:, pl.ds(pl.multiple_of((b0 + k) * tb, tb), tb)]
        field = lambda r: rec[r:r + 1]
        m1 = (field(R_C1) + _offset_of(field(R_E1), off_ref)) == rows
        m2 = (field(R_C2) + _offset_of(field(R_E2), off_ref)) == rows
        acc_ref[...] += _dot(jnp.where(m1 | m2, 1.0, 0.0).astype(bf16), hbuf[half, slot])
        gacc_ref[...] += jnp.sum(jnp.where(m1, field(R_G1), 0.0) + jnp.where(m2, field(R_G2), 0.0), axis=-1, keepdims=True)
        return carry

    lax.fori_loop(0, nb, body, 0)
    xs_ref[...] = acc_ref[...].astype(xs_ref.dtype)
    gs_ref[...] = gacc_ref[...]


def _gather(h, rec, plan):
    t, d = h.shape
    tg = ROUTE_BLOCK
    n_rows = plan["n_rows"]
    return pl.pallas_call(
        _gather_kernel,
        out_shape=[SDS((n_rows, d), bf16), SDS((n_rows, 1), f32)],
        grid_spec=pltpu.PrefetchScalarGridSpec(
            num_scalar_prefetch=3,
            grid=(n_rows // tg,),
            in_specs=[pl.BlockSpec(rec.shape, lambda j, *_: (0, 0)), pl.BlockSpec(memory_space=pl.ANY)],
            out_specs=[pl.BlockSpec((tg, d), lambda j, *_: (j, 0)), pl.BlockSpec((tg, 1), lambda j, *_: (j, 0))],
            scratch_shapes=[pltpu.VMEM((2, GATHER_SLOTS, tg, d), bf16), pltpu.SemaphoreType.DMA((2, GATHER_SLOTS)),
                            pltpu.VMEM((tg, d), f32), pltpu.VMEM((tg, 1), f32)],
        ),
        compiler_params=_cparams("arbitrary"),
        name="moe_gather",
    )(plan["blo"], plan["nb"], plan["off"], rec, h)


def _expert_kernel(te_ref, na_ref, xs_ref, gs_ref, wg_ref, wu_ref, wd_ref, o_ref, acc_ref):
    j, f = pl.program_id(0), pl.program_id(1)
    last = f == pl.num_programs(1) - 1
    active = j < na_ref[0]

    @pl.when(active)
    def _():
        @pl.when(f == 0)
        def _():
            acc_ref[...] = jnp.zeros_like(acc_ref)

        xs = xs_ref[...]
        g = _dot(xs, wg_ref[0].astype(bf16))
        u = _dot(xs, wu_ref[0].astype(bf16))
        acc_ref[...] += _dot((g * jax.nn.sigmoid(g) * u).astype(bf16), wd_ref[0].astype(bf16))

        @pl.when(last)
        def _():
            o_ref[...] = (acc_ref[...] * gs_ref[...]).astype(o_ref.dtype)

    @pl.when(jnp.logical_not(active) & last)
    def _():
        o_ref[...] = jnp.zeros_like(o_ref)


def _experts(xs, gs, plan, wg, wu, wd, tf):
    n_rows, d = xs.shape
    ff = wg.shape[2]
    tr = EXPERT_ROWS
    nf = ff // tf

    def live(j, na):
        return jnp.maximum(jnp.minimum(j, na[0] - 1), 0)

    def rows(j, f, te, na):
        return (live(j, na), 0)

    def w_in(j, f, te, na):
        return (te[live(j, na)], 0, jnp.where(j < na[0], f, nf - 1))

    def w_out(j, f, te, na):
        return (te[live(j, na)], jnp.where(j < na[0], f, nf - 1), 0)

    return pl.pallas_call(
        _expert_kernel,
        out_shape=SDS((n_rows, d), bf16),
        grid_spec=pltpu.PrefetchScalarGridSpec(
            num_scalar_prefetch=2,
            grid=(n_rows // tr, nf),
            in_specs=[pl.BlockSpec((tr, d), rows), pl.BlockSpec((tr, 1), rows),
                      pl.BlockSpec((1, d, tf), w_in), pl.BlockSpec((1, d, tf), w_in), pl.BlockSpec((1, tf, d), w_out)],
            out_specs=pl.BlockSpec((tr, d), lambda j, f, te, na: (j, 0)),
            scratch_shapes=[pltpu.VMEM((tr, d), f32)],
        ),
        compiler_params=_cparams("arbitrary", "arbitrary"),
        name="moe_experts",
    )(plan["tile_e"], plan["n_act"], xs, gs, wg, wu, wd)


def _combine_kernel(w0_ref, off_ref, x_ref, tok_ref, g_ref, *refs):
    ys_refs, o_ref = refs[:-1], refs[-1]
    b = pl.program_id(0)
    tb = x_ref.shape[0]
    tok = tok_ref[...]
    e1, e2 = tok[:, R_E1:R_E1 + 1], tok[:, R_E2:R_E2 + 1]
    p1 = tok[:, R_C1:R_C1 + 1] + _offset_of(e1, off_ref)
    p2 = tok[:, R_C2:R_C2 + 1] + _offset_of(e2, off_ref)
    nw = len(ys_refs) // N_EXPERTS
    lane = lax.broadcasted_iota(jnp.int32, (tb, nw * COMBINE_WIN), 1).astype(f32)
    acc = jnp.zeros(x_ref.shape, f32)
    for e in range(N_EXPERTS):
        base = (w0_ref[b * N_EXPERTS + e] * COMBINE_WIN).astype(f32)
        hit1 = jnp.where(e1 == float(e), p1 - base, -1.0) == lane
        hit2 = jnp.where(e2 == float(e), p2 - base, -1.0) == lane
        ys = jnp.concatenate([ys_refs[nw * e + s][...] for s in range(nw)], axis=0)
        acc = acc + _dot(jnp.where(hit1 | hit2, 1.0, 0.0).astype(bf16), ys)
    o_ref[...] = x_ref[...] + g_ref[0] * acc


def _combine(x, tok, gate, ys, plan, l):
    t, d = x.shape
    tb = ROUTE_BLOCK
    nrb = ys.shape[0] // COMBINE_WIN
    ne = N_EXPERTS
    nw = tb // COMBINE_WIN + 1

    def win(e, s):
        def index(i, w0, off):
            return (jnp.minimum(w0[i * ne + e] + s, nrb - 1), 0)
        return pl.BlockSpec((COMBINE_WIN, d), index)

    ys_specs = [win(e, s) for e in range(ne) for s in range(nw)]
    return pl.pallas_call(
        _combine_kernel,
        out_shape=SDS((t, d), f32),
        grid_spec=pltpu.PrefetchScalarGridSpec(
            num_scalar_prefetch=2,
            grid=(t // tb,),
            in_specs=[pl.BlockSpec((tb, d), lambda i, *_: (i, 0)), pl.BlockSpec((tb, LANES), lambda i, *_: (i, 0)),
                      pl.BlockSpec((1, 1, d), lambda i, *_: (i // (l // tb), 0, 0))] + ys_specs,
            out_specs=pl.BlockSpec((tb, d), lambda i, *_: (i, 0)),
        ),
        compiler_params=_cparams("arbitrary"),
        name="moe_combine",
    )(plan["w0"], plan["off"], x, tok, gate, *([ys] * (nw * ne)))


def _mod_params(c, c_ctx, w, b):
    nb = c.shape[0]
    cond = jnp.concatenate([c, c_ctx[None], jnp.zeros((8 - nb - 1, c.shape[1]), f32)], axis=0)
    m = _ada(cond, w, b)
    return [p[:, None, :] for p in jnp.split(m, N_MOD, axis=-1)], nb


def _even_tokens(s, a1, sh1, g1, a2, sh2, g2, w_in, hy_short, filt, hy_skip, w_out, wg, wu, wd, tm, tp, long):
    hw = hy_skip.shape[0]
    ph, pq = _inproj(s, a1, sh1, w_in, [(0, 3 * hw), (3 * hw, w_in.shape[1])], tp)
    u, x0 = _conv_call(functools.partial(_hyprep_kernel, width=hw), ph, hy_short, [hw, hw], tp, "hyena_prep")
    hfb, l1 = filt
    if long:
        y_h = _hyena_long(u, x0, hfb, l1, hy_skip)
        y_f = _fnet_long(pq)
    else:
        y_h = _hyena_short(u, x0, hfb, l1, hy_skip)
        y_f = _fnet_short(pq)
    s = _outproj(s, y_h, y_f, w_out, g1, tm)
    return _ffn(s, a2, sh2, g2, wg, wu, wd, tp, wg.shape[1] // 2)


def _rope_tables(l):
    quarter = HEAD_DIM // 4
    n_rows = l // GRID_W
    inv_freq = ROPE_BASE ** (-jnp.arange(quarter, dtype=f32) / quarter)
    ur = jnp.arange(n_rows, dtype=jnp.int32).astype(f32)[:, None] * inv_freq[None, :]
    uc = jnp.arange(GRID_W, dtype=jnp.int32).astype(f32)[:, None] * inv_freq[None, :]

    def lanes(first, second, own):
        z = jnp.zeros_like(first)
        half, other = jnp.concatenate([first, second], axis=1), jnp.concatenate([z, z], axis=1)
        head = jnp.concatenate([half, other] if own == "row" else [other, half], axis=1)
        return jnp.concatenate([head, head], axis=1)

    def tables(ang, own):
        c, s, z = jnp.cos(ang), jnp.sin(ang), jnp.zeros_like(ang)
        return jnp.stack([lanes(c, c, own), lanes(-s, z, own), lanes(z, s, own)])

    return tables(ur, "row"), tables(uc, "col")


def kernel(x, c, ctx, c_ctx, e_ada_w, e_ada_b, e_norm1, e_norm2, e_w_in, e_hy_short, e_hy_w1, e_hy_b1, e_hy_freq, e_hy_w2, e_hy_b2, e_hy_w3, e_hy_skip, e_w_out, e_ffn_wg, e_ffn_wu, e_ffn_wd, o_ada_w, o_ada_b, o_norm1, o_norm2, o_w_in, o_q_norm, o_k_norm, o_sink, o_sc_conv, o_w_out, o_router_w, o_router_b, o_moe_wg, o_moe_wu, o_moe_wd):
    b, l, d = x.shape
    n_ctx = ctx.shape[1]
    tm = 1024
    tp = 512

    (sh1, sc1, g1, sh2, sc2, g2), nb = _mod_params(c, c_ctx, e_ada_w[0], e_ada_b[0])
    a1 = e_norm1[0] * (1.0 + sc1)
    a2 = e_norm2[0] * (1.0 + sc2)
    hw = e_hy_skip.shape[1]
    fw = d - hw
    cg, sg = _cs(_outer(GROUP_DIM, GROUP_DIM), GROUP_DIM)
    eye = jnp.eye(fw // GROUP_DIM, dtype=f32)
    chan = jnp.concatenate([jnp.kron(eye, cg), jnp.kron(eye, sg)], axis=1) * (1.0 / math.sqrt(GROUP_DIM))
    w_in = jnp.concatenate([e_w_in[0][:, :3 * hw], _mm32(e_w_in[0][:, 3 * hw:], chan)], axis=1).astype(bf16)
    w_out = e_w_out[0].astype(bf16)
    wg, wu, wd = e_ffn_wg[0].astype(bf16), e_ffn_wu[0].astype(bf16), e_ffn_wd[0].astype(bf16)
    filt_args = (e_hy_w1[0], e_hy_b1[0], e_hy_freq[0], e_hy_w2[0], e_hy_b2[0], e_hy_w3[0])
    lat = lambda p: p[:nb]
    cx = lambda p: jnp.broadcast_to(p[nb:nb + 1], (b,) + p.shape[1:])
    x = _even_tokens(x, lat(a1), lat(sh1), lat(g1), lat(a2), lat(sh2), lat(g2), w_in, e_hy_short[0],
                     _hyena_filter(l, *filt_args, tl=1024), e_hy_skip[0], w_out, wg, wu, wd, tm, tp, True)
    ctx = _even_tokens(ctx, cx(a1), cx(sh1), cx(g1), cx(a2), cx(sh2), cx(g2), w_in, e_hy_short[0],
                       _hyena_filter(n_ctx, *filt_args, tl=n_ctx), e_hy_skip[0], w_out, wg, wu, wd, n_ctx, n_ctx, False)

    (sh1, sc1, g1, sh2, sc2, g2), nb = _mod_params(c, c_ctx, o_ada_w[0], o_ada_b[0])
    a1 = o_norm1[0] * (1.0 + sc1)
    a2 = o_norm2[0] * (1.0 + sc2)
    w_in = o_w_in[0].astype(bf16)
    unit = lambda rows, c: jnp.stack([jnp.full((rows, LANES), c, f32)] + [jnp.zeros((rows, LANES), f32)] * 2)
    ident = (unit(n_ctx // GRID_W, 1.0), unit(GRID_W, 0.0))
    _, kc, vc, _ = _inproj_odd(ctx, cx(a1), cx(sh1), w_in, ident, o_q_norm[0], o_k_norm[0], n_ctx)
    q, k, v, s = _inproj_odd(x, lat(a1), lat(sh1), w_in, _rope_tables(l), o_q_norm[0], o_k_norm[0], tp)
    y_att = _attention(q, k, v, kc, vc, o_sink[0])
    sw = s.shape[2] // 3
    (y_sc,) = _conv_call(functools.partial(_sgconv_kernel, width=sw), s, o_sc_conv[0], [sw], 512, "sgconv")
    x = _outproj(x, y_att, y_sc, o_w_out[0].astype(bf16), lat(g1), tm)
    h, tok, rec, cb, tot = _router(x, lat(a2), lat(sh2), o_router_w[0], o_router_b[0], tm)
    plan = _route_plan(cb, tot, b * l, tm)
    xs, gs = _gather(h.reshape(b * l, d), rec, plan)
    ys = _experts(xs, gs, plan, o_moe_wg[0], o_moe_wu[0], o_moe_wd[0], 512)
    return _combine(x.reshape(b * l, d), tok, lat(g2), ys, plan, l).reshape(b, l, d)
```

```python
import functools
import math

import jax
import jax.numpy as jnp
from jax import lax
from jax.experimental import pallas as pl
from jax.experimental.pallas import tpu as pltpu

f32 = jnp.float32
bf16 = jnp.bfloat16
SDS = jax.ShapeDtypeStruct

EPS = 1e-6
N_MOD = 6
GROUP_DIM = 64
HEAD_DIM = 64
N_Q_HEADS = 8
N_KV_HEADS = 2
Q_PER_KV = N_Q_HEADS // N_KV_HEADS
WINDOW = 128
GRID_W = 64
ROPE_BASE = 10000.0
N_EXPERTS = 8
NEG_INF = -1e30
HY_EMB_DIM = 33
HY_DECAY_TARGET = 1e-2
HY_FAST_DECAY_PCT = 0.3
HY_SLOW_DECAY_PCT = 1.5

LANES = 128
DFT_N2 = 128
K1_PER_STEP = 8
VMEM_LIMIT = 56 * 1024 * 1024


def _cparams(*sem):
    return pltpu.CompilerParams(dimension_semantics=sem, vmem_limit_bytes=VMEM_LIMIT)


def _dot(a, b, **kw):
    return jnp.dot(a, b, preferred_element_type=f32, **kw)


def _full_spec(shape):
    nd = len(shape)
    return pl.BlockSpec(shape, lambda *_: (0,) * nd)


def _ada_kernel(c_ref, w_ref, b_ref, o_ref):
    c = c_ref[...]
    s = (c * jax.nn.sigmoid(c)).astype(bf16)
    o_ref[...] = _dot(s, w_ref[...].astype(bf16)) + b_ref[...]


def _ada(cond, w, b):
    d, n = w.shape
    tn = n // 4
    return pl.pallas_call(
        _ada_kernel,
        out_shape=SDS((cond.shape[0], n), f32),
        grid=(n // tn,),
        in_specs=[
            pl.BlockSpec(cond.shape, lambda j: (0, 0)),
            pl.BlockSpec((d, tn), lambda j: (0, j)),
            pl.BlockSpec((1, tn), lambda j: (0, j)),
        ],
        out_specs=pl.BlockSpec((cond.shape[0], tn), lambda j: (0, j)),
        compiler_params=_cparams("arbitrary"),
        name="ada",
    )(cond, w, b.reshape(1, n))


def _modulated(x, a, sh):
    ms = jnp.mean(x * x, axis=-1, keepdims=True)
    return (x * lax.rsqrt(ms + EPS)) * a + sh


def _inproj_kernel(x_ref, a_ref, sh_ref, w_ref, *o_refs, splits):
    h = _modulated(x_ref[0], a_ref[0], sh_ref[0]).astype(bf16)
    for o_ref, (s, e) in zip(o_refs, splits):
        o_ref[0] = _dot(h, w_ref[:, s:e]).astype(o_ref.dtype)


def _inproj(x, a, sh, w, splits, tm):
    b, l, d = x.shape
    n = w.shape[1]
    return pl.pallas_call(
        functools.partial(_inproj_kernel, splits=splits),
        out_shape=[SDS((b, l, e - s), bf16) for s, e in splits],
        grid=(b, l // tm),
        in_specs=[
            pl.BlockSpec((1, tm, d), lambda i, j: (i, j, 0)),
            pl.BlockSpec((1, 1, d), lambda i, j: (i, 0, 0)),
            pl.BlockSpec((1, 1, d), lambda i, j: (i, 0, 0)),
            pl.BlockSpec((d, n), lambda i, j: (0, 0)),
        ],
        out_specs=[pl.BlockSpec((1, tm, e - s), lambda i, j: (i, j, 0)) for s, e in splits],
        compiler_params=_cparams("parallel", "parallel"),
        name="inproj",
    )(x, a, sh, w)


def _norm_rope(t, seg, gain, cos, sin_a, sin_b):
    ms = _dot((t * t).astype(bf16), seg)
    tn = t * lax.rsqrt(ms + EPS) * gain
    w = t.shape[1]
    reps = w // cos.shape[1]
    cos = jnp.concatenate([cos] * reps, axis=1)
    sin_a = jnp.concatenate([sin_a] * reps, axis=1)
    sin_b = jnp.concatenate([sin_b] * reps, axis=1)
    quarter = HEAD_DIM // 4
    return tn * cos + pltpu.roll(tn, w - quarter, 1) * sin_a + pltpu.roll(tn, quarter, 1) * sin_b


def _token_tables(rt_ref, ct_ref):
    g = rt_ref.shape[1]
    out = []
    for k in range(rt_ref.shape[0]):
        by_row = jnp.broadcast_to(rt_ref[k][:, None, :], (g, GRID_W, LANES)).reshape(g * GRID_W, LANES)
        by_col = jnp.concatenate([ct_ref[k]] * g, axis=0)
        out.append(by_row + by_col)
    return out


def _inproj_odd_kernel(x_ref, a_ref, sh_ref, w_ref, rt_ref, ct_ref, gq_ref, gk_ref,
                       segq_ref, segk_ref, q_ref, k_ref, v_ref, s_ref, *, wq, wk):
    h = _modulated(x_ref[0], a_ref[0], sh_ref[0]).astype(bf16)
    cos, sa, sb = _token_tables(rt_ref, ct_ref)
    q = _norm_rope(_dot(h, w_ref[:, :wq]), segq_ref[...], gq_ref[...], cos, sa, sb)
    q_ref[0] = (q * HEAD_DIM ** -0.5).astype(bf16)
    k = _norm_rope(_dot(h, w_ref[:, wq:wq + wk]), segk_ref[...], gk_ref[...], cos, sa, sb)
    k_ref[0] = k.astype(bf16)
    v_ref[0] = _dot(h, w_ref[:, wq + wk:wq + 2 * wk]).astype(bf16)
    s_ref[0] = _dot(h, w_ref[:, wq + 2 * wk:]).astype(bf16)


def _inproj_odd(x, a, sh, w, tables, gq, gk, tm):
    b, l, d = x.shape
    n = w.shape[1]
    wq, wk = N_Q_HEADS * HEAD_DIM, N_KV_HEADS * HEAD_DIM
    ws = n - wq - 2 * wk
    rt, ct = tables
    g = tm // GRID_W
    seg = lambda width: jnp.kron(jnp.eye(width // HEAD_DIM, dtype=f32),
                                 jnp.full((HEAD_DIM, HEAD_DIM), 1.0 / HEAD_DIM, f32)).astype(bf16)
    row = lambda i, j: (i, j, 0)
    return pl.pallas_call(
        functools.partial(_inproj_odd_kernel, wq=wq, wk=wk),
        out_shape=[SDS((b, l, wq), bf16), SDS((b, l, wk), bf16), SDS((b, l, wk), bf16), SDS((b, l, ws), bf16)],
        grid=(b, l // tm),
        in_specs=[
            pl.BlockSpec((1, tm, d), row),
            pl.BlockSpec((1, 1, d), lambda i, j: (i, 0, 0)),
            pl.BlockSpec((1, 1, d), lambda i, j: (i, 0, 0)),
            pl.BlockSpec((d, n), lambda i, j: (0, 0)),
            pl.BlockSpec((rt.shape[0], g, LANES), lambda i, j: (0, j, 0)), _full_spec(ct.shape),
            _full_spec((1, wq)), _full_spec((1, wk)), _full_spec((wq, wq)), _full_spec((wk, wk)),
        ],
        out_specs=[pl.BlockSpec((1, tm, wq), row), pl.BlockSpec((1, tm, wk), row),
                   pl.BlockSpec((1, tm, wk), row), pl.BlockSpec((1, tm, ws), row)],
        compiler_params=_cparams("parallel", "parallel"),
        name="inproj_odd",
    )(x, a, sh, w, rt, ct, jnp.tile(gq, wq // HEAD_DIM)[None], jnp.tile(gk, wk // HEAD_DIM)[None],
      seg(wq), seg(wk))


HALO = 16


def _conv3(p, prev_row, next_row, w):
    tl = p.shape[0]
    r = lax.broadcasted_iota(jnp.int32, p.shape, 0)
    pm1 = jnp.where(r == 0, prev_row, pltpu.roll(p, 1, 0))
    pp1 = jnp.where(r == tl - 1, next_row, pltpu.roll(p, tl - 1, 0))
    return pm1 * w[0:1] + p * w[1:2] + pp1 * w[2:3]


def _halo_rows(pp_ref, pn_ref):
    i, n = pl.program_id(1), pl.num_programs(1)
    prev = pp_ref[0].astype(f32)[HALO - 1:HALO]
    nxt = pn_ref[0].astype(f32)[0:1]
    return jnp.where(i > 0, prev, 0.0), jnp.where(i < n - 1, nxt, 0.0)


def _hyprep_kernel(p_ref, pp_ref, pn_ref, w_ref, u_ref, x0_ref, *, width):
    prev, nxt = _halo_rows(pp_ref, pn_ref)
    c = _conv3(p_ref[0].astype(f32), prev, nxt, w_ref[...])
    x0_ref[0] = c[:, :width].astype(bf16)
    u_ref[0] = (c[:, 2 * width:] * c[:, width:2 * width]).astype(bf16)


def _sgconv_kernel(s_ref, sp_ref, sn_ref, w_ref, o_ref, *, width):
    def inner(t):
        return t[:, width:2 * width] * t[:, 2 * width:]
    i, n = pl.program_id(1), pl.num_programs(1)
    s = s_ref[0].astype(f32)
    prev = jnp.where(i > 0, inner(sp_ref[0].astype(f32))[HALO - 1:HALO], 0.0)
    nxt = jnp.where(i < n - 1, inner(sn_ref[0].astype(f32))[0:1], 0.0)
    o_ref[0] = (s[:, :width] * _conv3(inner(s), prev, nxt, w_ref[...])).astype(bf16)


def _conv_call(kernel, p, w, out_widths, tl, name):
    b, l, c = p.shape
    r = tl // HALO
    nh = l // HALO
    row = lambda i, j: (i, j, 0)
    return pl.pallas_call(
        kernel,
        out_shape=[SDS((b, l, ow), bf16) for ow in out_widths],
        grid=(b, l // tl),
        in_specs=[
            pl.BlockSpec((1, tl, c), row),
            pl.BlockSpec((1, HALO, c), lambda i, j: (i, jnp.maximum(j * r - 1, 0), 0)),
            pl.BlockSpec((1, HALO, c), lambda i, j: (i, jnp.minimum((j + 1) * r, nh - 1), 0)),
            _full_spec(w.shape),
        ],
        out_specs=[pl.BlockSpec((1, tl, ow), row) for ow in out_widths],
        compiler_params=_cparams("parallel", "parallel"),
        name=name,
    )(p, p, p, w)


def _lmm_kernel(w_ref, x_ref, o_ref):
    o_ref[0] = _dot(w_ref[...], x_ref[0]).astype(o_ref.dtype)


def _lmm(w, x, tn, out_dtype=bf16):
    m, k = w.shape
    b, _, n = x.shape
    return pl.pallas_call(
        _lmm_kernel,
        out_shape=SDS((b, m, n), out_dtype),
        grid=(b, n // tn),
        in_specs=[_full_spec((m, k)), pl.BlockSpec((1, k, tn), lambda i, j: (i, 0, j))],
        out_specs=pl.BlockSpec((1, m, tn), lambda i, j: (i, 0, j)),
        compiler_params=_cparams("parallel", "parallel"),
        name="lmm",
    )(w, x)


def _filt_kernel(z_ref, t_ref, w1_ref, b1_ref, fr_ref, w2_ref, b2_ref, w3_ref, dl_ref, hfb_ref, l1_ref, *, width):
    i = pl.program_id(0)
    hp = lax.Precision.HIGHEST
    fr = fr_ref[...]
    h = jnp.sin(fr * (_dot(z_ref[...], w1_ref[...], precision=hp) + b1_ref[...]))
    h = jnp.sin(fr * (_dot(h, w2_ref[...], precision=hp) + b2_ref[...]))
    h = _dot(h.astype(bf16), w3_ref[...])
    decay = jnp.exp(-t_ref[...] * dl_ref[...])
    hf = h[:, :width] * decay
    hb = h[:, width:] * decay
    tl = hf.shape[0]
    row = lax.broadcasted_iota(jnp.int32, hb.shape, 0) + i * tl
    hb = jnp.where(row == 0, 0.0, hb)
    hfb_ref[0] = hf.astype(bf16)
    hfb_ref[1] = hb.astype(bf16)
    part = jnp.sum(jnp.abs(hf), axis=0, keepdims=True) + jnp.sum(jnp.abs(hb), axis=0, keepdims=True)

    @pl.when(i == 0)
    def _():
        l1_ref[...] = jnp.zeros_like(l1_ref)

    l1_ref[...] += part


def _hyena_filter(l, w1, b1, freq, w2, b2, w3, tl):
    hid = w1.shape[1]
    width = w3.shape[1] // 2
    t = jnp.linspace(0.0, 1.0, l, dtype=f32)[:, None]
    bands = (HY_EMB_DIM - 1) // 2
    ang = (2.0 * math.pi / l) * jnp.arange(l, dtype=f32)[:, None] * jnp.linspace(1e-4, bands - 1, bands, dtype=f32)[None, :]
    z = jnp.concatenate([t, jnp.cos(ang), -jnp.sin(ang)], axis=-1)
    z = jnp.pad(z, ((0, 0), (0, LANES - HY_EMB_DIM)))
    pad_h = LANES - hid
    w1p = jnp.pad(w1.astype(f32), ((0, LANES - HY_EMB_DIM), (0, pad_h)))
    b1p = jnp.pad(b1.astype(f32), (0, pad_h))[None]
    frp = jnp.pad(freq.astype(f32), (0, pad_h))[None]
    w2p = jnp.pad(w2.astype(f32), ((0, pad_h), (0, pad_h)))
    b2p = jnp.pad(b2.astype(f32), (0, pad_h))[None]
    w3p = jnp.pad(w3, ((0, pad_h), (0, 0))).astype(bf16)
    max_decay = math.log(HY_DECAY_TARGET) / HY_FAST_DECAY_PCT
    min_decay = math.log(HY_DECAY_TARGET) / HY_SLOW_DECAY_PCT
    deltas = jnp.abs(jnp.linspace(min_decay, max_decay, width, dtype=f32))[None]
    return pl.pallas_call(
        functools.partial(_filt_kernel, width=width),
        out_shape=[SDS((2, l, width), bf16), SDS((1, width), f32)],
        grid=(l // tl,),
        in_specs=[
            pl.BlockSpec((tl, LANES), lambda i: (i, 0)),
            pl.BlockSpec((tl, 1), lambda i: (i, 0)),
            _full_spec(w1p.shape), _full_spec(b1p.shape), _full_spec(frp.shape),
            _full_spec(w2p.shape), _full_spec(b2p.shape), _full_spec(w3p.shape), _full_spec(deltas.shape),
        ],
        out_specs=[pl.BlockSpec((2, tl, width), lambda i: (0, i, 0)), pl.BlockSpec((1, width), lambda i: (0, 0))],
        compiler_params=_cparams("arbitrary"),
        name="hyena_filter",
    )(z, t, w1p, b1p, frp, w2p, b2p, w3p, deltas)


def _cs(num, den):
    ang = (2.0 * math.pi / den) * (num % den).astype(f32)
    return jnp.cos(ang), jnp.sin(ang)


def _outer(n_rows, n_cols):
    return jnp.arange(n_rows, dtype=jnp.int32)[:, None] * jnp.arange(n_cols, dtype=jnp.int32)[None, :]


def _interleave_rows(a, b):
    return jnp.stack([a, b], axis=1).reshape(2 * a.shape[0], a.shape[1])


def _interleave_cols(a, b):
    return jnp.stack([a, b], axis=2).reshape(a.shape[0], 2 * a.shape[1])


def _inner_tables(n1, scale):
    c2, s2 = _cs(_outer(DFT_N2, DFT_N2), DFT_N2)
    ct, st = _cs(_outer(n1, DFT_N2), n1 * DFT_N2)
    return c2 * scale, -s2 * scale, ct[:, None, :], -st[:, None, :]


def _gmat(f2r, f2i, tr, ti):
    return f2r * tr - f2i * ti, f2r * ti + f2i * tr


def _gblock(gr, gi):
    return jnp.concatenate([jnp.concatenate([gr, -gi], axis=1), jnp.concatenate([gi, gr], axis=1)], axis=0)


N2_PER_STEP = 8


def _s1_kernel(fc_ref, fs_ref, x_ref, o_ref):
    fc, fs = fc_ref[...], fs_ref[...]
    for j in range(N2_PER_STEP):
        slab = x_ref[0, :, j, :]
        o_ref[0, :, 0, j, :] = _dot(fc, slab).astype(o_ref.dtype)
        o_ref[0, :, 1, j, :] = _dot(fs, slab).astype(o_ref.dtype)


def _s1(fc, fs, x):
    m, k = fc.shape
    b, _, n2, c = x.shape
    nb = N2_PER_STEP
    return pl.pallas_call(
        _s1_kernel,
        out_shape=SDS((b, m, 2, n2, c), bf16),
        grid=(b, n2 // nb),
        in_specs=[_full_spec((m, k)), _full_spec((m, k)), pl.BlockSpec((1, k, nb, c), lambda i, j: (i, 0, j, 0))],
        out_specs=pl.BlockSpec((1, m, 2, nb, c), lambda i, j: (i, 0, 0, j, 0)),
        compiler_params=_cparams("parallel", "parallel"),
        name="dft_outer",
    )(fc, fs, x)


def _s2f_kernel(a_ref, f2r_ref, f2i_ref, tr_ref, ti_ref, k_ref):
    f2r, f2i = f2r_ref[...], f2i_ref[...]
    h2 = DFT_N2
    for j in range(K1_PER_STEP):
        gr, gi = _gmat(f2r, f2i, tr_ref[j], ti_ref[j])
        g = _gblock(gr, gi).astype(bf16)
        hf, hb = _dot(g, a_ref[0, j]), _dot(g, a_ref[1, j])
        k_ref[j, :h2, :] = (hf[:h2] + hb[:h2]).astype(k_ref.dtype)
        k_ref[j, h2:, :] = (hf[h2:] - hb[h2:]).astype(k_ref.dtype)


def _s23_kernel(a_ref, kh_ref, f2r_ref, f2i_ref, tr_ref, ti_ref, o_ref):
    f2r, f2i = f2r_ref[...], f2i_ref[...]
    h2 = DFT_N2
    for j in range(K1_PER_STEP):
        gr, gi = _gmat(f2r, f2i, tr_ref[j], ti_ref[j])
        uh = _dot(_gblock(gr, gi).astype(bf16), a_ref[0, j])
        ur, ui = uh[:h2], uh[h2:]
        kr, ki = kh_ref[j, :h2, :].astype(f32), kh_ref[j, h2:, :].astype(f32)
        yh = jnp.concatenate([ur * kr - ui * ki, ur * ki + ui * kr], axis=0).astype(bf16)
        grt, git = gr.T, gi.T
        res = _dot(_gblock(grt, -git).astype(bf16), yh).astype(o_ref.dtype)
        o_ref[0, :, 2 * j, :] = res[:h2]
        o_ref[0, :, 2 * j + 1, :] = res[h2:]


def _s4t_kernel(f4_ref, bq_ref, u_ref, x0_ref, sc_ref, sk_ref, o_ref):
    f4, sc, sk = f4_ref[...], sc_ref[...], sk_ref[...]
    for j in range(N2_PER_STEP):
        y = _dot(f4, bq_ref[0, j])
        u = u_ref[0, :, j, :].astype(f32)
        o_ref[0, :, j, :] = (x0_ref[0, :, j, :].astype(f32) * (y * sc + u * sk)).astype(o_ref.dtype)


def _s4_kernel(f4_ref, bq_ref, u_ref, x0_ref, sc_ref, sk_ref, o_ref):
    y = _dot(f4_ref[...], bq_ref[0])
    u = u_ref[0].astype(f32)
    o_ref[0] = (x0_ref[0].astype(f32) * (y * sc_ref[...] + u * sk_ref[...])).astype(o_ref.dtype)


def _s4(f4, bq, u, x0, scale, skip, tn):
    m, k = f4.shape
    b, _, n = bq.shape
    reps = tn // scale.shape[1]
    col = lambda i, j: (i, 0, j)
    return pl.pallas_call(
        _s4_kernel,
        out_shape=SDS((b, m, n), bf16),
        grid=(b, n // tn),
        in_specs=[_full_spec((m, k)), pl.BlockSpec((1, k, tn), col), pl.BlockSpec((1, m, tn), col),
                  pl.BlockSpec((1, m, tn), col), _full_spec((1, tn)), _full_spec((1, tn))],
        out_specs=pl.BlockSpec((1, m, tn), col),
        compiler_params=_cparams("parallel", "parallel"),
        name="hyena_s4",
    )(f4, bq, u, x0, jnp.tile(scale, (1, reps)), jnp.tile(skip, (1, reps)))


def _hyena_long(u, x0, hfb, l1, skip):
    b, l, w = u.shape
    n2 = DFT_N2
    half = l // n2
    n1 = 2 * half
    kb = K1_PER_STEP
    nb = N2_PER_STEP
    c1, s1 = _cs(_outer(n1, half), n1)
    fc, fs = c1.astype(bf16), (-s1).astype(bf16)
    f2r, f2i, tr, ti = _inner_tables(n1, 1.0)
    tab_specs = [_full_spec((n2, n2)), _full_spec((n2, n2)),
                 pl.BlockSpec((kb, 1, n2), lambda i, *_: (i, 0, 0)), pl.BlockSpec((kb, 1, n2), lambda i, *_: (i, 0, 0))]

    a_f = _s1(fc, fs, hfb.reshape(2, half, n2, w)).reshape(2, n1, 2 * n2, w)
    kh = pl.pallas_call(
        _s2f_kernel,
        out_shape=SDS((n1, 2 * n2, w), bf16),
        grid=(n1 // kb,),
        in_specs=[pl.BlockSpec((2, kb, 2 * n2, w), lambda i: (0, i, 0, 0))] + tab_specs,
        out_specs=pl.BlockSpec((kb, 2 * n2, w), lambda i: (i, 0, 0)),
        compiler_params=_cparams("parallel"),
        name="hyena_s2f",
    )(a_f, f2r, f2i, tr, ti)

    u4, x04 = u.reshape(b, half, n2, w), x0.reshape(b, half, n2, w)
    a_u = _s1(fc, fs, u4).reshape(b, n1, 2 * n2, w)
    bq = pl.pallas_call(
        _s23_kernel,
        out_shape=SDS((b, n2, 2 * n1, w), bf16),
        grid=(n1 // kb, b),
        in_specs=[pl.BlockSpec((1, kb, 2 * n2, w), lambda i, j: (j, i, 0, 0)),
                  pl.BlockSpec((kb, 2 * n2, w), lambda i, j: (i, 0, 0))] + tab_specs,
        out_specs=pl.BlockSpec((1, n2, 2 * kb, w), lambda i, j: (j, 0, i, 0)),
        compiler_params=_cparams("parallel", "parallel"),
        name="hyena_s23",
    )(a_u, kh, f2r, f2i, tr, ti)

    c4, s4 = _cs(_outer(half, n1), n1)
    f4 = (_interleave_cols(c4, -s4) * (1.0 / (n1 * n2))).astype(bf16)
    slab = pl.BlockSpec((1, half, nb, w), lambda i, j: (i, 0, j, 0))
    y = pl.pallas_call(
        _s4t_kernel,
        out_shape=SDS((b, half, n2, w), bf16),
        grid=(b, n2 // nb),
        in_specs=[_full_spec(f4.shape), pl.BlockSpec((1, nb, 2 * n1, w), lambda i, j: (i, j, 0, 0)), slab, slab,
                  _full_spec((1, w)), _full_spec((1, w))],
        out_specs=slab,
        compiler_params=_cparams("parallel", "parallel"),
        name="hyena_s4",
    )(f4, bq, u4, x04, 1.0 / l1, skip[None])
    return y.reshape(b, l, w)


def _cmul_kernel(uh_ref, hh_ref, o_ref, *, nf):
    ur, ui = uh_ref[0, :nf, :], uh_ref[0, nf:, :]
    kr = hh_ref[0, :nf, :] + hh_ref[1, :nf, :]
    ki = hh_ref[0, nf:, :] - hh_ref[1, nf:, :]
    o_ref[0, :nf, :] = (ur * kr - ui * ki).astype(o_ref.dtype)
    o_ref[0, nf:, :] = (ur * ki + ui * kr).astype(o_ref.dtype)


def _hyena_short(u, x0, hfb, l1, skip):
    b, l, w = u.shape
    nf = 2 * l
    c, s = _cs(_outer(nf, l), nf)
    fwd = jnp.concatenate([c, -s], axis=0).astype(bf16)
    uh = _lmm(fwd, u, tn=w, out_dtype=f32)
    hh = _lmm(fwd, hfb, tn=w, out_dtype=f32)
    yh = pl.pallas_call(
        functools.partial(_cmul_kernel, nf=nf),
        out_shape=SDS((b, 2 * nf, w), bf16),
        grid=(b,),
        in_specs=[pl.BlockSpec((1, 2 * nf, w), lambda i: (i, 0, 0)), _full_spec((2, 2 * nf, w))],
        out_specs=pl.BlockSpec((1, 2 * nf, w), lambda i: (i, 0, 0)),
        compiler_params=_cparams("parallel"),
        name="hyena_cmul",
    )(uh, hh)
    ci, si = _cs(_outer(l, nf), nf)
    inv = (jnp.concatenate([ci, -si], axis=1) * (1.0 / nf)).astype(bf16)
    return _s4(inv, yh, u, x0, 1.0 / l1, skip[None], tn=w)


def _f2_kernel(a_ref, f2r_ref, f2i_ref, tr_ref, ti_ref, o_ref, *, width):
    f2r, f2i = f2r_ref[...], f2i_ref[...]
    for j in range(K1_PER_STEP):
        gr, gi = _gmat(f2r, f2i, tr_ref[j], ti_ref[j])
        gcat = jnp.concatenate([gr, gi], axis=1).astype(bf16)
        s0 = a_ref[0, j, 0].astype(f32)
        s1 = a_ref[0, j, 1].astype(f32)
        x2 = jnp.concatenate([s0[:, :width] - s1[:, width:], s0[:, width:] + s1[:, :width]], axis=0).astype(bf16)
        o_ref[0, :, j, :] = _dot(gcat, x2).astype(o_ref.dtype)


def _fnet_long(pq):
    b, l, w2 = pq.shape
    w = w2 // 2
    n2 = DFT_N2
    n1 = l // n2
    kb = K1_PER_STEP
    c1, s1 = _cs(_outer(n1, n1), n1)
    a = _s1(c1.astype(bf16), s1.astype(bf16), pq.reshape(b, n1, n2, w2))
    f2r, f2i, tr, ti = _inner_tables(n1, 1.0 / math.sqrt(l))
    out = pl.pallas_call(
        functools.partial(_f2_kernel, width=w),
        out_shape=SDS((b, n2, n1, w), bf16),
        grid=(b, n1 // kb),
        in_specs=[pl.BlockSpec((1, kb, 2, n2, w2), lambda i, j: (i, j, 0, 0, 0)),
                  _full_spec((n2, n2)), _full_spec((n2, n2)),
                  pl.BlockSpec((kb, 1, n2), lambda i, j: (j, 0, 0)), pl.BlockSpec((kb, 1, n2), lambda i, j: (j, 0, 0))],
        out_specs=pl.BlockSpec((1, n2, kb, w), lambda i, j: (i, 0, j, 0)),
        compiler_params=_cparams("parallel", "parallel"),
        name="fnet_f2",
    )(a, f2r, f2i, tr, ti)
    return out.reshape(b, l, w)


def _fnet_short(pq):
    b, l, w2 = pq.shape
    w = w2 // 2
    c, s = _cs(_outer(l, l), l)
    m = (jnp.concatenate([c, -s], axis=1) * (1.0 / math.sqrt(l))).astype(bf16)
    x = jnp.concatenate([pq[..., :w], pq[..., w:]], axis=1)
    return _lmm(m, x, tn=w)


def _mm32_kernel(a_ref, b_ref, o_ref):
    o_ref[...] = _dot(a_ref[...], b_ref[...], precision=lax.Precision.HIGHEST)


def _mm32(a, b):
    return pl.pallas_call(
        _mm32_kernel,
        out_shape=SDS((a.shape[0], b.shape[1]), f32),
        in_specs=[_full_spec(a.shape), _full_spec(b.shape)],
        out_specs=_full_spec((a.shape[0], b.shape[1])),
        grid=(1,),
        compiler_params=_cparams("arbitrary"),
        name="mm32",
    )(a, b)


def _outproj_kernel(x_ref, ya_ref, yb_ref, wa_ref, wb_ref, g_ref, o_ref):
    acc = _dot(ya_ref[0], wa_ref[...]) + _dot(yb_ref[0], wb_ref[...])
    o_ref[0] = x_ref[0] + g_ref[0] * acc


def _outproj(x, ya, yb, w, gate, tm):
    b, l, d = x.shape
    ka, kb_ = ya.shape[2], yb.shape[2]
    row = lambda i, j: (i, j, 0)
    return pl.pallas_call(
        _outproj_kernel,
        out_shape=SDS((b, l, d), f32),
        grid=(b, l // tm),
        in_specs=[pl.BlockSpec((1, tm, d), row), pl.BlockSpec((1, tm, ka), row), pl.BlockSpec((1, tm, kb_), row),
                  _full_spec((ka, d)), _full_spec((kb_, d)), pl.BlockSpec((1, 1, d), lambda i, j: (i, 0, 0))],
        out_specs=pl.BlockSpec((1, tm, d), row),
        compiler_params=_cparams("parallel", "parallel"),
        name="outproj",
    )(x, ya, yb, w[:ka], w[ka:], gate)


def _ffn_kernel(x_ref, a_ref, sh_ref, g_ref, wg_ref, wu_ref, wd_ref, o_ref, h_ref, acc_ref):
    f = pl.program_id(2)

    @pl.when(f == 0)
    def _():
        h_ref[...] = _modulated(x_ref[0], a_ref[0], sh_ref[0]).astype(bf16)
        acc_ref[...] = jnp.zeros_like(acc_ref)

    h = h_ref[...]
    g = _dot(h, wg_ref[...])
    u = _dot(h, wu_ref[...])
    acc_ref[...] += _dot((g * jax.nn.sigmoid(g) * u).astype(bf16), wd_ref[...])

    @pl.when(f == pl.num_programs(2) - 1)
    def _():
        o_ref[0] = x_ref[0] + g_ref[0] * acc_ref[...]


def _ffn(x, a, sh, gate, wg, wu, wd, tm, tf):
    b, l, d = x.shape
    ff = wg.shape[1]
    row = lambda i, j, f: (i, j, 0)
    vec = lambda i, j, f: (i, 0, 0)
    return pl.pallas_call(
        _ffn_kernel,
        out_shape=SDS((b, l, d), f32),
        grid=(b, l // tm, ff // tf),
        in_specs=[pl.BlockSpec((1, tm, d), row), pl.BlockSpec((1, 1, d), vec), pl.BlockSpec((1, 1, d), vec),
                  pl.BlockSpec((1, 1, d), vec),
                  pl.BlockSpec((d, tf), lambda i, j, f: (0, f)), pl.BlockSpec((d, tf), lambda i, j, f: (0, f)),
                  pl.BlockSpec((tf, d), lambda i, j, f: (f, 0))],
        out_specs=pl.BlockSpec((1, tm, d), row),
        scratch_shapes=[pltpu.VMEM((tm, d), bf16), pltpu.VMEM((tm, d), f32)],
        compiler_params=_cparams("parallel", "parallel", "arbitrary"),
        name="ffn",
    )(x, a, sh, gate, wg, wu, wd)


def _attn_kernel(sink_ref, q_ref, kp_ref, kc_ref, kn_ref, vp_ref, vc_ref, vn_ref, ck_ref, cv_ref, o_ref):
    i, n = pl.program_id(1), pl.num_programs(1)
    tq, halo = q_ref.shape[1], kp_ref.shape[1]
    n_ctx = ck_ref.shape[1]
    n_loc = tq + 2 * halo
    r = lax.broadcasted_iota(jnp.int32, (tq, n_loc + n_ctx), 0)
    c = lax.broadcasted_iota(jnp.int32, (tq, n_loc + n_ctx), 1)
    lo = jnp.where(i > 0, 0, halo)
    hi = jnp.where(i < n - 1, n_loc, halo + tq)
    mask = ((c >= r) & (c <= r + 2 * WINDOW) & (c >= lo) & (c < hi)) | (c >= n_loc)
    keys = jnp.concatenate([kp_ref[0], kc_ref[0], kn_ref[0], ck_ref[0]], axis=0)
    vals = jnp.concatenate([vp_ref[0], vc_ref[0], vn_ref[0], cv_ref[0]], axis=0)
    q = q_ref[0]
    outs = []
    for hq in range(N_Q_HEADS):
        hk = hq // Q_PER_KV
        qh = q[:, hq * HEAD_DIM:(hq + 1) * HEAD_DIM]
        kh = keys[:, hk * HEAD_DIM:(hk + 1) * HEAD_DIM]
        vh = vals[:, hk * HEAD_DIM:(hk + 1) * HEAD_DIM]
        s = lax.dot_general(qh, kh, (((1,), (1,)), ((), ())), preferred_element_type=f32)
        s = jnp.where(mask, s, NEG_INF)
        sink = sink_ref[hq]
        m = jnp.maximum(jnp.max(s, axis=-1, keepdims=True), sink)
        p = jnp.exp(s - m)
        denom = jnp.sum(p, axis=-1, keepdims=True) + jnp.exp(sink - m)
        outs.append(_dot(p.astype(bf16), vh) / denom)
    o_ref[0] = jnp.concatenate(outs, axis=1).astype(o_ref.dtype)


def _attention(q, k, v, kc, vc, sink):
    b, l, wq = q.shape
    wk = k.shape[2]
    n_ctx = kc.shape[1]
    halo = WINDOW
    reps = 4
    tq = reps * halo
    nb = l // tq
    nh = l // halo
    cur = lambda i, j: (i, j, 0)
    prv = lambda i, j: (i, jnp.maximum(j * reps - 1, 0), 0)
    nxt = lambda i, j: (i, jnp.minimum((j + 1) * reps, nh - 1), 0)
    kv = lambda f: pl.BlockSpec((1, tq if f is cur else halo, wk), f)
    ctx = pl.BlockSpec((1, n_ctx, wk), lambda i, j: (i, 0, 0))
    return pl.pallas_call(
        _attn_kernel,
        out_shape=SDS((b, l, wq), bf16),
        grid=(b, nb),
        in_specs=[pl.BlockSpec(memory_space=pltpu.SMEM), pl.BlockSpec((1, tq, wq), cur),
                  kv(prv), kv(cur), kv(nxt), kv(prv), kv(cur), kv(nxt), ctx, ctx],
        out_specs=pl.BlockSpec((1, tq, wq), cur),
        compiler_params=_cparams("parallel", "parallel"),
        name="attention",
    )(sink.astype(f32), q, k, k, k, v, v, v, kc, vc)


COMBINE_WIN = 256
GATHER_SLOTS = 8
ROUTE_BLOCK = 256
EXPERT_ROWS = 1024
R_E1, R_E2, R_C1, R_C2, R_G1, R_G2 = range(6)


def _router_kernel(x_ref, a_ref, sh_ref, rw_ref, rb_ref, h_ref, tok_ref, rec_ref, cb_ref, tot_ref, carry_ref):
    @pl.when((pl.program_id(0) == 0) & (pl.program_id(1) == 0))
    def _():
        carry_ref[...] = jnp.zeros_like(carry_ref)

    h = _modulated(x_ref[0], a_ref[0], sh_ref[0])
    h_ref[0] = h.astype(bf16)
    h_hi = h.astype(bf16)
    h_lo = (h - h_hi.astype(f32)).astype(bf16)
    logits = _dot(h_hi, rw_ref[0]) + _dot(h_lo, rw_ref[0]) + _dot(h_hi, rw_ref[1]) + rb_ref[...]
    lane = lax.broadcasted_iota(jnp.int32, logits.shape, 1).astype(f32)
    m1 = jnp.max(logits, axis=-1, keepdims=True)
    i1 = jnp.min(jnp.where(logits == m1, lane, float(LANES)), axis=-1, keepdims=True)
    rest = jnp.where(lane == i1, NEG_INF, logits)
    m2 = jnp.max(rest, axis=-1, keepdims=True)
    i2 = jnp.min(jnp.where(rest == m2, lane, float(LANES)), axis=-1, keepdims=True)
    e2 = jnp.exp(m2 - m1)
    g1 = 1.0 / (1.0 + e2)
    sel = jnp.where((lane == i1) | (lane == i2), 1.0, 0.0)
    tm = sel.shape[0]
    tri = jnp.where(lax.broadcasted_iota(jnp.int32, (tm, tm), 0) > lax.broadcasted_iota(jnp.int32, (tm, tm), 1), 1.0, 0.0)
    cnt = _dot(tri.astype(bf16), sel.astype(bf16)) + carry_ref[...]
    c1 = jnp.sum(jnp.where(lane == i1, cnt, 0.0), axis=-1, keepdims=True)
    c2 = jnp.sum(jnp.where(lane == i2, cnt, 0.0), axis=-1, keepdims=True)
    rec = jnp.zeros_like(logits)
    for k, v in ((R_E1, i1), (R_E2, i2), (R_C1, c1), (R_C2, c2), (R_G1, g1), (R_G2, e2 * g1)):
        rec = rec + jnp.where(lane == float(k), v, 0.0)
    tok_ref[...] = rec
    rec_ref[...] = rec.T[:rec_ref.shape[0]]
    cb_ref[0] = jnp.zeros(cb_ref.shape[1:], f32)
    for s in range(tm // ROUTE_BLOCK):
        cb_ref[0, s:s + 1, :] = cnt[s * ROUTE_BLOCK:s * ROUTE_BLOCK + 1]
    total = carry_ref[...] + jnp.sum(sel, axis=0, keepdims=True)
    carry_ref[...] = total
    tot_ref[...] = total


def _router(x, a, sh, rw, rb, tm):
    b, l, d = x.shape
    e = rw.shape[1]
    nl = l // tm
    rwp = jnp.pad(rw.astype(f32), ((0, 0), (0, LANES - e)))
    rw_hi = rwp.astype(bf16)
    rwp = jnp.stack([rw_hi, (rwp - rw_hi.astype(f32)).astype(bf16)])
    rbp = jnp.pad(rb.astype(f32), (0, LANES - e), constant_values=NEG_INF)[None]
    row = lambda i, j: (i, j, 0)
    vec = lambda i, j: (i, 0, 0)
    return pl.pallas_call(
        _router_kernel,
        out_shape=[SDS((b, l, d), bf16), SDS((b * l, LANES), f32), SDS((8, b * l), f32),
                   SDS((b * nl, 8, LANES), f32), SDS((1, LANES), f32)],
        grid=(b, nl),
        in_specs=[pl.BlockSpec((1, tm, d), row), pl.BlockSpec((1, 1, d), vec), pl.BlockSpec((1, 1, d), vec),
                  _full_spec(rwp.shape), _full_spec(rbp.shape)],
        out_specs=[pl.BlockSpec((1, tm, d), row), pl.BlockSpec((tm, LANES), lambda i, j: (i * nl + j, 0)),
                   pl.BlockSpec((8, tm), lambda i, j: (0, i * nl + j)),
                   pl.BlockSpec((1, 8, LANES), lambda i, j: (i * nl + j, 0, 0)), _full_spec((1, LANES))],
        scratch_shapes=[pltpu.VMEM((1, LANES), f32)],
        compiler_params=_cparams("arbitrary", "arbitrary"),
        name="router",
    )(x, a, sh, rwp, rbp)


def _route_plan(cb, tot, n_tok, tm):
    ne, tg, tr = N_EXPERTS, ROUTE_BLOCK, EXPERT_ROWS
    i32 = jnp.int32
    cbl = cb[:, :tm // tg, :ne].reshape(-1, ne).astype(i32)
    cnt = tot[0, :ne].astype(i32)
    cbx = jnp.concatenate([cbl, cnt[None]], axis=0)
    padded = ((cnt + tr - 1) // tr) * tr
    ends = jnp.cumsum(padded)
    off = ends - padded
    n_tiles = 2 * n_tok // tr + ne
    n_act = (ends[-1] // tr).astype(i32)[None]
    owner = lambda row0: jnp.minimum(jnp.sum(ends[None, :] <= row0[:, None], axis=1), ne - 1).astype(i32)
    tile_e = owner(jnp.arange(n_tiles, dtype=i32) * tr)
    g0 = jnp.arange(n_tiles * (tr // tg), dtype=i32) * tg
    ge = owner(g0)
    rho0 = g0 - off[ge]
    cbe = cbx[:, ge]
    blo = jnp.sum(cbe[1:] <= rho0[None], axis=0)
    bhi = jnp.sum(cbe[:-1] < (rho0 + tg)[None], axis=0) - 1
    valid = (g0 < ends[-1]) & (rho0 < cnt[ge])
    nb = jnp.where(valid, jnp.maximum(bhi - blo + 1, 0), 0).astype(i32)
    blo = jnp.where(valid, blo, 0).astype(i32)
    w0 = ((off[None, :] + cbl) // COMBINE_WIN).astype(i32).reshape(-1)
    return dict(off=off.astype(i32), n_act=n_act, tile_e=tile_e, blo=blo, nb=nb, w0=w0, n_rows=n_tiles * tr)


def _offset_of(idx, off_ref):
    o = jnp.zeros_like(idx)
    for e in range(N_EXPERTS):
        o = o + jnp.where(idx == float(e), off_ref[e].astype(f32), 0.0)
    return o


def _gather_kernel(blo_ref, nb_ref, off_ref, rec_ref, h_hbm, xs_ref, gs_ref, hbuf, sem, acc_ref, gacc_ref):
    j, nj = pl.program_id(0), pl.num_programs(0)
    tg, tb = xs_ref.shape[0], hbuf.shape[2]
    half = j % 2
    nb, b0 = nb_ref[j], blo_ref[j]
    rows = (lax.broadcasted_iota(jnp.int32, (tg, 1), 0) + j * tg).astype(f32)
    acc_ref[...] = jnp.zeros_like(acc_ref)
    gacc_ref[...] = jnp.zeros_like(gacc_ref)

    def copy(first, k, hf, slot):
        return pltpu.make_async_copy(h_hbm.at[pl.ds((first + k) * tb, tb)], hbuf.at[hf, slot], sem.at[hf, slot])

    def prefetch(tile, hf):
        first, count = blo_ref[tile], nb_ref[tile]
        for s in range(GATHER_SLOTS):
            @pl.when(s < count)
            def _():
                copy(first, s, hf, s).start()

    @pl.when(j == 0)
    def _():
        prefetch(0, 0)

    @pl.when(j + 1 < nj)
    def _():
        prefetch(j + 1, 1 - half)

    def body(k, carry):
        slot = k % GATHER_SLOTS

        @pl.when(k >= GATHER_SLOTS)
        def _():
            copy(b0, k, half, slot).start()

        copy(b0, k, half, slot).wait()
        rec = rec_ref[:, pl.ds(pl.multiple_of((b0 + k) * tb, tb), tb)]
        field = lambda r: rec[r:r + 1]
        m1 = (field(R_C1) + _offset_of(field(R_E1), off_ref)) == rows
        m2 = (field(R_C2) + _offset_of(field(R_E2), off_ref)) == rows
        acc_ref[...] += _dot(jnp.where(m1 | m2, 1.0, 0.0).astype(bf16), hbuf[half, slot])
        gacc_ref[...] += jnp.sum(jnp.where(m1, field(R_G1), 0.0) + jnp.where(m2, field(R_G2), 0.0), axis=-1, keepdims=True)
        return carry

    lax.fori_loop(0, nb, body, 0)
    xs_ref[...] = acc_ref[...].astype(xs_ref.dtype)
    gs_ref[...] = gacc_ref[...]


def _gather(h, rec, plan):
    t, d = h.shape
    tg = ROUTE_BLOCK
    n_rows = plan["n_rows"]
    return pl.pallas_call(
        _gather_kernel,
        out_shape=[SDS((n_rows, d), bf16), SDS((n_rows, 1), f32)],
        grid_spec=pltpu.PrefetchScalarGridSpec(
            num_scalar_prefetch=3,
            grid=(n_rows // tg,),
            in_specs=[pl.BlockSpec(rec.shape, lambda j, *_: (0, 0)), pl.BlockSpec(memory_space=pl.ANY)],
            out_specs=[pl.BlockSpec((tg, d), lambda j, *_: (j, 0)), pl.BlockSpec((tg, 1), lambda j, *_: (j, 0))],
            scratch_shapes=[pltpu.VMEM((2, GATHER_SLOTS, tg, d), bf16), pltpu.SemaphoreType.DMA((2, GATHER_SLOTS)),
                            pltpu.VMEM((tg, d), f32), pltpu.VMEM((tg, 1), f32)],
        ),
        compiler_params=_cparams("arbitrary"),
        name="moe_gather",
    )(plan["blo"], plan["nb"], plan["off"], rec, h)


def _expert_kernel(te_ref, na_ref, xs_ref, gs_ref, wg_ref, wu_ref, wd_ref, o_ref, acc_ref):
    j, f = pl.program_id(0), pl.program_id(1)
    last = f == pl.num_programs(1) - 1
    active = j < na_ref[0]

    @pl.when(active)
    def _():
        @pl.when(f == 0)
        def _():
            acc_ref[...] = jnp.zeros_like(acc_ref)

        xs = xs_ref[...]
        g = _dot(xs, wg_ref[0].astype(bf16))
        u = _dot(xs, wu_ref[0].astype(bf16))
        acc_ref[...] += _dot((g * jax.nn.sigmoid(g) * u).astype(bf16), wd_ref[0].astype(bf16))

        @pl.when(last)
        def _():
            o_ref[...] = (acc_ref[...] * gs_ref[...]).astype(o_ref.dtype)

    @pl.when(jnp.logical_not(active) & last)
    def _():
        o_ref[...] = jnp.zeros_like(o_ref)


def _experts(xs, gs, plan, wg, wu, wd, tf):
    n_rows, d = xs.shape
    ff = wg.shape[2]
    tr = EXPERT_ROWS
    nf = ff // tf

    def live(j, na):
        return jnp.maximum(jnp.minimum(j, na[0] - 1), 0)

    def rows(j, f, te, na):
        return (live(j, na), 0)

    def w_in(j, f, te, na):
        return (te[live(j, na)], 0, jnp.where(j < na[0], f, nf - 1))

    def w_out(j, f, te, na):
        return (te[live(j, na)], jnp.where(j < na[0], f, nf - 1), 0)

    return pl.pallas_call(
        _expert_kernel,
        out_shape=SDS((n_rows, d), bf16),
        grid_spec=pltpu.PrefetchScalarGridSpec(
            num_scalar_prefetch=2,
            grid=(n_rows // tr, nf),
            in_specs=[pl.BlockSpec((tr, d), rows), pl.BlockSpec((tr, 1), rows),
                      pl.BlockSpec((1, d, tf), w_in), pl.BlockSpec((1, d, tf), w_in), pl.BlockSpec((1, tf, d), w_out)],
            out_specs=pl.BlockSpec((tr, d), lambda j, f, te, na: (j, 0)),
            scratch_shapes=[pltpu.VMEM((tr, d), f32)],
        ),
        compiler_params=_cparams("arbitrary", "arbitrary"),
        name="moe_experts",
    )(plan["tile_e"], plan["n_act"], xs, gs, wg, wu, wd)


def _combine_kernel(w0_ref, off_ref, x_ref, tok_ref, g_ref, *refs):
    ys_refs, o_ref = refs[:-1], refs[-1]
    b = pl.program_id(0)
    tb = x_ref.shape[0]
    tok = tok_ref[...]
    e1, e2 = tok[:, R_E1:R_E1 + 1], tok[:, R_E2:R_E2 + 1]
    p1 = tok[:, R_C1:R_C1 + 1] + _offset_of(e1, off_ref)
    p2 = tok[:, R_C2:R_C2 + 1] + _offset_of(e2, off_ref)
    nw = len(ys_refs) // N_EXPERTS
    lane = lax.broadcasted_iota(jnp.int32, (tb, nw * COMBINE_WIN), 1).astype(f32)
    acc = jnp.zeros(x_ref.shape, f32)
    for e in range(N_EXPERTS):
        base = (w0_ref[b * N_EXPERTS + e] * COMBINE_WIN).astype(f32)
        hit1 = jnp.where(e1 == float(e), p1 - base, -1.0) == lane
        hit2 = jnp.where(e2 == float(e), p2 - base, -1.0) == lane
        ys = jnp.concatenate([ys_refs[nw * e + s][...] for s in range(nw)], axis=0)
        acc = acc + _dot(jnp.where(hit1 | hit2, 1.0, 0.0).astype(bf16), ys)
    o_ref[...] = x_ref[...] + g_ref[0] * acc


def _combine(x, tok, gate, ys, plan, l):
    t, d = x.shape
    tb = ROUTE_BLOCK
    nrb = ys.shape[0] // COMBINE_WIN
    ne = N_EXPERTS
    nw = tb // COMBINE_WIN + 1

    def win(e, s):
        def index(i, w0, off):
            return (jnp.minimum(w0[i * ne + e] + s, nrb - 1), 0)
        return pl.BlockSpec((COMBINE_WIN, d), index)

    ys_specs = [win(e, s) for e in range(ne) for s in range(nw)]
    return pl.pallas_call(
        _combine_kernel,
        out_shape=SDS((t, d), f32),
        grid_spec=pltpu.PrefetchScalarGridSpec(
            num_scalar_prefetch=2,
            grid=(t // tb,),
            in_specs=[pl.BlockSpec((tb, d), lambda i, *_: (i, 0)), pl.BlockSpec((tb, LANES), lambda i, *_: (i, 0)),
                      pl.BlockSpec((1, 1, d), lambda i, *_: (i // (l // tb), 0, 0))] + ys_specs,
            out_specs=pl.BlockSpec((tb, d), lambda i, *_: (i, 0)),
        ),
        compiler_params=_cparams("arbitrary"),
        name="moe_combine",
    )(plan["w0"], plan["off"], x, tok, gate, *([ys] * (nw * ne)))


def _mod_params(c, c_ctx, w, b):
    nb = c.shape[0]
    cond = jnp.concatenate([c, c_ctx[None], jnp.zeros((8 - nb - 1, c.shape[1]), f32)], axis=0)
    m = _ada(cond, w, b)
    return [p[:, None, :] for p in jnp.split(m, N_MOD, axis=-1)], nb


def _even_tokens(s, a1, sh1, g1, a2, sh2, g2, w_in, hy_short, filt, hy_skip, w_out, wg, wu, wd, tm, tp, long):
    hw = hy_skip.shape[0]
    ph, pq = _inproj(s, a1, sh1, w_in, [(0, 3 * hw), (3 * hw, w_in.shape[1])], tp)
    u, x0 = _conv_call(functools.partial(_hyprep_kernel, width=hw), ph, hy_short, [hw, hw], tp, "hyena_prep")
    hfb, l1 = filt
    if long:
        y_h = _hyena_long(u, x0, hfb, l1, hy_skip)
        y_f = _fnet_long(pq)
    else:
        y_h = _hyena_short(u, x0, hfb, l1, hy_skip)
        y_f = _fnet_short(pq)
    s = _outproj(s, y_h, y_f, w_out, g1, tm)
    return _ffn(s, a2, sh2, g2, wg, wu, wd, tp, wg.shape[1] // 2)


def _rope_tables(l):
    quarter = HEAD_DIM // 4
    n_rows = l // GRID_W
    inv_freq = ROPE_BASE ** (-jnp.arange(quarter, dtype=f32) / quarter)
    ur = jnp.arange(n_rows, dtype=jnp.int32).astype(f32)[:, None] * inv_freq[None, :]
    uc = jnp.arange(GRID_W, dtype=jnp.int32).astype(f32)[:, None] * inv_freq[None, :]

    def lanes(first, second, own):
        z = jnp.zeros_like(first)
        half, other = jnp.concatenate([first, second], axis=1), jnp.concatenate([z, z], axis=1)
        head = jnp.concatenate([half, other] if own == "row" else [other, half], axis=1)
        return jnp.concatenate([head, head], axis=1)

    def tables(ang, own):
        c, s, z = jnp.cos(ang), jnp.sin(ang), jnp.zeros_like(ang)
        return jnp.stack([lanes(c, c, own), lanes(-s, z, own), lanes(z, s, own)])

    return tables(ur, "row"), tables(uc, "col")


def kernel(x, c, ctx, c_ctx, e_ada_w, e_ada_b, e_norm1, e_norm2, e_w_in, e_hy_short, e_hy_w1, e_hy_b1, e_hy_freq, e_hy_w2, e_hy_b2, e_hy_w3, e_hy_skip, e_w_out, e_ffn_wg, e_ffn_wu, e_ffn_wd, o_ada_w, o_ada_b, o_norm1, o_norm2, o_w_in, o_q_norm, o_k_norm, o_sink, o_sc_conv, o_w_out, o_router_w, o_router_b, o_moe_wg, o_moe_wu, o_moe_wd):
    b, l, d = x.shape
    n_ctx = ctx.shape[1]
    tm = 1024
    tp = 512

    (sh1, sc1, g1, sh2, sc2, g2), nb = _mod_params(c, c_ctx, e_ada_w[0], e_ada_b[0])
    a1 = e_norm1[0] * (1.0 + sc1)
    a2 = e_norm2[0] * (1.0 + sc2)
    hw = e_hy_skip.shape[1]
    fw = d - hw
    cg, sg = _cs(_outer(GROUP_DIM, GROUP_DIM), GROUP_DIM)
    eye = jnp.eye(fw // GROUP_DIM, dtype=f32)
    chan = jnp.concatenate([jnp.kron(eye, cg), jnp.kron(eye, sg)], axis=1) * (1.0 / math.sqrt(GROUP_DIM))
    w_in = jnp.concatenate([e_w_in[0][:, :3 * hw], _mm32(e_w_in[0][:, 3 * hw:], chan)], axis=1).astype(bf16)
    w_out = e_w_out[0].astype(bf16)
    wg, wu, wd = e_ffn_wg[0].astype(bf16), e_ffn_wu[0].astype(bf16), e_ffn_wd[0].astype(bf16)
    filt_args = (e_hy_w1[0], e_hy_b1[0], e_hy_freq[0], e_hy_w2[0], e_hy_b2[0], e_hy_w3[0])
    lat = lambda p: p[:nb]
    cx = lambda p: jnp.broadcast_to(p[nb:nb + 1], (b,) + p.shape[1:])
    x = _even_tokens(x, lat(a1), lat(sh1), lat(g1), lat(a2), lat(sh2), lat(g2), w_in, e_hy_short[0],
                     _hyena_filter(l, *filt_args, tl=1024), e_hy_skip[0], w_out, wg, wu, wd, tm, tp, True)
    ctx = _even_tokens(ctx, cx(a1), cx(sh1), cx(g1), cx(a2), cx(sh2), cx(g2), w_in, e_hy_short[0],
                       _hyena_filter(n_ctx, *filt_args, tl=n_ctx), e_hy_skip[0], w_out, wg, wu, wd, n_ctx, n_ctx, False)

    (sh1, sc1, g1, sh2, sc2, g2), nb = _mod_params(c, c_ctx, o_ada_w[0], o_ada_b[0])
    a1 = o_norm1[0] * (1.0 + sc1)
    a2 = o_norm2[0] * (1.0 + sc2)
    w_in = o_w_in[0].astype(bf16)
    unit = lambda rows, c: jnp.stack([jnp.full((rows, LANES), c, f32)] + [jnp.zeros((rows, LANES), f32)] * 2)
    ident = (unit(n_ctx // GRID_W, 1.0), unit(GRID_W, 0.0))
    _, kc, vc, _ = _inproj_odd(ctx, cx(a1), cx(sh1), w_in, ident, o_q_norm[0], o_k_norm[0], n_ctx)
    q, k, v, s = _inproj_odd(x, lat(a1), lat(sh1), w_in, _rope_tables(l), o_q_norm[0], o_k_norm[0], tp)
    y_att = _attention(q, k, v, kc, vc, o_sink[0])
    sw = s.shape[2] // 3
    (y_sc,) = _conv_call(functools.partial(_sgconv_kernel, width=sw), s, o_sc_conv[0], [sw], 512, "sgconv")
    x = _outproj(x, y_att, y_sc, o_w_out[0].astype(bf16), lat(g1), tm)
    h, tok, rec, cb, tot = _router(x, lat(a2), lat(sh2), o_router_w[0], o_router_b[0], tm)
    plan = _route_plan(cb, tot, b * l, tm)
    xs, gs = _gather(h.reshape(b * l, d), rec, plan)
    ys = _experts(xs, gs, plan, o_moe_wg[0], o_moe_wu[0], o_moe_wd[0], 512)
    return _combine(x.reshape(b * l, d), tok, lat(g2), ys, plan, l).reshape(b, l, d)
```

```python
import functools
import math

import jax
import jax.numpy as jnp
from jax import lax
from jax.experimental import pallas as pl
from jax.experimental.pallas import tpu as pltpu

f32 = jnp.float32
bf16 = jnp.bfloat16
SDS = jax.ShapeDtypeStruct

EPS = 1e-6
N_MOD = 6
GROUP_DIM = 64
HEAD_DIM = 64
N_Q_HEADS = 8
N_KV_HEADS = 2
Q_PER_KV = N_Q_HEADS // N_KV_HEADS
WINDOW = 128
GRID_W = 64
ROPE_BASE = 10000.0
N_EXPERTS = 8
NEG_INF = -1e30
HY_EMB_DIM = 33
HY_DECAY_TARGET = 1e-2
HY_FAST_DECAY_PCT = 0.3
HY_SLOW_DECAY_PCT = 1.5

LANES = 128
DFT_N2 = 128
K1_PER_STEP = 8
VMEM_LIMIT = 56 * 1024 * 1024


def _cparams(*sem):
    return pltpu.CompilerParams(dimension_semantics=sem, vmem_limit_bytes=VMEM_LIMIT)


def _dot(a, b, **kw):
    return jnp.dot(a, b, preferred_element_type=f32, **kw)


def _full_spec(shape):
    nd = len(shape)
    return pl.BlockSpec(shape, lambda *_: (0,) * nd)


def _ada_kernel(c_ref, w_ref, b_ref, o_ref):
    c = c_ref[...]
    s = (c * jax.nn.sigmoid(c)).astype(bf16)
    o_ref[...] = _dot(s, w_ref[...].astype(bf16)) + b_ref[...]


def _ada(cond, w, b):
    d, n = w.shape
    tn = n // 4
    return pl.pallas_call(
        _ada_kernel,
        out_shape=SDS((cond.shape[0], n), f32),
        grid=(n // tn,),
        in_specs=[
            pl.BlockSpec(cond.shape, lambda j: (0, 0)),
            pl.BlockSpec((d, tn), lambda j: (0, j)),
            pl.BlockSpec((1, tn), lambda j: (0, j)),
        ],
        out_specs=pl.BlockSpec((cond.shape[0], tn), lambda j: (0, j)),
        compiler_params=_cparams("arbitrary"),
        name="ada",
    )(cond, w, b.reshape(1, n))


def _modulated(x, a, sh):
    ms = jnp.mean(x * x, axis=-1, keepdims=True)
    return (x * lax.rsqrt(ms + EPS)) * a + sh


def _inproj_kernel(x_ref, a_ref, sh_ref, w_ref, *o_refs, splits):
    h = _modulated(x_ref[0], a_ref[0], sh_ref[0]).astype(bf16)
    for o_ref, (s, e) in zip(o_refs, splits):
        o_ref[0] = _dot(h, w_ref[:, s:e]).astype(o_ref.dtype)


def _inproj(x, a, sh, w, splits, tm):
    b, l, d = x.shape
    n = w.shape[1]
    return pl.pallas_call(
        functools.partial(_inproj_kernel, splits=splits),
        out_shape=[SDS((b, l, e - s), bf16) for s, e in splits],
        grid=(b, l // tm),
        in_specs=[
            pl.BlockSpec((1, tm, d), lambda i, j: (i, j, 0)),
            pl.BlockSpec((1, 1, d), lambda i, j: (i, 0, 0)),
            pl.BlockSpec((1, 1, d), lambda i, j: (i, 0, 0)),
            pl.BlockSpec((d, n), lambda i, j: (0, 0)),
        ],
        out_specs=[pl.BlockSpec((1, tm, e - s), lambda i, j: (i, j, 0)) for s, e in splits],
        compiler_params=_cparams("parallel", "parallel"),
        name="inproj",
    )(x, a, sh, w)


def _norm_rope(t, seg, gain, cos, sin_a, sin_b):
    ms = _dot((t * t).astype(bf16), seg)
    tn = t * lax.rsqrt(ms + EPS) * gain
    w = t.shape[1]
    reps = w // cos.shape[1]
    cos = jnp.concatenate([cos] * reps, axis=1)
    sin_a = jnp.concatenate([sin_a] * reps, axis=1)
    sin_b = jnp.concatenate([sin_b] * reps, axis=1)
    quarter = HEAD_DIM // 4
    return tn * cos + pltpu.roll(tn, w - quarter, 1) * sin_a + pltpu.roll(tn, quarter, 1) * sin_b


def _token_tables(rt_ref, ct_ref):
    g = rt_ref.shape[1]
    out = []
    for k in range(rt_ref.shape[0]):
        by_row = jnp.broadcast_to(rt_ref[k][:, None, :], (g, GRID_W, LANES)).reshape(g * GRID_W, LANES)
        by_col = jnp.concatenate([ct_ref[k]] * g, axis=0)
        out.append(by_row + by_col)
    return out


def _inproj_odd_kernel(x_ref, a_ref, sh_ref, w_ref, rt_ref, ct_ref, gq_ref, gk_ref,
                       segq_ref, segk_ref, q_ref, k_ref, v_ref, s_ref, *, wq, wk):
    h = _modulated(x_ref[0], a_ref[0], sh_ref[0]).astype(bf16)
    cos, sa, sb = _token_tables(rt_ref, ct_ref)
    q = _norm_rope(_dot(h, w_ref[:, :wq]), segq_ref[...], gq_ref[...], cos, sa, sb)
    q_ref[0] = (q * HEAD_DIM ** -0.5).astype(bf16)
    k = _norm_rope(_dot(h, w_ref[:, wq:wq + wk]), segk_ref[...], gk_ref[...], cos, sa, sb)
    k_ref[0] = k.astype(bf16)
    v_ref[0] = _dot(h, w_ref[:, wq + wk:wq + 2 * wk]).astype(bf16)
    s_ref[0] = _dot(h, w_ref[:, wq + 2 * wk:]).astype(bf16)


def _inproj_odd(x, a, sh, w, tables, gq, gk, tm):
    b, l, d = x.shape
    n = w.shape[1]
    wq, wk = N_Q_HEADS * HEAD_DIM, N_KV_HEADS * HEAD_DIM
    ws = n - wq - 2 * wk
    rt, ct = tables
    g = tm // GRID_W
    seg = lambda width: jnp.kron(jnp.eye(width // HEAD_DIM, dtype=f32),
                                 jnp.full((HEAD_DIM, HEAD_DIM), 1.0 / HEAD_DIM, f32)).astype(bf16)
    row = lambda i, j: (i, j, 0)
    return pl.pallas_call(
        functools.partial(_inproj_odd_kernel, wq=wq, wk=wk),
        out_shape=[SDS((b, l, wq), bf16), SDS((b, l, wk), bf16), SDS((b, l, wk), bf16), SDS((b, l, ws), bf16)],
        grid=(b, l // tm),
        in_specs=[
            pl.BlockSpec((1, tm, d), row),
            pl.BlockSpec((1, 1, d), lambda i, j: (i, 0, 0)),
            pl.BlockSpec((1, 1, d), lambda i, j: (i, 0, 0)),
            pl.BlockSpec((d, n), lambda i, j: (0, 0)),
            pl.BlockSpec((rt.shape[0], g, LANES), lambda i, j: (0, j, 0)), _full_spec(ct.shape),
            _full_spec((1, wq)), _full_spec((1, wk)), _full_spec((wq, wq)), _full_spec((wk, wk)),
        ],
        out_specs=[pl.BlockSpec((1, tm, wq), row), pl.BlockSpec((1, tm, wk), row),
                   pl.BlockSpec((1, tm, wk), row), pl.BlockSpec((1, tm, ws), row)],
        compiler_params=_cparams("parallel", "parallel"),
        name="inproj_odd",
    )(x, a, sh, w, rt, ct, jnp.tile(gq, wq // HEAD_DIM)[None], jnp.tile(gk, wk // HEAD_DIM)[None],
      seg(wq), seg(wk))


HALO = 16


def _conv3(p, prev_row, next_row, w):
    tl = p.shape[0]
    r = lax.broadcasted_iota(jnp.int32, p.shape, 0)
    pm1 = jnp.where(r == 0, prev_row, pltpu.roll(p, 1, 0))
    pp1 = jnp.where(r == tl - 1, next_row, pltpu.roll(p, tl - 1, 0))
    return pm1 * w[0:1] + p * w[1:2] + pp1 * w[2:3]


def _halo_rows(pp_ref, pn_ref):
    i, n = pl.program_id(1), pl.num_programs(1)
    prev = pp_ref[0].astype(f32)[HALO - 1:HALO]
    nxt = pn_ref[0].astype(f32)[0:1]
    return jnp.where(i > 0, prev, 0.0), jnp.where(i < n - 1, nxt, 0.0)


def _hyprep_kernel(p_ref, pp_ref, pn_ref, w_ref, u_ref, x0_ref, *, width):
    prev, nxt = _halo_rows(pp_ref, pn_ref)
    c = _conv3(p_ref[0].astype(f32), prev, nxt, w_ref[...])
    x0_ref[0] = c[:, :width].astype(bf16)
    u_ref[0] = (c[:, 2 * width:] * c[:, width:2 * width]).astype(bf16)


def _sgconv_kernel(s_ref, sp_ref, sn_ref, w_ref, o_ref, *, width):
    def inner(t):
        return t[:, width:2 * width] * t[:, 2 * width:]
    i, n = pl.program_id(1), pl.num_programs(1)
    s = s_ref[0].astype(f32)
    prev = jnp.where(i > 0, inner(sp_ref[0].astype(f32))[HALO - 1:HALO], 0.0)
    nxt = jnp.where(i < n - 1, inner(sn_ref[0].astype(f32))[0:1], 0.0)
    o_ref[0] = (s[:, :width] * _conv3(inner(s), prev, nxt, w_ref[...])).astype(bf16)


def _conv_call(kernel, p, w, out_widths, tl, name):
    b, l, c = p.shape
    r = tl // HALO
    nh = l // HALO
    row = lambda i, j: (i, j, 0)
    return pl.pallas_call(
        kernel,
        out_shape=[SDS((b, l, ow), bf16) for ow in out_widths],
        grid=(b, l // tl),
        in_specs=[
            pl.BlockSpec((1, tl, c), row),
            pl.BlockSpec((1, HALO, c), lambda i, j: (i, jnp.maximum(j * r - 1, 0), 0)),
            pl.BlockSpec((1, HALO, c), lambda i, j: (i, jnp.minimum((j + 1) * r, nh - 1), 0)),
            _full_spec(w.shape),
        ],
        out_specs=[pl.BlockSpec((1, tl, ow), row) for ow in out_widths],
        compiler_params=_cparams("parallel", "parallel"),
        name=name,
    )(p, p, p, w)


def _lmm_kernel(w_ref, x_ref, o_ref):
    o_ref[0] = _dot(w_ref[...], x_ref[0]).astype(o_ref.dtype)


def _lmm(w, x, tn, out_dtype=bf16):
    m, k = w.shape
    b, _, n = x.shape
    return pl.pallas_call(
        _lmm_kernel,
        out_shape=SDS((b, m, n), out_dtype),
        grid=(b, n // tn),
        in_specs=[_full_spec((m, k)), pl.BlockSpec((1, k, tn), lambda i, j: (i, 0, j))],
        out_specs=pl.BlockSpec((1, m, tn), lambda i, j: (i, 0, j)),
        compiler_params=_cparams("parallel", "parallel"),
        name="lmm",
    )(w, x)


def _filt_kernel(z_ref, t_ref, w1_ref, b1_ref, fr_ref, w2_ref, b2_ref, w3_ref, dl_ref, hfb_ref, l1_ref, *, width):
    i = pl.program_id(0)
    hp = lax.Precision.HIGHEST
    fr = fr_ref[...]
    h = jnp.sin(fr * (_dot(z_ref[...], w1_ref[...], precision=hp) + b1_ref[...]))
    h = jnp.sin(fr * (_dot(h, w2_ref[...], precision=hp) + b2_ref[...]))
    h = _dot(h.astype(bf16), w3_ref[...])
    decay = jnp.exp(-t_ref[...] * dl_ref[...])
    hf = h[:, :width] * decay
    hb = h[:, width:] * decay
    tl = hf.shape[0]
    row = lax.broadcasted_iota(jnp.int32, hb.shape, 0) + i * tl
    hb = jnp.where(row == 0, 0.0, hb)
    hfb_ref[0] = hf.astype(bf16)
    hfb_ref[1] = hb.astype(bf16)
    part = jnp.sum(jnp.abs(hf), axis=0, keepdims=True) + jnp.sum(jnp.abs(hb), axis=0, keepdims=True)

    @pl.when(i == 0)
    def _():
        l1_ref[...] = jnp.zeros_like(l1_ref)

    l1_ref[...] += part


def _hyena_filter(l, w1, b1, freq, w2, b2, w3, tl):
    hid = w1.shape[1]
    width = w3.shape[1] // 2
    t = jnp.linspace(0.0, 1.0, l, dtype=f32)[:, None]
    bands = (HY_EMB_DIM - 1) // 2
    ang = (2.0 * math.pi / l) * jnp.arange(l, dtype=f32)[:, None] * jnp.linspace(1e-4, bands - 1, bands, dtype=f32)[None, :]
    z = jnp.concatenate([t, jnp.cos(ang), -jnp.sin(ang)], axis=-1)
    z = jnp.pad(z, ((0, 0), (0, LANES - HY_EMB_DIM)))
    pad_h = LANES - hid
    w1p = jnp.pad(w1.astype(f32), ((0, LANES - HY_EMB_DIM), (0, pad_h)))
    b1p = jnp.pad(b1.astype(f32), (0, pad_h))[None]
    frp = jnp.pad(freq.astype(f32), (0, pad_h))[None]
    w2p = jnp.pad(w2.astype(f32), ((0, pad_h), (0, pad_h)))
    b2p = jnp.pad(b2.astype(f32), (0, pad_h))[None]
    w3p = jnp.pad(w3, ((0, pad_h), (0, 0))).astype(bf16)
    max_decay = math.log(HY_DECAY_TARGET) / HY_FAST_DECAY_PCT
    min_decay = math.log(HY_DECAY_TARGET) / HY_SLOW_DECAY_PCT
    deltas = jnp.abs(jnp.linspace(min_decay, max_decay, width, dtype=f32))[None]
    return pl.pallas_call(
        functools.partial(_filt_kernel, width=width),
        out_shape=[SDS((2, l, width), bf16), SDS((1, width), f32)],
        grid=(l // tl,),
        in_specs=[
            pl.BlockSpec((tl, LANES), lambda i: (i, 0)),
            pl.BlockSpec((tl, 1), lambda i: (i, 0)),
            _full_spec(w1p.shape), _full_spec(b1p.shape), _full_spec(frp.shape),
            _full_spec(w2p.shape), _full_spec(b2p.shape), _full_spec(w3p.shape), _full_spec(deltas.shape),
        ],
        out_specs=[pl.BlockSpec((2, tl, width), lambda i: (0, i, 0)), pl.BlockSpec((1, width), lambda i: (0, 0))],
        compiler_params=_cparams("arbitrary"),
        name="hyena_filter",
    )(z, t, w1p, b1p, frp, w2p, b2p, w3p, deltas)


def _cs(num, den):
    ang = (2.0 * math.pi / den) * (num % den).astype(f32)
    return jnp.cos(ang), jnp.sin(ang)


def _outer(n_rows, n_cols):
    return jnp.arange(n_rows, dtype=jnp.int32)[:, None] * jnp.arange(n_cols, dtype=jnp.int32)[None, :]


def _interleave_rows(a, b):
    return jnp.stack([a, b], axis=1).reshape(2 * a.shape[0], a.shape[1])


def _interleave_cols(a, b):
    return jnp.stack([a, b], axis=2).reshape(a.shape[0], 2 * a.shape[1])


def _inner_tables(n1, scale):
    c2, s2 = _cs(_outer(DFT_N2, DFT_N2), DFT_N2)
    ct, st = _cs(_outer(n1, DFT_N2), n1 * DFT_N2)
    return c2 * scale, -s2 * scale, ct[:, None, :], -st[:, None, :]


def _gmat(f2r, f2i, tr, ti):
    return f2r * tr - f2i * ti, f2r * ti + f2i * tr


def _gblock(gr, gi):
    return jnp.concatenate([jnp.concatenate([gr, -gi], axis=1), jnp.concatenate([gi, gr], axis=1)], axis=0)


N2_PER_STEP = 8


def _s1_kernel(fc_ref, fs_ref, x_ref, o_ref):
    fc, fs = fc_ref[...], fs_ref[...]
    for j in range(N2_PER_STEP):
        slab = x_ref[0, :, j, :]
        o_ref[0, :, 0, j, :] = _dot(fc, slab).astype(o_ref.dtype)
        o_ref[0, :, 1, j, :] = _dot(fs, slab).astype(o_ref.dtype)


def _s1(fc, fs, x):
    m, k = fc.shape
    b, _, n2, c = x.shape
    nb = N2_PER_STEP
    return pl.pallas_call(
        _s1_kernel,
        out_shape=SDS((b, m, 2, n2, c), bf16),
        grid=(b, n2 // nb),
        in_specs=[_full_spec((m, k)), _full_spec((m, k)), pl.BlockSpec((1, k, nb, c), lambda i, j: (i, 0, j, 0))],
        out_specs=pl.BlockSpec((1, m, 2, nb, c), lambda i, j: (i, 0, 0, j, 0)),
        compiler_params=_cparams("parallel", "parallel"),
        name="dft_outer",
    )(fc, fs, x)


def _s2f_kernel(a_ref, f2r_ref, f2i_ref, tr_ref, ti_ref, k_ref):
    f2r, f2i = f2r_ref[...], f2i_ref[...]
    h2 = DFT_N2
    for j in range(K1_PER_STEP):
        gr, gi = _gmat(f2r, f2i, tr_ref[j], ti_ref[j])
        g = _gblock(gr, gi).astype(bf16)
        hf, hb = _dot(g, a_ref[0, j]), _dot(g, a_ref[1, j])
        k_ref[j, :h2, :] = (hf[:h2] + hb[:h2]).astype(k_ref.dtype)
        k_ref[j, h2:, :] = (hf[h2:] - hb[h2:]).astype(k_ref.dtype)


def _s23_kernel(a_ref, kh_ref, f2r_ref, f2i_ref, tr_ref, ti_ref, o_ref):
    f2r, f2i = f2r_ref[...], f2i_ref[...]
    h2 = DFT_N2
    for j in range(K1_PER_STEP):
        gr, gi = _gmat(f2r, f2i, tr_ref[j], ti_ref[j])
        uh = _dot(_gblock(gr, gi).astype(bf16), a_ref[0, j])
        ur, ui = uh[:h2], uh[h2:]
        kr, ki = kh_ref[j, :h2, :].astype(f32), kh_ref[j, h2:, :].astype(f32)
        yh = jnp.concatenate([ur * kr - ui * ki, ur * ki + ui * kr], axis=0).astype(bf16)
        grt, git = gr.T, gi.T
        res = _dot(_gblock(grt, -git).astype(bf16), yh).astype(o_ref.dtype)
        o_ref[0, :, 2 * j, :] = res[:h2]
        o_ref[0, :, 2 * j + 1, :] = res[h2:]


def _s4t_kernel(f4_ref, bq_ref, u_ref, x0_ref, sc_ref, sk_ref, o_ref):
    f4, sc, sk = f4_ref[...], sc_ref[...], sk_ref[...]
    for j in range(N2_PER_STEP):
        y = _dot(f4, bq_ref[0, j])
        u = u_ref[0, :, j, :].astype(f32)
        o_ref[0, :, j, :] = (x0_ref[0, :, j, :].astype(f32) * (y * sc + u * sk)).astype(o_ref.dtype)


def _s4_kernel(f4_ref, bq_ref, u_ref, x0_ref, sc_ref, sk_ref, o_ref):
    y = _dot(f4_ref[...], bq_ref[0])
    u = u_ref[0].astype(f32)
    o_ref[0] = (x0_ref[0].astype(f32) * (y * sc_ref[...] + u * sk_ref[...])).astype(o_ref.dtype)


def _s4(f4, bq, u, x0, scale, skip, tn):
    m, k = f4.shape
    b, _, n = bq.shape
    reps = tn // scale.shape[1]
    col = lambda i, j: (i, 0, j)
    return pl.pallas_call(
        _s4_kernel,
        out_shape=SDS((b, m, n), bf16),
        grid=(b, n // tn),
        in_specs=[_full_spec((m, k)), pl.BlockSpec((1, k, tn), col), pl.BlockSpec((1, m, tn), col),
                  pl.BlockSpec((1, m, tn), col), _full_spec((1, tn)), _full_spec((1, tn))],
        out_specs=pl.BlockSpec((1, m, tn), col),
        compiler_params=_cparams("parallel", "parallel"),
        name="hyena_s4",
    )(f4, bq, u, x0, jnp.tile(scale, (1, reps)), jnp.tile(skip, (1, reps)))


def _hyena_long(u, x0, hfb, l1, skip):
    b, l, w = u.shape
    n2 = DFT_N2
    half = l // n2
    n1 = 2 * half
    kb = K1_PER_STEP
    nb = N2_PER_STEP
    c1, s1 = _cs(_outer(n1, half), n1)
    fc, fs = c1.astype(bf16), (-s1).astype(bf16)
    f2r, f2i, tr, ti = _inner_tables(n1, 1.0)
    tab_specs = [_full_spec((n2, n2)), _full_spec((n2, n2)),
                 pl.BlockSpec((kb, 1, n2), lambda i, *_: (i, 0, 0)), pl.BlockSpec((kb, 1, n2), lambda i, *_: (i, 0, 0))]

    a_f = _s1(fc, fs, hfb.reshape(2, half, n2, w)).reshape(2, n1, 2 * n2, w)
    kh = pl.pallas_call(
        _s2f_kernel,
        out_shape=SDS((n1, 2 * n2, w), bf16),
        grid=(n1 // kb,),
        in_specs=[pl.BlockSpec((2, kb, 2 * n2, w), lambda i: (0, i, 0, 0))] + tab_specs,
        out_specs=pl.BlockSpec((kb, 2 * n2, w), lambda i: (i, 0, 0)),
        compiler_params=_cparams("parallel"),
        name="hyena_s2f",
    )(a_f, f2r, f2i, tr, ti)

    u4, x04 = u.reshape(b, half, n2, w), x0.reshape(b, half, n2, w)
    a_u = _s1(fc, fs, u4).reshape(b, n1, 2 * n2, w)
    bq = pl.pallas_call(
        _s23_kernel,
        out_shape=SDS((b, n2, 2 * n1, w), bf16),
        grid=(n1 // kb, b),
        in_specs=[pl.BlockSpec((1, kb, 2 * n2, w), lambda i, j: (j, i, 0, 0)),
                  pl.BlockSpec((kb, 2 * n2, w), lambda i, j: (i, 0, 0))] + tab_specs,
        out_specs=pl.BlockSpec((1, n2, 2 * kb, w), lambda i, j: (j, 0, i, 0)),
        compiler_params=_cparams("parallel", "parallel"),
        name="hyena_s23",
    )(a_u, kh, f2r, f2i, tr, ti)

    c4, s4 = _cs(_outer(half, n1), n1)
    f4 = (_interleave_cols(c4, -s4) * (1.0 / (n1 * n2))).astype(bf16)
    slab = pl.BlockSpec((1, half, nb, w), lambda i, j: (i, 0, j, 0))
    y = pl.pallas_call(
        _s4t_kernel,
        out_shape=SDS((b, half, n2, w), bf16),
        grid=(b, n2 // nb),
        in_specs=[_full_spec(f4.shape), pl.BlockSpec((1, nb, 2 * n1, w), lambda i, j: (i, j, 0, 0)), slab, slab,
                  _full_spec((1, w)), _full_spec((1, w))],
        out_specs=slab,
        compiler_params=_cparams("parallel", "parallel"),
        name="hyena_s4",
    )(f4, bq, u4, x04, 1.0 / l1, skip[None])
    return y.reshape(b, l, w)


def _cmul_kernel(uh_ref, hh_ref, o_ref, *, nf):
    ur, ui = uh_ref[0, :nf, :], uh_ref[0, nf:, :]
    kr = hh_ref[0, :nf, :] + hh_ref[1, :nf, :]
    ki = hh_ref[0, nf:, :] - hh_ref[1, nf:, :]
    o_ref[0, :nf, :] = (ur * kr - ui * ki).astype(o_ref.dtype)
    o_ref[0, nf:, :] = (ur * ki + ui * kr).astype(o_ref.dtype)


def _hyena_short(u, x0, hfb, l1, skip):
    b, l, w = u.shape
    nf = 2 * l
    c, s = _cs(_outer(nf, l), nf)
    fwd = jnp.concatenate([c, -s], axis=0).astype(bf16)
    uh = _lmm(fwd, u, tn=w, out_dtype=f32)
    hh = _lmm(fwd, hfb, tn=w, out_dtype=f32)
    yh = pl.pallas_call(
        functools.partial(_cmul_kernel, nf=nf),
        out_shape=SDS((b, 2 * nf, w), bf16),
        grid=(b,),
        in_specs=[pl.BlockSpec((1, 2 * nf, w), lambda i: (i, 0, 0)), _full_spec((2, 2 * nf, w))],
        out_specs=pl.BlockSpec((1, 2 * nf, w), lambda i: (i, 0, 0)),
        compiler_params=_cparams("parallel"),
        name="hyena_cmul",
    )(uh, hh)
    ci, si = _cs(_outer(l, nf), nf)
    inv = (jnp.concatenate([ci, -si], axis=1) * (1.0 / nf)).astype(bf16)
    return _s4(inv, yh, u, x0, 1.0 / l1, skip[None], tn=w)


def _f2_kernel(a_ref, f2r_ref, f2i_ref, tr_ref, ti_ref, o_ref, *, width):
    f2r, f2i = f2r_ref[...], f2i_ref[...]
    for j in range(K1_PER_STEP):
        gr, gi = _gmat(f2r, f2i, tr_ref[j], ti_ref[j])
        gcat = jnp.concatenate([gr, gi], axis=1).astype(bf16)
        s0 = a_ref[0, j, 0].astype(f32)
        s1 = a_ref[0, j, 1].astype(f32)
        x2 = jnp.concatenate([s0[:, :width] - s1[:, width:], s0[:, width:] + s1[:, :width]], axis=0).astype(bf16)
        o_ref[0, :, j, :] = _dot(gcat, x2).astype(o_ref.dtype)


def _fnet_long(pq):
    b, l, w2 = pq.shape
    w = w2 // 2
    n2 = DFT_N2
    n1 = l // n2
    kb = K1_PER_STEP
    c1, s1 = _cs(_outer(n1, n1), n1)
    a = _s1(c1.astype(bf16), s1.astype(bf16), pq.reshape(b, n1, n2, w2))
    f2r, f2i, tr, ti = _inner_tables(n1, 1.0 / math.sqrt(l))
    out = pl.pallas_call(
        functools.partial(_f2_kernel, width=w),
        out_shape=SDS((b, n2, n1, w), bf16),
        grid=(b, n1 // kb),
        in_specs=[pl.BlockSpec((1, kb, 2, n2, w2), lambda i, j: (i, j, 0, 0, 0)),
                  _full_spec((n2, n2)), _full_spec((n2, n2)),
                  pl.BlockSpec((kb, 1, n2), lambda i, j: (j, 0, 0)), pl.BlockSpec((kb, 1, n2), lambda i, j: (j, 0, 0))],
        out_specs=pl.BlockSpec((1, n2, kb, w), lambda i, j: (i, 0, j, 0)),
        compiler_params=_cparams("parallel", "parallel"),
        name="fnet_f2",
    )(a, f2r, f2i, tr, ti)
    return out.reshape(b, l, w)


def _fnet_short(pq):
    b, l, w2 = pq.shape
    w = w2 // 2
    c, s = _cs(_outer(l, l), l)
    m = (jnp.concatenate([c, -s], axis=1) * (1.0 / math.sqrt(l))).astype(bf16)
    x = jnp.concatenate([pq[..., :w], pq[..., w:]], axis=1)
    return _lmm(m, x, tn=w)


def _mm32_kernel(a_ref, b_ref, o_ref):
    o_ref[...] = _dot(a_ref[...], b_ref[...], precision=lax.Precision.HIGHEST)


def _mm32(a, b):
    return pl.pallas_call(
        _mm32_kernel,
        out_shape=SDS((a.shape[0], b.shape[1]), f32),
        in_specs=[_full_spec(a.shape), _full_spec(b.shape)],
        out_specs=_full_spec((a.shape[0], b.shape[1])),
        grid=(1,),
        compiler_params=_cparams("arbitrary"),
        name="mm32",
    )(a, b)


def _outproj_kernel(x_ref, ya_ref, yb_ref, wa_ref, wb_ref, g_ref, o_ref):
    acc = _dot(ya_ref[0], wa_ref[...]) + _dot(yb_ref[0], wb_ref[...])
    o_ref[0] = x_ref[0] + g_ref[0] * acc


def _outproj(x, ya, yb, w, gate, tm):
    b, l, d = x.shape
    ka, kb_ = ya.shape[2], yb.shape[2]
    row = lambda i, j: (i, j, 0)
    return pl.pallas_call(
        _outproj_kernel,
        out_shape=SDS((b, l, d), f32),
        grid=(b, l // tm),
        in_specs=[pl.BlockSpec((1, tm, d), row), pl.BlockSpec((1, tm, ka), row), pl.BlockSpec((1, tm, kb_), row),
                  _full_spec((ka, d)), _full_spec((kb_, d)), pl.BlockSpec((1, 1, d), lambda i, j: (i, 0, 0))],
        out_specs=pl.BlockSpec((1, tm, d), row),
        compiler_params=_cparams("parallel", "parallel"),
        name="outproj",
    )(x, ya, yb, w[:ka], w[ka:], gate)


def _ffn_kernel(x_ref, a_ref, sh_ref, g_ref, wg_ref, wu_ref, wd_ref, o_ref, h_ref, acc_ref):
    f = pl.program_id(2)

    @pl.when(f == 0)
    def _():
        h_ref[...] = _modulated(x_ref[0], a_ref[0], sh_ref[0]).astype(bf16)
        acc_ref[...] = jnp.zeros_like(acc_ref)

    h = h_ref[...]
    g = _dot(h, wg_ref[...])
    u = _dot(h, wu_ref[...])
    acc_ref[...] += _dot((g * jax.nn.sigmoid(g) * u).astype(bf16), wd_ref[...])

    @pl.when(f == pl.num_programs(2) - 1)
    def _():
        o_ref[0] = x_ref[0] + g_ref[0] * acc_ref[...]


def _ffn(x, a, sh, gate, wg, wu, wd, tm, tf):
    b, l, d = x.shape
    ff = wg.shape[1]
    row = lambda i, j, f: (i, j, 0)
    vec = lambda i, j, f: (i, 0, 0)
    return pl.pallas_call(
        _ffn_kernel,
        out_shape=SDS((b, l, d), f32),
        grid=(b, l // tm, ff // tf),
        in_specs=[pl.BlockSpec((1, tm, d), row), pl.BlockSpec((1, 1, d), vec), pl.BlockSpec((1, 1, d), vec),
                  pl.BlockSpec((1, 1, d), vec),
                  pl.BlockSpec((d, tf), lambda i, j, f: (0, f)), pl.BlockSpec((d, tf), lambda i, j, f: (0, f)),
                  pl.BlockSpec((tf, d), lambda i, j, f: (f, 0))],
        out_specs=pl.BlockSpec((1, tm, d), row),
        scratch_shapes=[pltpu.VMEM((tm, d), bf16), pltpu.VMEM((tm, d), f32)],
        compiler_params=_cparams("parallel", "parallel", "arbitrary"),
        name="ffn",
    )(x, a, sh, gate, wg, wu, wd)


def _attn_kernel(sink_ref, q_ref, kp_ref, kc_ref, kn_ref, vp_ref, vc_ref, vn_ref, ck_ref, cv_ref, o_ref):
    i, n = pl.program_id(1), pl.num_programs(1)
    tq, halo = q_ref.shape[1], kp_ref.shape[1]
    n_ctx = ck_ref.shape[1]
    n_loc = tq + 2 * halo
    r = lax.broadcasted_iota(jnp.int32, (tq, n_loc + n_ctx), 0)
    c = lax.broadcasted_iota(jnp.int32, (tq, n_loc + n_ctx), 1)
    lo = jnp.where(i > 0, 0, halo)
    hi = jnp.where(i < n - 1, n_loc, halo + tq)
    mask = ((c >= r) & (c <= r + 2 * WINDOW) & (c >= lo) & (c < hi)) | (c >= n_loc)
    keys = jnp.concatenate([kp_ref[0], kc_ref[0], kn_ref[0], ck_ref[0]], axis=0)
    vals = jnp.concatenate([vp_ref[0], vc_ref[0], vn_ref[0], cv_ref[0]], axis=0)
    q = q_ref[0]
    outs = []
    for hq in range(N_Q_HEADS):
        hk = hq // Q_PER_KV
        qh = q[:, hq * HEAD_DIM:(hq + 1) * HEAD_DIM]
        kh = keys[:, hk * HEAD_DIM:(hk + 1) * HEAD_DIM]
        vh = vals[:, hk * HEAD_DIM:(hk + 1) * HEAD_DIM]
        s = lax.dot_general(qh, kh, (((1,), (1,)), ((), ())), preferred_element_type=f32)
        s = jnp.where(mask, s, NEG_INF)
        sink = sink_ref[hq]
        m = jnp.maximum(jnp.max(s, axis=-1, keepdims=True), sink)
        p = jnp.exp(s - m)
        denom = jnp.sum(p, axis=-1, keepdims=True) + jnp.exp(sink - m)
        outs.append(_dot(p.astype(bf16), vh) / denom)
    o_ref[0] = jnp.concatenate(outs, axis=1).astype(o_ref.dtype)


def _attention(q, k, v, kc, vc, sink):
    b, l, wq = q.shape
    wk = k.shape[2]
    n_ctx = kc.shape[1]
    halo = WINDOW
    reps = 8
    tq = reps * halo
    nb = l // tq
    nh = l // halo
    cur = lambda i, j: (i, j, 0)
    prv = lambda i, j: (i, jnp.maximum(j * reps - 1, 0), 0)
    nxt = lambda i, j: (i, jnp.minimum((j + 1) * reps, nh - 1), 0)
    kv = lambda f: pl.BlockSpec((1, tq if f is cur else halo, wk), f)
    ctx = pl.BlockSpec((1, n_ctx, wk), lambda i, j: (i, 0, 0))
    return pl.pallas_call(
        _attn_kernel,
        out_shape=SDS((b, l, wq), bf16),
        grid=(b, nb),
        in_specs=[pl.BlockSpec(memory_space=pltpu.SMEM), pl.BlockSpec((1, tq, wq), cur),
                  kv(prv), kv(cur), kv(nxt), kv(prv), kv(cur), kv(nxt), ctx, ctx],
        out_specs=pl.BlockSpec((1, tq, wq), cur),
        compiler_params=_cparams("parallel", "parallel"),
        name="attention",
    )(sink.astype(f32), q, k, k, k, v, v, v, kc, vc)


COMBINE_WIN = 256
GATHER_SLOTS = 8
ROUTE_BLOCK = 256
EXPERT_ROWS = 1024
R_E1, R_E2, R_C1, R_C2, R_G1, R_G2 = range(6)


def _router_kernel(x_ref, a_ref, sh_ref, rw_ref, rb_ref, h_ref, tok_ref, rec_ref, cb_ref, tot_ref, carry_ref):
    @pl.when((pl.program_id(0) == 0) & (pl.program_id(1) == 0))
    def _():
        carry_ref[...] = jnp.zeros_like(carry_ref)

    h = _modulated(x_ref[0], a_ref[0], sh_ref[0])
    h_ref[0] = h.astype(bf16)
    h_hi = h.astype(bf16)
    h_lo = (h - h_hi.astype(f32)).astype(bf16)
    logits = _dot(h_hi, rw_ref[0]) + _dot(h_lo, rw_ref[0]) + _dot(h_hi, rw_ref[1]) + rb_ref[...]
    lane = lax.broadcasted_iota(jnp.int32, logits.shape, 1).astype(f32)
    m1 = jnp.max(logits, axis=-1, keepdims=True)
    i1 = jnp.min(jnp.where(logits == m1, lane, float(LANES)), axis=-1, keepdims=True)
    rest = jnp.where(lane == i1, NEG_INF, logits)
    m2 = jnp.max(rest, axis=-1, keepdims=True)
    i2 = jnp.min(jnp.where(rest == m2, lane, float(LANES)), axis=-1, keepdims=True)
    e2 = jnp.exp(m2 - m1)
    g1 = 1.0 / (1.0 + e2)
    sel = jnp.where((lane == i1) | (lane == i2), 1.0, 0.0)
    tm = sel.shape[0]
    tri = jnp.where(lax.broadcasted_iota(jnp.int32, (tm, tm), 0) > lax.broadcasted_iota(jnp.int32, (tm, tm), 1), 1.0, 0.0)
    cnt = _dot(tri.astype(bf16), sel.astype(bf16)) + carry_ref[...]
    c1 = jnp.sum(jnp.where(lane == i1, cnt, 0.0), axis=-1, keepdims=True)
    c2 = jnp.sum(jnp.where(lane == i2, cnt, 0.0), axis=-1, keepdims=True)
    rec = jnp.zeros_like(logits)
    for k, v in ((R_E1, i1), (R_E2, i2), (R_C1, c1), (R_C2, c2), (R_G1, g1), (R_G2, e2 * g1)):
        rec = rec + jnp.where(lane == float(k), v, 0.0)
    tok_ref[...] = rec
    rec_ref[...] = rec.T[:rec_ref.shape[0]]
    cb_ref[0] = jnp.zeros(cb_ref.shape[1:], f32)
    for s in range(tm // ROUTE_BLOCK):
        cb_ref[0, s:s + 1, :] = cnt[s * ROUTE_BLOCK:s * ROUTE_BLOCK + 1]
    total = carry_ref[...] + jnp.sum(sel, axis=0, keepdims=True)
    carry_ref[...] = total
    tot_ref[...] = total


def _router(x, a, sh, rw, rb, tm):
    b, l, d = x.shape
    e = rw.shape[1]
    nl = l // tm
    rwp = jnp.pad(rw.astype(f32), ((0, 0), (0, LANES - e)))
    rw_hi = rwp.astype(bf16)
    rwp = jnp.stack([rw_hi, (rwp - rw_hi.astype(f32)).astype(bf16)])
    rbp = jnp.pad(rb.astype(f32), (0, LANES - e), constant_values=NEG_INF)[None]
    row = lambda i, j: (i, j, 0)
    vec = lambda i, j: (i, 0, 0)
    return pl.pallas_call(
        _router_kernel,
        out_shape=[SDS((b, l, d), bf16), SDS((b * l, LANES), f32), SDS((8, b * l), f32),
                   SDS((b * nl, 8, LANES), f32), SDS((1, LANES), f32)],
        grid=(b, nl),
        in_specs=[pl.BlockSpec((1, tm, d), row), pl.BlockSpec((1, 1, d), vec), pl.BlockSpec((1, 1, d), vec),
                  _full_spec(rwp.shape), _full_spec(rbp.shape)],
        out_specs=[pl.BlockSpec((1, tm, d), row), pl.BlockSpec((tm, LANES), lambda i, j: (i * nl + j, 0)),
                   pl.BlockSpec((8, tm), lambda i, j: (0, i * nl + j)),
                   pl.BlockSpec((1, 8, LANES), lambda i, j: (i * nl + j, 0, 0)), _full_spec((1, LANES))],
        scratch_shapes=[pltpu.VMEM((1, LANES), f32)],
        compiler_params=_cparams("arbitrary", "arbitrary"),
        name="router",
    )(x, a, sh, rwp, rbp)


def _route_plan(cb, tot, n_tok, tm):
    ne, tg, tr = N_EXPERTS, ROUTE_BLOCK, EXPERT_ROWS
    i32 = jnp.int32
    cbl = cb[:, :tm // tg, :ne].reshape(-1, ne).astype(i32)
    cnt = tot[0, :ne].astype(i32)
    cbx = jnp.concatenate([cbl, cnt[None]], axis=0)
    padded = ((cnt + tr - 1) // tr) * tr
    ends = jnp.cumsum(padded)
    off = ends - padded
    n_tiles = 2 * n_tok // tr + ne
    n_act = (ends[-1] // tr).astype(i32)[None]
    owner = lambda row0: jnp.minimum(jnp.sum(ends[None, :] <= row0[:, None], axis=1), ne - 1).astype(i32)
    tile_e = owner(jnp.arange(n_tiles, dtype=i32) * tr)
    g0 = jnp.arange(n_tiles * (tr // tg), dtype=i32) * tg
    ge = owner(g0)
    rho0 = g0 - off[ge]
    cbe = cbx[:, ge]
    blo = jnp.sum(cbe[1:] <= rho0[None], axis=0)
    bhi = jnp.sum(cbe[:-1] < (rho0 + tg)[None], axis=0) - 1
    valid = (g0 < ends[-1]) & (rho0 < cnt[ge])
    nb = jnp.where(valid, jnp.maximum(bhi - blo + 1, 0), 0).astype(i32)
    blo = jnp.where(valid, blo, 0).astype(i32)
    w0 = ((off[None, :] + cbl) // COMBINE_WIN).astype(i32).reshape(-1)
    return dict(off=off.astype(i32), n_act=n_act, tile_e=tile_e, blo=blo, nb=nb, w0=w0, n_rows=n_tiles * tr)


def _offset_of(idx, off_ref):
    o = jnp.zeros_like(idx)
    for e in range(N_EXPERTS):
        o = o + jnp.where(idx == float(e), off_ref[e].astype(f32), 0.0)
    return o


def _gather_kernel(blo_ref, nb_ref, off_ref, rec_ref, h_hbm, xs_ref, gs_ref, hbuf, sem, acc_ref, gacc_ref):
    j, nj = pl.program_id(0), pl.num_programs(0)
    tg, tb = xs_ref.shape[0], hbuf.shape[2]
    half = j % 2
    nb, b0 = nb_ref[j], blo_ref[j]
    rows = (lax.broadcasted_iota(jnp.int32, (tg, 1), 0) + j * tg).astype(f32)
    acc_ref[...] = jnp.zeros_like(acc_ref)
    gacc_ref[...] = jnp.zeros_like(gacc_ref)

    def copy(first, k, hf, slot):
        return pltpu.make_async_copy(h_hbm.at[pl.ds((first + k) * tb, tb)], hbuf.at[hf, slot], sem.at[hf, slot])

    def prefetch(tile, hf):
        first, count = blo_ref[tile], nb_ref[tile]
        for s in range(GATHER_SLOTS):
            @pl.when(s < count)
            def _():
                copy(first, s, hf, s).start()

    @pl.when(j == 0)
    def _():
        prefetch(0, 0)

    @pl.when(j + 1 < nj)
    def _():
        prefetch(j + 1, 1 - half)

    def body(k, carry):
        slot = k % GATHER_SLOTS

        @pl.when(k >= GATHER_SLOTS)
        def _():
            copy(b0, k, half, slot).start()

        copy(b0, k, half, slot).wait()
        rec = rec_ref[:, pl.ds(pl.multiple_of((b0 + k) * tb, tb), tb)]
        field = lambda r: rec[r:r + 1]
        m1 = (field(R_C1) + _offset_of(field(R_E1), off_ref)) == rows
        m2 = (field(R_C2) + _offset_of(field(R_E2), off_ref)) == rows
        acc_ref[...] += _dot(jnp.where(m1 | m2, 1.0, 0.0).astype(bf16), hbuf[half, slot])
        gacc_ref[...] += jnp.sum(jnp.where(m1, field(R_G1), 0.0) + jnp.where(m2, field(R_G2), 0.0), axis=-1, keepdims=True)
        return carry

    lax.fori_loop(0, nb, body, 0)
    xs_ref[...] = acc_ref[...].astype(xs_ref.dtype)
    gs_ref[...] = gacc_ref[...]


def _gather(h, rec, plan):
    t, d = h.shape
    tg = ROUTE_BLOCK
    n_rows = plan["n_rows"]
    return pl.pallas_call(
        _gather_kernel,
        out_shape=[SDS((n_rows, d), bf16), SDS((n_rows, 1), f32)],
        grid_spec=pltpu.PrefetchScalarGridSpec(
            num_scalar_prefetch=3,
            grid=(n_rows // tg,),
            in_specs=[pl.BlockSpec(rec.shape, lambda j, *_: (0, 0)), pl.BlockSpec(memory_space=pl.ANY)],
            out_specs=[pl.BlockSpec((tg, d), lambda j, *_: (j, 0)), pl.BlockSpec((tg, 1), lambda j, *_: (j, 0))],
            scratch_shapes=[pltpu.VMEM((2, GATHER_SLOTS, tg, d), bf16), pltpu.SemaphoreType.DMA((2, GATHER_SLOTS)),
                            pltpu.VMEM((tg, d), f32), pltpu.VMEM((tg, 1), f32)],
        ),
        compiler_params=_cparams("arbitrary"),
        name="moe_gather",
    )(plan["blo"], plan["nb"], plan["off"], rec, h)


def _expert_kernel(te_ref, na_ref, xs_ref, gs_ref, wg_ref, wu_ref, wd_ref, o_ref, acc_ref):
    j, f = pl.program_id(0), pl.program_id(1)
    last = f == pl.num_programs(1) - 1
    active = j < na_ref[0]

    @pl.when(active)
    def _():
        @pl.when(f == 0)
        def _():
            acc_ref[...] = jnp.zeros_like(acc_ref)

        xs = xs_ref[...]
        g = _dot(xs, wg_ref[0].astype(bf16))
        u = _dot(xs, wu_ref[0].astype(bf16))
        acc_ref[...] += _dot((g * jax.nn.sigmoid(g) * u).astype(bf16), wd_ref[0].astype(bf16))

        @pl.when(last)
        def _():
            o_ref[...] = (acc_ref[...] * gs_ref[...]).astype(o_ref.dtype)

    @pl.when(jnp.logical_not(active) & last)
    def _():
        o_ref[...] = jnp.zeros_like(o_ref)


def _experts(xs, gs, plan, wg, wu, wd, tf):
    n_rows, d = xs.shape
    ff = wg.shape[2]
    tr = EXPERT_ROWS
    nf = ff // tf

    def live(j, na):
        return jnp.maximum(jnp.minimum(j, na[0] - 1), 0)

    def rows(j, f, te, na):
        return (live(j, na), 0)

    def w_in(j, f, te, na):
        return (te[live(j, na)], 0, jnp.where(j < na[0], f, nf - 1))

    def w_out(j, f, te, na):
        return (te[live(j, na)], jnp.where(j < na[0], f, nf - 1), 0)

    return pl.pallas_call(
        _expert_kernel,
        out_shape=SDS((n_rows, d), bf16),
        grid_spec=pltpu.PrefetchScalarGridSpec(
            num_scalar_prefetch=2,
            grid=(n_rows // tr, nf),
            in_specs=[pl.BlockSpec((tr, d), rows), pl.BlockSpec((tr, 1), rows),
                      pl.BlockSpec((1, d, tf), w_in), pl.BlockSpec((1, d, tf), w_in), pl.BlockSpec((1, tf, d), w_out)],
            out_specs=pl.BlockSpec((tr, d), lambda j, f, te, na: (j, 0)),
            scratch_shapes=[pltpu.VMEM((tr, d), f32)],
        ),
        compiler_params=_cparams("arbitrary", "arbitrary"),
        name="moe_experts",
    )(plan["tile_e"], plan["n_act"], xs, gs, wg, wu, wd)


def _combine_kernel(w0_ref, off_ref, x_ref, tok_ref, g_ref, *refs):
    ys_refs, o_ref = refs[:-1], refs[-1]
    b = pl.program_id(0)
    tb = x_ref.shape[0]
    tok = tok_ref[...]
    e1, e2 = tok[:, R_E1:R_E1 + 1], tok[:, R_E2:R_E2 + 1]
    p1 = tok[:, R_C1:R_C1 + 1] + _offset_of(e1, off_ref)
    p2 = tok[:, R_C2:R_C2 + 1] + _offset_of(e2, off_ref)
    nw = len(ys_refs) // N_EXPERTS
    lane = lax.broadcasted_iota(jnp.int32, (tb, nw * COMBINE_WIN), 1).astype(f32)
    acc = jnp.zeros(x_ref.shape, f32)
    for e in range(N_EXPERTS):
        base = (w0_ref[b * N_EXPERTS + e] * COMBINE_WIN).astype(f32)
        hit1 = jnp.where(e1 == float(e), p1 - base, -1.0) == lane
        hit2 = jnp.where(e2 == float(e), p2 - base, -1.0) == lane
        ys = jnp.concatenate([ys_refs[nw * e + s][...] for s in range(nw)], axis=0)
        acc = acc + _dot(jnp.where(hit1 | hit2, 1.0, 0.0).astype(bf16), ys)
    o_ref[...] = x_ref[...] + g_ref[0] * acc


def _combine(x, tok, gate, ys, plan, l):
    t, d = x.shape
    tb = ROUTE_BLOCK
    nrb = ys.shape[0] // COMBINE_WIN
    ne = N_EXPERTS
    nw = tb // COMBINE_WIN + 1

    def win(e, s):
        def index(i, w0, off):
            return (jnp.minimum(w0[i * ne + e] + s, nrb - 1), 0)
        return pl.BlockSpec((COMBINE_WIN, d), index)

    ys_specs = [win(e, s) for e in range(ne) for s in range(nw)]
    return pl.pallas_call(
        _combine_kernel,
        out_shape=SDS((t, d), f32),
        grid_spec=pltpu.PrefetchScalarGridSpec(
            num_scalar_prefetch=2,
            grid=(t // tb,),
            in_specs=[pl.BlockSpec((tb, d), lambda i, *_: (i, 0)), pl.BlockSpec((tb, LANES), lambda i, *_: (i, 0)),
                      pl.BlockSpec((1, 1, d), lambda i, *_: (i // (l // tb), 0, 0))] + ys_specs,
            out_specs=pl.BlockSpec((tb, d), lambda i, *_: (i, 0)),
        ),
        compiler_params=_cparams("arbitrary"),
        name="moe_combine",
    )(plan["w0"], plan["off"], x, tok, gate, *([ys] * (nw * ne)))


def _mod_params(c, c_ctx, w, b):
    nb = c.shape[0]
    cond = jnp.concatenate([c, c_ctx[None], jnp.zeros((8 - nb - 1, c.shape[1]), f32)], axis=0)
    m = _ada(cond, w, b)
    return [p[:, None, :] for p in jnp.split(m, N_MOD, axis=-1)], nb


def _even_tokens(s, a1, sh1, g1, a2, sh2, g2, w_in, hy_short, filt, hy_skip, w_out, wg, wu, wd, tm, tp, long):
    hw = hy_skip.shape[0]
    ph, pq = _inproj(s, a1, sh1, w_in, [(0, 3 * hw), (3 * hw, w_in.shape[1])], tp)
    u, x0 = _conv_call(functools.partial(_hyprep_kernel, width=hw), ph, hy_short, [hw, hw], tp, "hyena_prep")
    hfb, l1 = filt
    if long:
        y_h = _hyena_long(u, x0, hfb, l1, hy_skip)
        y_f = _fnet_long(pq)
    else:
        y_h = _hyena_short(u, x0, hfb, l1, hy_skip)
        y_f = _fnet_short(pq)
    s = _outproj(s, y_h, y_f, w_out, g1, tm)
    return _ffn(s, a2, sh2, g2, wg, wu, wd, tp, wg.shape[1] // 2)


def _rope_tables(l):
    quarter = HEAD_DIM // 4
    n_rows = l // GRID_W
    inv_freq = ROPE_BASE ** (-jnp.arange(quarter, dtype=f32) / quarter)
    ur = jnp.arange(n_rows, dtype=jnp.int32).astype(f32)[:, None] * inv_freq[None, :]
    uc = jnp.arange(GRID_W, dtype=jnp.int32).astype(f32)[:, None] * inv_freq[None, :]

    def lanes(first, second, own):
        z = jnp.zeros_like(first)
        half, other = jnp.concatenate([first, second], axis=1), jnp.concatenate([z, z], axis=1)
        head = jnp.concatenate([half, other] if own == "row" else [other, half], axis=1)
        return jnp.concatenate([head, head], axis=1)

    def tables(ang, own):
        c, s, z = jnp.cos(ang), jnp.sin(ang), jnp.zeros_like(ang)
        return jnp.stack([lanes(c, c, own), lanes(-s, z, own), lanes(z, s, own)])

    return tables(ur, "row"), tables(uc, "col")


def kernel(x, c, ctx, c_ctx, e_ada_w, e_ada_b, e_norm1, e_norm2, e_w_in, e_hy_short, e_hy_w1, e_hy_b1, e_hy_freq, e_hy_w2, e_hy_b2, e_hy_w3, e_hy_skip, e_w_out, e_ffn_wg, e_ffn_wu, e_ffn_wd, o_ada_w, o_ada_b, o_norm1, o_norm2, o_w_in, o_q_norm, o_k_norm, o_sink, o_sc_conv, o_w_out, o_router_w, o_router_b, o_moe_wg, o_moe_wu, o_moe_wd):
    b, l, d = x.shape
    n_ctx = ctx.shape[1]
    tm = 1024
    tp = 512

    (sh1, sc1, g1, sh2, sc2, g2), nb = _mod_params(c, c_ctx, e_ada_w[0], e_ada_b[0])
    a1 = e_norm1[0] * (1.0 + sc1)
    a2 = e_norm2[0] * (1.0 + sc2)
    hw = e_hy_skip.shape[1]
    fw = d - hw
    cg, sg = _cs(_outer(GROUP_DIM, GROUP_DIM), GROUP_DIM)
    eye = jnp.eye(fw // GROUP_DIM, dtype=f32)
    chan = jnp.concatenate([jnp.kron(eye, cg), jnp.kron(eye, sg)], axis=1) * (1.0 / math.sqrt(GROUP_DIM))
    w_in = jnp.concatenate([e_w_in[0][:, :3 * hw], _mm32(e_w_in[0][:, 3 * hw:], chan)], axis=1).astype(bf16)
    w_out = e_w_out[0].astype(bf16)
    wg, wu, wd = e_ffn_wg[0].astype(bf16), e_ffn_wu[0].astype(bf16), e_ffn_wd[0].astype(bf16)
    filt_args = (e_hy_w1[0], e_hy_b1[0], e_hy_freq[0], e_hy_w2[0], e_hy_b2[0], e_hy_w3[0])
    lat = lambda p: p[:nb]
    cx = lambda p: jnp.broadcast_to(p[nb:nb + 1], (b,) + p.shape[1:])
    x = _even_tokens(x, lat(a1), lat(sh1), lat(g1), lat(a2), lat(sh2), lat(g2), w_in, e_hy_short[0],
                     _hyena_filter(l, *filt_args, tl=1024), e_hy_skip[0], w_out, wg, wu, wd, tm, tp, True)
    ctx = _even_tokens(ctx, cx(a1), cx(sh1), cx(g1), cx(a2), cx(sh2), cx(g2), w_in, e_hy_short[0],
                       _hyena_filter(n_ctx, *filt_args, tl=n_ctx), e_hy_skip[0], w_out, wg, wu, wd, n_ctx, n_ctx, False)

    (sh1, sc1, g1, sh2, sc2, g2), nb = _mod_params(c, c_ctx, o_ada_w[0], o_ada_b[0])
    a1 = o_norm1[0] * (1.0 + sc1)
    a2 = o_norm2[0] * (1.0 + sc2)
    w_in = o_w_in[0].astype(bf16)
    unit = lambda rows, c: jnp.stack([jnp.full((rows, LANES), c, f32)] + [jnp.zeros((rows, LANES), f32)] * 2)
    ident = (unit(n_ctx // GRID_W, 1.0), unit(GRID_W, 0.0))
    _, kc, vc, _ = _inproj_odd(ctx, cx(a1), cx(sh1), w_in, ident, o_q_norm[0], o_k_norm[0], n_ctx)
    q, k, v, s = _inproj_odd(x, lat(a1), lat(sh1), w_in, _rope_tables(l), o_q_norm[0], o_k_norm[0], tp)
    y_att = _attention(q, k, v, kc, vc, o_sink[0])
    sw = s.shape[2] // 3
    (y_sc,) = _conv_call(functools.partial(_sgconv_kernel, width=sw), s, o_sc_conv[0], [sw], 512, "sgconv")
    x = _outproj(x, y_att, y_sc, o_w_out[0].astype(bf16), lat(g1), tm)
    h, tok, rec, cb, tot = _router(x, lat(a2), lat(sh2), o_router_w[0], o_router_b[0], tm)
    plan = _route_plan(cb, tot, b * l, tm)
    xs, gs = _gather(h.reshape(b * l, d), rec, plan)
    ys = _experts(xs, gs, plan, o_moe_wg[0], o_moe_wu[0], o_moe_wd[0], 512)
    return _combine(x.reshape(b * l, d), tok, lat(g2), ys, plan, l).reshape(b, l, d)
```
